```python
import math
import jax, jax.numpy as jnp
from jax import lax
import numpy as np

D_MODEL = 2048
BATCH = 2
SEQ = 4096
DEPTH = 2

N_A_LAYERS = DEPTH // 2
N_B_LAYERS = DEPTH - N_A_LAYERS
GLA_HEADS = 4
GLA_KEY_DIM = D_MODEL // 2
GLA_VAL_DIM = D_MODEL
GLA_DK = GLA_KEY_DIM // GLA_HEADS
GLA_DV = GLA_VAL_DIM // GLA_HEADS
GATE_RANK = 16
GATE_NORMALIZER = 16.0
GLA_CHUNK = 64
GLA_IN_DIM = 2 * GLA_KEY_DIM + 2 * GLA_VAL_DIM + GATE_RANK
ATT_HEADS = 16
HEAD_DIM = D_MODEL // ATT_HEADS
WINDOWS = (128, 512, 2048)
DILATIONS = (1, 4, 16)
N_BRANCH = 3
ATT_BLOCK = 128
D_FF = 5632
CONV_WIDTH = 3
EPS = 1e-6

kernel_name = "yoco_gla_dilated_swa_convglu"


def rmsnorm(x, g):
    x32 = x.astype(jnp.float32)
    y = x32 * lax.rsqrt(jnp.mean(x32 * x32, axis=-1, keepdims=True) + EPS)
    return (y * g.astype(jnp.float32)).astype(x.dtype)


def alibi_slopes(n):
    def pow2_slopes(m):
        start = 2.0 ** (-8.0 / m)
        return [start ** (i + 1) for i in range(m)]
    if math.log2(n).is_integer():
        s = pow2_slopes(n)
    else:
        c = 2 ** math.floor(math.log2(n))
        s = pow2_slopes(c) + pow2_slopes(2 * c)[0::2][: n - c]
    return jnp.asarray(np.array(s, dtype=np.float32))


def gla_mixer(h, w_in, w_a2, b_a2, head_norm, w_out):
    bsz, s_len, _ = h.shape
    n_chunks = s_len // GLA_CHUNK
    f32 = jnp.float32
    proj = h @ w_in
    q, k, v, r, a = jnp.split(
        proj, [GLA_KEY_DIM, 2 * GLA_KEY_DIM, 2 * GLA_KEY_DIM + GLA_VAL_DIM,
               2 * GLA_KEY_DIM + 2 * GLA_VAL_DIM], axis=-1)
    log_alpha = jax.nn.log_sigmoid((a @ w_a2 + b_a2).astype(f32)) / GATE_NORMALIZER

    def chunks(t, hd):
        return t.astype(f32).reshape(bsz, n_chunks, GLA_CHUNK, GLA_HEADS, hd).transpose(1, 0, 3, 2, 4)

    qc = chunks(q, GLA_DK) * (GLA_DK ** -0.5)
    kc = chunks(k, GLA_DK)
    vc = chunks(v, GLA_DV)
    cum = jnp.cumsum(chunks(log_alpha, GLA_DK), axis=3)
    last = cum[:, :, :, -1:, :]
    q_dec = qc * jnp.exp(cum)
    k_inv = kc * jnp.exp(-cum)
    k_to_end = kc * jnp.exp(last - cum)

    causal = jnp.tril(jnp.ones((GLA_CHUNK, GLA_CHUNK), dtype=bool))
    scores = jnp.where(causal, jnp.einsum('nbhtk,nbhsk->nbhts', q_dec, k_inv), 0.0)
    o_intra = jnp.einsum('nbhts,nbhsv->nbhtv', scores, vc)

    def step(state, xs):
        q_n, k_n, v_n, dec_n = xs
        o_n = jnp.einsum('bhtk,bhkv->bhtv', q_n, state)
        state = state * dec_n[..., None] + jnp.einsum('bhsk,bhsv->bhkv', k_n, v_n)
        return state, o_n

    state0 = jnp.zeros((bsz, GLA_HEADS, GLA_DK, GLA_DV), f32)
    _, o_inter = lax.scan(step, state0, (q_dec, k_to_end, vc, jnp.exp(last[:, :, :, 0, :])))
    o = (o_intra + o_inter).transpose(1, 0, 3, 2, 4).reshape(bsz, s_len, GLA_HEADS, GLA_DV)
    o = rmsnorm(o, head_norm)
    gate = jax.nn.silu(r.astype(f32)).reshape(bsz, s_len, GLA_HEADS, GLA_DV)
    o = (o * gate).reshape(bsz, s_len, GLA_VAL_DIM).astype(h.dtype)
    return o @ w_out


def to_dilated(t, d):
    bsz, s_len, nh, e = t.shape
    return t.reshape(bsz, s_len // d, d, nh, e).transpose(0, 2, 1, 3, 4)


def n_blocks(sub_len):
    return -(-sub_len // ATT_BLOCK)


def to_query_blocks(t, d):
    td = to_dilated(t, d)
    bsz, _, sub_len, nh, e = td.shape
    nb = n_blocks(sub_len)
    td = jnp.pad(td, ((0, 0), (0, 0), (0, nb * ATT_BLOCK - sub_len), (0, 0), (0, 0)))
    return td.reshape(bsz, d, nb, ATT_BLOCK, nh, e)


def to_key_blocks(t, d):
    td = to_dilated(t, d)
    bsz, _, sub_len, nh, e = td.shape
    nb = n_blocks(sub_len)
    td = jnp.pad(td, ((0, 0), (0, 0), (ATT_BLOCK, nb * ATT_BLOCK - sub_len), (0, 0), (0, 0)))
    return td.reshape(bsz, d, nb + 1, ATT_BLOCK, nh, e)


def from_blocks(t, d, s_len):
    bsz, _, nb, _, nh, e = t.shape
    t = t.reshape(bsz, d, nb * ATT_BLOCK, nh, e)[:, :, : s_len // d]
    return t.transpose(0, 2, 1, 3, 4).reshape(bsz, s_len, nh, e)


def shared_kv(h, kv_norm, w_kv):
    bsz, s_len, _ = h.shape
    kv = rmsnorm(h, kv_norm) @ w_kv
    k, v = jnp.split(kv, 2, axis=-1)
    k = k.reshape(bsz, s_len, ATT_HEADS, HEAD_DIM)
    v = v.reshape(bsz, s_len, ATT_HEADS, HEAD_DIM)
    return [(to_key_blocks(k, d), to_key_blocks(v, d)) for d in DILATIONS]


def dilated_branch(qb, kb, vb, d, keys_back, slopes):
    nb = qb.shape[2]
    s_prev = jnp.einsum('brnqhe,brnkhe->brnhqk', qb, kb[:, :, :-1])
    s_cur = jnp.einsum('brnqhe,brnkhe->brnhqk', qb, kb[:, :, 1:])
    s = jnp.concatenate([s_prev, s_cur], axis=-1).astype(jnp.float32) * (HEAD_DIM ** -0.5)
    qa = jnp.arange(ATT_BLOCK)
    kc = jnp.arange(2 * ATT_BLOCK)
    j = qa[:, None] - kc[None, :] + ATT_BLOCK
    key_sub = jnp.arange(nb)[:, None] * ATT_BLOCK - ATT_BLOCK + kc[None, :]
    valid = ((j >= 0) & (j <= keys_back))[None] & (key_sub >= 0)[:, None, :]
    alibi = -slopes[:, None, None] * (j * d).astype(jnp.float32)[None]
    s = jnp.where(valid[None, None, :, None], s + alibi[None, None, None], -jnp.inf)
    m = jnp.max(s, axis=-1, keepdims=True)
    p = jnp.exp(s - m)
    l = jnp.sum(p, axis=-1, keepdims=True)
    o = (jnp.einsum('brnhqk,brnkhe->brnqhe', p[..., :ATT_BLOCK], vb[:, :, :-1])
         + jnp.einsum('brnhqk,brnkhe->brnqhe', p[..., ATT_BLOCK:], vb[:, :, 1:]))
    o = o / l.transpose(0, 1, 2, 4, 3, 5)
    lse = (m + jnp.log(l)).transpose(0, 1, 2, 4, 3, 5)
    return o, lse


def dilated_mixer(h, kv_blocks, w_q, w_out):
    bsz, s_len, _ = h.shape
    q = (h @ w_q).reshape(bsz, s_len, N_BRANCH, ATT_HEADS, HEAD_DIM)
    slopes = alibi_slopes(ATT_HEADS)
    outs, lses = [], []
    for g in range(N_BRANCH):
        d = DILATIONS[g]
        kb, vb = kv_blocks[g]
        o, lse = dilated_branch(to_query_blocks(q[:, :, g], d), kb, vb, d, WINDOWS[g] // d, slopes)
        outs.append(from_blocks(o, d, s_len))
        lses.append(from_blocks(lse, d, s_len))
    w = jax.nn.softmax(jnp.stack(lses, axis=0), axis=0)
    o = jnp.sum(w * jnp.stack(outs, axis=0), axis=0)
    return o.reshape(bsz, s_len, ATT_HEADS * HEAD_DIM).astype(h.dtype) @ w_out


def conv_glu(h, w_up, conv_w, conv_b, w_down):
    u, g = jnp.split(h @ w_up, 2, axis=-1)
    gp = jnp.pad(g, ((0, 0), (CONV_WIDTH - 1, 0), (0, 0)))
    g = conv_w[0] * gp[:, :-2] + conv_w[1] * gp[:, 1:-1] + conv_w[2] * gp[:, 2:] + conv_b
    return (jax.nn.gelu(g, approximate=False) * u) @ w_down


def setup_inputs(seed: int = 0) -> dict:
    key = jax.random.key(seed)
    ks = jax.random.split(key, 17)
    f32 = jnp.float32

    def nrm(k, shape, fan_in):
        return jax.random.normal(k, shape, f32) * (fan_in ** -0.5)

    def gain(k, shape):
        return 1.0 + 0.02 * jax.random.normal(k, shape, f32)

    return {
        "x": jax.random.normal(ks[0], (BATCH, SEQ, D_MODEL), f32),
        "attn_norm": gain(ks[1], (DEPTH, D_MODEL)),
        "gla_w_in": nrm(ks[2], (N_A_LAYERS, D_MODEL, GLA_IN_DIM), D_MODEL),
        "gla_w_a2": nrm(ks[3], (N_A_LAYERS, GATE_RANK, GLA_KEY_DIM), GATE_RANK),
        "gla_b_a2": 0.1 * jax.random.normal(ks[4], (N_A_LAYERS, GLA_KEY_DIM), f32),
        "gla_head_norm": gain(ks[5], (N_A_LAYERS, GLA_DV)),
        "gla_w_out": nrm(ks[6], (N_A_LAYERS, GLA_VAL_DIM, D_MODEL), GLA_VAL_DIM),
        "kv_norm": gain(ks[7], (D_MODEL,)),
        "w_kv": nrm(ks[8], (D_MODEL, 2 * ATT_HEADS * HEAD_DIM), D_MODEL),
        "dsa_w_q": nrm(ks[9], (N_B_LAYERS, D_MODEL, N_BRANCH * ATT_HEADS * HEAD_DIM), D_MODEL),
        "dsa_w_out": nrm(ks[10], (N_B_LAYERS, ATT_HEADS * HEAD_DIM, D_MODEL), ATT_HEADS * HEAD_DIM),
        "ffn_norm": gain(ks[11], (DEPTH, D_MODEL)),
        "ffn_w_up": nrm(ks[12], (DEPTH, D_MODEL, 2 * D_FF), D_MODEL),
        "ffn_conv_w": nrm(ks[13], (DEPTH, CONV_WIDTH, D_FF), CONV_WIDTH),
        "ffn_conv_b": 0.02 * jax.random.normal(ks[14], (DEPTH, D_FF), f32),
        "ffn_w_down": nrm(ks[15], (DEPTH, D_FF, D_MODEL), D_FF),
        "final_norm": gain(ks[16], (D_MODEL,)),
    }


def reference(x, attn_norm, gla_w_in, gla_w_a2, gla_b_a2, gla_head_norm, gla_w_out,
              kv_norm, w_kv, dsa_w_q, dsa_w_out, ffn_norm, ffn_w_up, ffn_conv_w,
              ffn_conv_b, ffn_w_down, final_norm):
    h = x
    kv_blocks = None
    for i in range(DEPTH):
        if i < N_A_LAYERS:
            h = h + gla_mixer(rmsnorm(h, attn_norm[i]), gla_w_in[i], gla_w_a2[i], gla_b_a2[i],
                              gla_head_norm[i], gla_w_out[i])
        else:
            if i == N_A_LAYERS:
                kv_blocks = shared_kv(h, kv_norm, w_kv)
            j = i - N_A_LAYERS
            h = h + dilated_mixer(rmsnorm(h, attn_norm[i]), kv_blocks, dsa_w_q[j], dsa_w_out[j])
        h = h + conv_glu(rmsnorm(h, ffn_norm[i]), ffn_w_up[i], ffn_conv_w[i], ffn_conv_b[i],
                         ffn_w_down[i])
    return rmsnorm(h, final_norm)
```

```python
import functools
import math

import numpy as np
import jax
import jax.numpy as jnp
from jax import lax
from jax.experimental import pallas as pl
from jax.experimental.pallas import tpu as pltpu

D_MODEL = 2048
GLA_HEADS = 4
GLA_KEY_DIM = 1024
GLA_VAL_DIM = 2048
GLA_DK = 256
GLA_DV = 512
GATE_RANK = 16
GATE_NORMALIZER = 16.0
GLA_CHUNK = 64
GLA_BLOCK = 256
ATT_HEADS = 16
HEAD_DIM = 128
WINDOWS = (128, 512, 2048)
DILATIONS = (1, 4, 16)
ATT_BLOCK = 128
D_FF = 5632
EPS = 1e-6
LANES = 128
HALO_ROWS = 16

F32 = jnp.float32
BF16 = jnp.bfloat16


def _alibi_slopes(n):
    def pow2_slopes(m):
        start = 2.0 ** (-8.0 / m)
        return [start ** (i + 1) for i in range(m)]
    assert math.log2(n).is_integer()
    return [float(v) for v in np.array(pow2_slopes(n), dtype=np.float32)]


def _rmsnorm_kernel(x_ref, g_ref, o_ref):
    x = x_ref[...]
    ms = jnp.mean(x * x, axis=-1, keepdims=True)
    o_ref[...] = ((x * lax.rsqrt(ms + EPS)) * g_ref[...]).astype(o_ref.dtype)


def _rmsnorm(x, g, out_dtype, tm=512):
    m, d = x.shape
    return pl.pallas_call(
        _rmsnorm_kernel,
        grid=(m // tm,),
        in_specs=[pl.BlockSpec((tm, d), lambda i: (i, 0)),
                  pl.BlockSpec((1, d), lambda i: (0, 0))],
        out_specs=pl.BlockSpec((tm, d), lambda i: (i, 0)),
        out_shape=jax.ShapeDtypeStruct((m, d), out_dtype),
        compiler_params=pltpu.CompilerParams(dimension_semantics=("arbitrary",)),
        name="rmsnorm",
    )(x, g.reshape(1, d))


def _mm_kernel(x_ref, w_ref, o_ref):
    o_ref[...] = jnp.dot(x_ref[...], w_ref[...],
                         preferred_element_type=F32).astype(o_ref.dtype)


def _mm_res_kernel(x_ref, w_ref, r_ref, o_ref):
    acc = jnp.dot(x_ref[...], w_ref[...], preferred_element_type=F32)
    o_ref[...] = (r_ref[...] + acc).astype(o_ref.dtype)


def _matmul(x, w, out_dtype, n_cols=None, residual=None, tm=512, tn=1024):
    m, k = x.shape
    n = w.shape[1] if n_cols is None else n_cols
    assert m % tm == 0 and n % tn == 0
    in_specs = [pl.BlockSpec((tm, k), lambda j, i: (i, 0)),
                pl.BlockSpec((k, tn), lambda j, i: (0, j))]
    args = [x, w]
    kern = _mm_kernel
    if residual is not None:
        in_specs.append(pl.BlockSpec((tm, tn), lambda j, i: (i, j)))
        args.append(residual)
        kern = _mm_res_kernel
    return pl.pallas_call(
        kern,
        grid=(n // tn, m // tm),
        in_specs=in_specs,
        out_specs=pl.BlockSpec((tm, tn), lambda j, i: (i, j)),
        out_shape=jax.ShapeDtypeStruct((m, n), out_dtype),
        compiler_params=pltpu.CompilerParams(
            dimension_semantics=("arbitrary", "arbitrary")),
        name="matmul",
    )(*args)


def _gate_kernel(x_ref, w1_ref, w2_ref, b_ref, o_ref):
    a = jnp.dot(x_ref[...], w1_ref[...], preferred_element_type=F32)
    z = jnp.dot(a.astype(BF16), w2_ref[...], preferred_element_type=F32) + b_ref[...]
    log_sig = jnp.minimum(z, 0.0) - jnp.log1p(jnp.exp(-jnp.abs(z)))
    o_ref[...] = log_sig / GATE_NORMALIZER


def _gla_gate(xn, w_a1, w_a2, b_a2, tm=512):
    m, k = xn.shape
    return pl.pallas_call(
        _gate_kernel,
        grid=(m // tm,),
        in_specs=[pl.BlockSpec((tm, k), lambda i: (i, 0)),
                  pl.BlockSpec((k, LANES), lambda i: (0, 0)),
                  pl.BlockSpec((LANES, GLA_KEY_DIM), lambda i: (0, 0)),
                  pl.BlockSpec((1, GLA_KEY_DIM), lambda i: (0, 0))],
        out_specs=pl.BlockSpec((tm, GLA_KEY_DIM), lambda i: (i, 0)),
        out_shape=jax.ShapeDtypeStruct((m, GLA_KEY_DIM), F32),
        compiler_params=pltpu.CompilerParams(dimension_semantics=("arbitrary",)),
        name="gla_gate",
    )(xn, w_a1, w_a2, b_a2.reshape(1, GLA_KEY_DIM))


def _split3(x):
    hi = x.astype(BF16)
    r1 = x - hi.astype(F32)
    mid = r1.astype(BF16)
    lo = (r1 - mid.astype(F32)).astype(BF16)
    return hi, mid, lo


def _gla_kernel(q_ref, k_ref, v_ref, r_ref, la_ref, hn_ref, o_ref, st_ref):
    @pl.when(pl.program_id(2) == 0)
    def _():
        st_ref[...] = jnp.zeros_like(st_ref)

    nb = GLA_BLOCK
    row = lax.broadcasted_iota(jnp.int32, (nb, nb), 0)
    col = lax.broadcasted_iota(jnp.int32, (nb, nb), 1)
    same_chunk = (row // GLA_CHUNK) == (col // GLA_CHUNK)
    causal = same_chunk & (col <= row)
    tri = jnp.where(causal, 1.0, 0.0).astype(BF16)

    la = la_ref[...]
    hi, mid, lo = _split3(la)
    cum = (jnp.dot(tri, hi, preferred_element_type=F32)
           + jnp.dot(tri, mid, preferred_element_type=F32)
           + jnp.dot(tri, lo, preferred_element_type=F32))

    q = q_ref[...].astype(F32) * (GLA_DK ** -0.5)
    k = k_ref[...].astype(F32)
    v = v_ref[...]
    q_dec = (q * jnp.exp(cum)).astype(BF16)
    k_inv = (k * jnp.exp(-cum)).astype(BF16)

    scores = lax.dot_general(q_dec, k_inv, (((1,), (1,)), ((), ())),
                             preferred_element_type=F32)
    scores = jnp.where(causal, scores, 0.0).astype(BF16)
    o_intra = jnp.dot(scores, v, preferred_element_type=F32)

    o_parts = []
    for c in range(nb // GLA_CHUNK):
        lo_r, hi_r = c * GLA_CHUNK, (c + 1) * GLA_CHUNK
        cum_c = cum[lo_r:hi_r]
        last_c = cum[hi_r - 1:hi_r]
        k_end = (k[lo_r:hi_r] * jnp.exp(last_c - cum_c)).astype(BF16)
        st = st_ref[...]
        o_inter = lax.dot_general(q_dec[lo_r:hi_r], st.astype(BF16),
                                  (((1,), (1,)), ((), ())),
                                  preferred_element_type=F32)
        upd = lax.dot_general(v[lo_r:hi_r], k_end, (((0,), (0,)), ((), ())),
                              preferred_element_type=F32)
        st_ref[...] = st * jnp.exp(last_c) + upd
        o_parts.append(o_intra[lo_r:hi_r] + o_inter)
    o = jnp.concatenate(o_parts, axis=0)

    ms = jnp.mean(o * o, axis=-1, keepdims=True)
    o = (o * lax.rsqrt(ms + EPS)) * hn_ref[...]
    r = r_ref[...].astype(F32)
    gate = r * (1.0 / (1.0 + jnp.exp(-r)))
    o_ref[...] = (o * gate).astype(o_ref.dtype)


def _gla_recurrence(proj, la, head_norm, bsz, s_len):
    t = bsz * s_len
    nblk = s_len // GLA_BLOCK
    kq = GLA_KEY_DIM // GLA_DK
    row = lambda b, h, i: b * nblk + i
    return pl.pallas_call(
        _gla_kernel,
        grid=(bsz, GLA_HEADS, nblk),
        in_specs=[
            pl.BlockSpec((GLA_BLOCK, GLA_DK), lambda b, h, i: (row(b, h, i), h)),
            pl.BlockSpec((GLA_BLOCK, GLA_DK), lambda b, h, i: (row(b, h, i), kq + h)),
            pl.BlockSpec((GLA_BLOCK, GLA_DV), lambda b, h, i: (row(b, h, i), kq + h)),
            pl.BlockSpec((GLA_BLOCK, GLA_DV), lambda b, h, i: (row(b, h, i), 2 * kq + h)),
            pl.BlockSpec((GLA_BLOCK, GLA_DK), lambda b, h, i: (row(b, h, i), h)),
            pl.BlockSpec((1, GLA_DV), lambda b, h, i: (0, 0)),
        ],
        out_specs=pl.BlockSpec((GLA_BLOCK, GLA_DV), lambda b, h, i: (row(b, h, i), h)),
        out_shape=jax.ShapeDtypeStruct((t, GLA_VAL_DIM), BF16),
        scratch_shapes=[pltpu.VMEM((GLA_DV, GLA_DK), F32)],
        compiler_params=pltpu.CompilerParams(
            dimension_semantics=("arbitrary", "arbitrary", "arbitrary")),
        name="gla_recurrence",
    )(proj, proj, proj, proj, la, head_norm.reshape(1, GLA_DV))


def _dsa_kernel(q_ref, kp_ref, kc_ref, vp_ref, vc_ref, o_ref, lse_ref, *, dil, slopes):
    n = pl.program_id(2)
    a = lax.broadcasted_iota(jnp.int32, (ATT_BLOCK, ATT_BLOCK), 0)
    c = lax.broadcasted_iota(jnp.int32, (ATT_BLOCK, ATT_BLOCK), 1)
    valid_prev = (c >= a) & (n > 0)
    valid_cur = c <= a
    dist_prev = ((a - c + ATT_BLOCK) * dil).astype(F32)
    dist_cur = ((a - c) * dil).astype(F32)
    lane = lax.broadcasted_iota(jnp.int32, (ATT_BLOCK, LANES), 1)
    lse_tile = jnp.zeros((ATT_BLOCK, LANES), F32)
    scale = HEAD_DIM ** -0.5
    nt = (((1,), (1,)), ((), ()))
    for h in range(ATT_HEADS):
        sl = slice(h * HEAD_DIM, (h + 1) * HEAD_DIM)
        q = q_ref[0, :, sl]
        s_prev = lax.dot_general(q, kp_ref[0, :, sl], nt, preferred_element_type=F32)
        s_cur = lax.dot_general(q, kc_ref[0, :, sl], nt, preferred_element_type=F32)
        s_prev = jnp.where(valid_prev, s_prev * scale - slopes[h] * dist_prev, -jnp.inf)
        s_cur = jnp.where(valid_cur, s_cur * scale - slopes[h] * dist_cur, -jnp.inf)
        m = jnp.maximum(jnp.max(s_prev, axis=-1, keepdims=True),
                        jnp.max(s_cur, axis=-1, keepdims=True))
        p_prev = jnp.exp(s_prev - m)
        p_cur = jnp.exp(s_cur - m)
        l = jnp.sum(p_prev, axis=-1, keepdims=True) + jnp.sum(p_cur, axis=-1, keepdims=True)
        acc = (jnp.dot(p_prev.astype(BF16), vp_ref[0, :, sl], preferred_element_type=F32)
               + jnp.dot(p_cur.astype(BF16), vc_ref[0, :, sl], preferred_element_type=F32))
        o_ref[0, :, sl] = (acc / l).astype(o_ref.dtype)
        lse_tile = jnp.where(lane == h, m + jnp.log(l), lse_tile)
    lse_ref[0] = lse_tile


def _dsa_branch(q, kv, g, bsz, s_len):
    d = DILATIONS[g]
    assert WINDOWS[g] // d == ATT_BLOCK
    sub = s_len // d
    nb = sub // ATT_BLOCK
    hd = ATT_HEADS * HEAD_DIM
    q3 = q.reshape(bsz, sub, d * 3 * hd)
    kv3 = kv.reshape(bsz, sub, d * 2 * hd)
    prev = lambda n: jnp.maximum(n - 1, 0)
    blk = (1, ATT_BLOCK, hd)
    o, lse = pl.pallas_call(
        functools.partial(_dsa_kernel, dil=d, slopes=_alibi_slopes(ATT_HEADS)),
        grid=(bsz, d, nb),
        in_specs=[
            pl.BlockSpec(blk, lambda b, r, n: (b, n, r * 3 + g)),
            pl.BlockSpec(blk, lambda b, r, n: (b, prev(n), r * 2)),
            pl.BlockSpec(blk, lambda b, r, n: (b, n, r * 2)),
            pl.BlockSpec(blk, lambda b, r, n: (b, prev(n), r * 2 + 1)),
            pl.BlockSpec(blk, lambda b, r, n: (b, n, r * 2 + 1)),
        ],
        out_specs=[pl.BlockSpec(blk, lambda b, r, n: (b, n, r)),
                   pl.BlockSpec((1, ATT_BLOCK, LANES), lambda b, r, n: (b, n, r))],
        out_shape=[jax.ShapeDtypeStruct((bsz, sub, d * hd), BF16),
                   jax.ShapeDtypeStruct((bsz, sub, d * LANES), F32)],
        compiler_params=pltpu.CompilerParams(
            dimension_semantics=("arbitrary", "arbitrary", "arbitrary")),
        name=f"dsa_branch{g}",
    )(q3, kv3, kv3, kv3, kv3)
    return o.reshape(bsz * s_len, hd), lse.reshape(bsz * s_len, LANES)


def _merge_kernel(o0_ref, o1_ref, o2_ref, l0_ref, l1_ref, l2_ref, o_ref):
    l0, l1, l2 = l0_ref[...], l1_ref[...], l2_ref[...]
    m = jnp.maximum(jnp.maximum(l0, l1), l2)
    e0, e1, e2 = jnp.exp(l0 - m), jnp.exp(l1 - m), jnp.exp(l2 - m)
    den = e0 + e1 + e2
    w0, w1, w2 = e0 / den, e1 / den, e2 / den
    for h in range(ATT_HEADS):
        sl = slice(h * HEAD_DIM, (h + 1) * HEAD_DIM)
        o = (w0[:, h:h + 1] * o0_ref[:, sl].astype(F32)
             + w1[:, h:h + 1] * o1_ref[:, sl].astype(F32)
             + w2[:, h:h + 1] * o2_ref[:, sl].astype(F32))
        o_ref[:, sl] = o.astype(o_ref.dtype)


def _dsa_merge(outs, lses, tm=512):
    t, hd = outs[0].shape
    o_spec = pl.BlockSpec((tm, hd), lambda i: (i, 0))
    l_spec = pl.BlockSpec((tm, LANES), lambda i: (i, 0))
    return pl.pallas_call(
        _merge_kernel,
        grid=(t // tm,),
        in_specs=[o_spec, o_spec, o_spec, l_spec, l_spec, l_spec],
        out_specs=o_spec,
        out_shape=jax.ShapeDtypeStruct((t, hd), BF16),
        compiler_params=pltpu.CompilerParams(dimension_semantics=("arbitrary",)),
        name="dsa_merge",
    )(*outs, *lses)


def _ffn_up_kernel(x_ref, xh_ref, wu_ref, wg_ref, cw_ref, cb_ref, o_ref, *, tm, s_len):
    x = x_ref[...]
    u = jnp.dot(x, wu_ref[...], preferred_element_type=F32)
    g = jnp.dot(x, wg_ref[...], preferred_element_type=F32)
    gh = jnp.dot(xh_ref[...], wg_ref[...], preferred_element_type=F32)
    seq_start = (pl.program_id(1) * tm) % s_len == 0
    gh = jnp.where(seq_start, 0.0, gh)
    g_m1 = gh[HALO_ROWS - 1:HALO_ROWS]
    g_m2 = gh[HALO_ROWS - 2:HALO_ROWS - 1]
    row = lax.broadcasted_iota(jnp.int32, g.shape, 0)
    g1 = jnp.where(row == 0, g_m1, pltpu.roll(g, 1, axis=0))
    g2 = jnp.where(row == 0, g_m2, jnp.where(row == 1, g_m1, pltpu.roll(g, 2, axis=0)))
    cw = cw_ref[...]
    conv = cw[0:1] * g2 + cw[1:2] * g1 + cw[2:3] * g + cb_ref[...]
    act = 0.5 * conv * (1.0 + lax.erf(conv * (2.0 ** -0.5)))
    o_ref[...] = (act * u).astype(o_ref.dtype)


def _ffn_up(xn, w_up, conv_w, conv_b, s_len, tm=512, tn=512):
    m, k = xn.shape
    nn = D_FF // tn
    hb = tm // HALO_ROWS
    return pl.pallas_call(
        functools.partial(_ffn_up_kernel, tm=tm, s_len=s_len),
        grid=(nn, m // tm),
        in_specs=[
            pl.BlockSpec((tm, k), lambda j, i: (i, 0)),
            pl.BlockSpec((HALO_ROWS, k), lambda j, i: (jnp.maximum(i * hb - 1, 0), 0)),
            pl.BlockSpec((k, tn), lambda j, i: (0, j)),
            pl.BlockSpec((k, tn), lambda j, i: (0, nn + j)),
            pl.BlockSpec((3, tn), lambda j, i: (0, j)),
            pl.BlockSpec((1, tn), lambda j, i: (0, j)),
        ],
        out_specs=pl.BlockSpec((tm, tn), lambda j, i: (i, j)),
        out_shape=jax.ShapeDtypeStruct((m, D_FF), BF16),
        compiler_params=pltpu.CompilerParams(
            dimension_semantics=("arbitrary", "arbitrary")),
        name="ffn_up",
    )(xn, xn, w_up, w_up, conv_w, conv_b.reshape(1, D_FF))


def _conv_glu(h, norm_g, w_up, conv_w, conv_b, w_down, s_len):
    xn = _rmsnorm(h, norm_g, BF16)
    act = _ffn_up(xn, w_up.astype(BF16), conv_w, conv_b, s_len)
    return _matmul(act, w_down.astype(BF16), F32, residual=h, tn=512)


def kernel(x, attn_norm, gla_w_in, gla_w_a2, gla_b_a2, gla_head_norm, gla_w_out, kv_norm, w_kv,
           dsa_w_q, dsa_w_out, ffn_norm, ffn_w_up, ffn_conv_w, ffn_conv_b, ffn_w_down, final_norm):
    bsz, s_len, d = x.shape
    t = bsz * s_len
    h = x.reshape(t, d)

    xn = _rmsnorm(h, attn_norm[0], BF16)
    w_in = gla_w_in[0].astype(BF16)
    n_main = 2 * GLA_KEY_DIM + 2 * GLA_VAL_DIM
    proj = _matmul(xn, w_in, BF16, n_cols=n_main)
    w_a1 = jnp.pad(w_in[:, n_main:], ((0, 0), (0, LANES - GATE_RANK)))
    w_a2 = jnp.pad(gla_w_a2[0].astype(BF16), ((0, LANES - GATE_RANK), (0, 0)))
    la = _gla_gate(xn, w_a1, w_a2, gla_b_a2[0])
    o = _gla_recurrence(proj, la, gla_head_norm[0], bsz, s_len)
    h = _matmul(o, gla_w_out[0].astype(BF16), F32, residual=h)
    h = _conv_glu(h, ffn_norm[0], ffn_w_up[0], ffn_conv_w[0], ffn_conv_b[0], ffn_w_down[0], s_len)

    kv = _matmul(_rmsnorm(h, kv_norm, BF16), w_kv.astype(BF16), BF16)

    q = _matmul(_rmsnorm(h, attn_norm[1], BF16), dsa_w_q[0].astype(BF16), BF16)
    outs, lses = [], []
    for g in range(len(DILATIONS)):
        o_g, lse_g = _dsa_branch(q, kv, g, bsz, s_len)
        outs.append(o_g)
        lses.append(lse_g)
    o = _dsa_merge(outs, lses)
    h = _matmul(o, dsa_w_out[0].astype(BF16), F32, residual=h)
    h = _conv_glu(h, ffn_norm[1], ffn_w_up[1], ffn_conv_w[1], ffn_conv_b[1], ffn_w_down[1], s_len)

    return _rmsnorm(h, final_norm, F32).reshape(bsz, s_len, d)
```

```python
import functools
import math

import numpy as np
import jax
import jax.numpy as jnp
from jax import lax
from jax.experimental import pallas as pl
from jax.experimental.pallas import tpu as pltpu

D_MODEL = 2048
GLA_HEADS = 4
GLA_KEY_DIM = 1024
GLA_VAL_DIM = 2048
GLA_DK = 256
GLA_DV = 512
GATE_RANK = 16
GATE_NORMALIZER = 16.0
GLA_CHUNK = 64
GLA_BLOCK = 256
ATT_HEADS = 16
HEAD_DIM = 128
ATT_WIDTH = ATT_HEADS * HEAD_DIM
WINDOWS = (128, 512, 2048)
DILATIONS = (1, 4, 16)
ATT_BLOCK = 128
QK_LOOKAHEAD = 2
PERM_TILE = 512
PERM_RUN = PERM_TILE // 16
D_FF = 5632
EPS = 1e-6
LANES = 128
HALO_ROWS = 16

F32 = jnp.float32
BF16 = jnp.bfloat16


def _alibi_slopes(n):
    def pow2_slopes(m):
        start = 2.0 ** (-8.0 / m)
        return [start ** (i + 1) for i in range(m)]
    assert math.log2(n).is_integer()
    return [float(v) for v in np.array(pow2_slopes(n), dtype=np.float32)]


def _rmsnorm_kernel(x_ref, g_ref, o_ref):
    x = x_ref[...]
    ms = jnp.mean(x * x, axis=-1, keepdims=True)
    o_ref[...] = ((x * lax.rsqrt(ms + EPS)) * g_ref[...]).astype(o_ref.dtype)


def _rmsnorm(x, g, out_dtype, tm=512):
    m, d = x.shape
    return pl.pallas_call(
        _rmsnorm_kernel,
        grid=(m // tm,),
        in_specs=[pl.BlockSpec((tm, d), lambda i: (i, 0)),
                  pl.BlockSpec((1, d), lambda i: (0, 0))],
        out_specs=pl.BlockSpec((tm, d), lambda i: (i, 0)),
        out_shape=jax.ShapeDtypeStruct((m, d), out_dtype),
        compiler_params=pltpu.CompilerParams(dimension_semantics=("arbitrary",)),
        name="rmsnorm",
    )(x, g.reshape(1, d))


def _mm_kernel(x_ref, w_ref, o_ref):
    o_ref[...] = jnp.dot(x_ref[...], w_ref[...],
                         preferred_element_type=F32).astype(o_ref.dtype)


def _mm_res_kernel(x_ref, w_ref, r_ref, o_ref):
    acc = jnp.dot(x_ref[...], w_ref[...], preferred_element_type=F32)
    o_ref[...] = (r_ref[...] + acc).astype(o_ref.dtype)


def _matmul(x, w, out_dtype, n_cols=None, col_start=0, residual=None, tm=512, tn=1024):
    m, k = x.shape
    n = w.shape[1] if n_cols is None else n_cols
    assert m % tm == 0 and n % tn == 0 and col_start % tn == 0
    j0 = col_start // tn
    in_specs = [pl.BlockSpec((tm, k), lambda j, i: (i, 0)),
                pl.BlockSpec((k, tn), lambda j, i: (0, j0 + j))]
    args = [x, w]
    kern = _mm_kernel
    if residual is not None:
        in_specs.append(pl.BlockSpec((tm, tn), lambda j, i: (i, j)))
        args.append(residual)
        kern = _mm_res_kernel
    return pl.pallas_call(
        kern,
        grid=(n // tn, m // tm),
        in_specs=in_specs,
        out_specs=pl.BlockSpec((tm, tn), lambda j, i: (i, j)),
        out_shape=jax.ShapeDtypeStruct((m, n), out_dtype),
        compiler_params=pltpu.CompilerParams(
            dimension_semantics=("arbitrary", "arbitrary")),
        name="matmul",
    )(*args)


def _gate_kernel(x_ref, w1_ref, w2_ref, b_ref, o_ref):
    a = jnp.dot(x_ref[...], w1_ref[...], preferred_element_type=F32)
    z = jnp.dot(a.astype(BF16), w2_ref[...], preferred_element_type=F32) + b_ref[...]
    log_sig = jnp.minimum(z, 0.0) - jnp.log1p(jnp.exp(-jnp.abs(z)))
    o_ref[...] = log_sig / GATE_NORMALIZER


def _gla_gate(xn, w_a1, w_a2, b_a2, tm=512):
    m, k = xn.shape
    return pl.pallas_call(
        _gate_kernel,
        grid=(m // tm,),
        in_specs=[pl.BlockSpec((tm, k), lambda i: (i, 0)),
                  pl.BlockSpec((k, LANES), lambda i: (0, 0)),
                  pl.BlockSpec((LANES, GLA_KEY_DIM), lambda i: (0, 0)),
                  pl.BlockSpec((1, GLA_KEY_DIM), lambda i: (0, 0))],
        out_specs=pl.BlockSpec((tm, GLA_KEY_DIM), lambda i: (i, 0)),
        out_shape=jax.ShapeDtypeStruct((m, GLA_KEY_DIM), F32),
        compiler_params=pltpu.CompilerParams(dimension_semantics=("arbitrary",)),
        name="gla_gate",
    )(xn, w_a1, w_a2, b_a2.reshape(1, GLA_KEY_DIM))


def _split3(x):
    hi = x.astype(BF16)
    r1 = x - hi.astype(F32)
    mid = r1.astype(BF16)
    lo = (r1 - mid.astype(F32)).astype(BF16)
    return hi, mid, lo


def _dot01(mat01, x):
    hi, mid, lo = _split3(x)
    return (jnp.dot(mat01, hi, preferred_element_type=F32)
            + jnp.dot(mat01, mid, preferred_element_type=F32)
            + jnp.dot(mat01, lo, preferred_element_type=F32))


def _gla_kernel(q_ref, k_ref, v_ref, r_ref, la_ref, hn_ref, o_ref, st_ref):
    @pl.when(pl.program_id(2) == 0)
    def _():
        st_ref[...] = jnp.zeros_like(st_ref)

    nb = GLA_BLOCK
    row = lax.broadcasted_iota(jnp.int32, (nb, nb), 0)
    col = lax.broadcasted_iota(jnp.int32, (nb, nb), 1)
    same_chunk = (row // GLA_CHUNK) == (col // GLA_CHUNK)
    causal = same_chunk & (col <= row)
    tri = jnp.where(causal, 1.0, 0.0).astype(BF16)

    cum = _dot01(tri, la_ref[...])

    q = q_ref[...].astype(F32) * (GLA_DK ** -0.5)
    k = k_ref[...].astype(F32)
    v = v_ref[...]
    q_dec = (q * jnp.exp(cum)).astype(BF16)
    k_inv = (k * jnp.exp(-cum)).astype(BF16)

    scores = lax.dot_general(q_dec, k_inv, (((1,), (1,)), ((), ())),
                             preferred_element_type=F32)
    scores = jnp.where(causal, scores, 0.0).astype(BF16)
    o_intra = jnp.dot(scores, v, preferred_element_type=F32)

    o_parts = []
    for c in range(nb // GLA_CHUNK):
        lo_r, hi_r = c * GLA_CHUNK, (c + 1) * GLA_CHUNK
        cum_c = cum[lo_r:hi_r]
        last_c = cum[hi_r - 1:hi_r]
        k_end = (k[lo_r:hi_r] * jnp.exp(last_c - cum_c)).astype(BF16)
        st = st_ref[...]
        o_inter = lax.dot_general(q_dec[lo_r:hi_r], st.astype(BF16),
                                  (((1,), (1,)), ((), ())),
                                  preferred_element_type=F32)
        upd = lax.dot_general(v[lo_r:hi_r], k_end, (((0,), (0,)), ((), ())),
                              preferred_element_type=F32)
        st_ref[...] = st * jnp.exp(last_c) + upd
        o_parts.append(o_intra[lo_r:hi_r] + o_inter)
    o = jnp.concatenate(o_parts, axis=0)

    ms = jnp.mean(o * o, axis=-1, keepdims=True)
    o = (o * lax.rsqrt(ms + EPS)) * hn_ref[...]
    r = r_ref[...].astype(F32)
    gate = r * (1.0 / (1.0 + jnp.exp(-r)))
    o_ref[...] = (o * gate).astype(o_ref.dtype)


def _gla_recurrence(proj, la, head_norm, bsz, s_len):
    t = bsz * s_len
    nblk = s_len // GLA_BLOCK
    kq = GLA_KEY_DIM // GLA_DK
    row = lambda b, h, i: b * nblk + i
    return pl.pallas_call(
        _gla_kernel,
        grid=(bsz, GLA_HEADS, nblk),
        in_specs=[
            pl.BlockSpec((GLA_BLOCK, GLA_DK), lambda b, h, i: (row(b, h, i), h)),
            pl.BlockSpec((GLA_BLOCK, GLA_DK), lambda b, h, i: (row(b, h, i), kq + h)),
            pl.BlockSpec((GLA_BLOCK, GLA_DV), lambda b, h, i: (row(b, h, i), kq + h)),
            pl.BlockSpec((GLA_BLOCK, GLA_DV), lambda b, h, i: (row(b, h, i), 2 * kq + h)),
            pl.BlockSpec((GLA_BLOCK, GLA_DK), lambda b, h, i: (row(b, h, i), h)),
            pl.BlockSpec((1, GLA_DV), lambda b, h, i: (0, 0)),
        ],
        out_specs=pl.BlockSpec((GLA_BLOCK, GLA_DV), lambda b, h, i: (row(b, h, i), h)),
        out_shape=jax.ShapeDtypeStruct((t, GLA_VAL_DIM), BF16),
        scratch_shapes=[pltpu.VMEM((GLA_DV, GLA_DK), F32)],
        compiler_params=pltpu.CompilerParams(
            dimension_semantics=("arbitrary", "arbitrary", "arbitrary")),
        name="gla_recurrence",
    )(proj, proj, proj, proj, la, head_norm.reshape(1, GLA_DV))


def _perm_matrix():
    p = np.zeros((PERM_TILE, PERM_TILE), np.float32)
    for i in range(PERM_RUN):
        for a in range(4):
            for r4 in range(4):
                p[(4 * r4 + a) * PERM_RUN + i, 16 * i + 4 * a + r4] = 1.0
    return p


def _permute_kernel(p_ref, x_ref, o_ref):
    o_ref[...] = jnp.dot(p_ref[...], x_ref[...],
                         preferred_element_type=F32).astype(o_ref.dtype)


def _permute_rows(x, perm, tn=1024):
    m, n = x.shape
    return pl.pallas_call(
        _permute_kernel,
        grid=(m // PERM_TILE, n // tn),
        in_specs=[pl.BlockSpec((PERM_TILE, PERM_TILE), lambda i, j: (0, 0)),
                  pl.BlockSpec((PERM_TILE, tn), lambda i, j: (i, j))],
        out_specs=pl.BlockSpec((PERM_TILE, tn), lambda i, j: (i, j)),
        out_shape=jax.ShapeDtypeStruct((m, n), x.dtype),
        compiler_params=pltpu.CompilerParams(
            dimension_semantics=("arbitrary", "arbitrary")),
        name="permute_rows",
    )(perm, x)


def _dsa_kernel(q_ref, kp_ref, kc_ref, vp_ref, vc_ref, o_ref, lse_ref, *, dil, interleaved,
                slopes):
    n = pl.program_id(2)
    nk = 2 * ATT_BLOCK
    rq = lax.broadcasted_iota(jnp.int32, (ATT_BLOCK, nk), 0)
    ck = lax.broadcasted_iota(jnp.int32, (ATT_BLOCK, nk), 1)
    ck_in = ck & (ATT_BLOCK - 1)
    if interleaved:
        sub_q = 4 * (rq & (PERM_RUN - 1)) + (rq >> 5)
        sub_k = 4 * (ck_in & (PERM_RUN - 1)) + (ck_in >> 5)
    else:
        sub_q, sub_k = rq, ck_in
    is_cur = ck >= ATT_BLOCK
    j = sub_q - sub_k + jnp.where(is_cur, 0, ATT_BLOCK)
    valid = (j >= 0) & (j <= ATT_BLOCK) & (is_cur | (n > 0))
    neg_dist = jnp.where(valid, -(j * dil).astype(F32), -jnp.inf)

    lane = lax.broadcasted_iota(jnp.int32, (ATT_BLOCK, LANES), 1)
    ones = jnp.ones((nk, HEAD_DIM), BF16)
    scale = HEAD_DIM ** -0.5
    nt = (((1,), (1,)), ((), ()))

    def head_slice(ref, h):
        x = ref[..., h * HEAD_DIM:(h + 1) * HEAD_DIM]
        return x.reshape(ATT_BLOCK, HEAD_DIM)

    def scores(h):
        kcat = jnp.concatenate([head_slice(kp_ref, h), head_slice(kc_ref, h)], axis=0)
        return lax.dot_general(head_slice(q_ref, h), kcat, nt, preferred_element_type=F32)

    pending = [scores(h) for h in range(QK_LOOKAHEAD)]
    lse_tile = jnp.zeros((ATT_BLOCK, LANES), F32)
    for h in range(ATT_HEADS):
        s = pending.pop(0)
        if h + QK_LOOKAHEAD < ATT_HEADS:
            pending.append(scores(h + QK_LOOKAHEAD))
        z = s * scale + slopes[h] * neg_dist
        m = jnp.max(z, axis=-1, keepdims=True)
        p = jnp.exp(z - m).astype(BF16)
        vcat = jnp.concatenate([head_slice(vp_ref, h), head_slice(vc_ref, h)], axis=0)
        acc = jnp.dot(p, jnp.concatenate([vcat, ones], axis=1), preferred_element_type=F32)
        l = acc[:, HEAD_DIM:]
        o = (acc[:, :HEAD_DIM] / l).astype(o_ref.dtype)
        o_ref[..., h * HEAD_DIM:(h + 1) * HEAD_DIM] = o.reshape(o_ref.shape[:-1] + (HEAD_DIM,))
        lse_tile = jnp.where(lane == h, m + jnp.log(l), lse_tile)
    lse_ref[...] = lse_tile.reshape(lse_ref.shape)


def _dsa_branch(q, q_col, kv, g, bsz, s_len):
    d = DILATIONS[g]
    assert WINDOWS[g] // d == ATT_BLOCK
    t = bsz * s_len
    ntile = s_len // PERM_TILE
    prev = lambda n: jnp.maximum(n - 1, 0)
    if g == 0:
        view = lambda x: x.reshape(bsz, s_len, x.shape[-1])
        blk = lambda w: (None, ATT_BLOCK, w)
        grid = (bsz, 1, s_len // ATT_BLOCK)
        at = lambda c, n_of: (lambda b, r, n: (b, n_of(n), c))
    elif g == 1:
        view = lambda x: x.reshape(bsz, ntile, 4, 4, PERM_RUN, x.shape[-1])
        blk = lambda w: (None, None, None, 4, PERM_RUN, w)
        grid = (bsz, 4, ntile)
        at = lambda c, n_of: (lambda b, r, n: (b, n_of(n), r, 0, 0, c))
    else:
        view = lambda x: x.reshape(bsz, ntile, 16, PERM_RUN, x.shape[-1])
        blk = lambda w: (None, 4, None, PERM_RUN, w)
        grid = (bsz, 16, ntile // 4)
        at = lambda c, n_of: (lambda b, r, n: (b, n_of(n), r, 0, c))
    cur = lambda n: n
    qv, kvv = view(q), view(kv)
    o, lse = pl.pallas_call(
        functools.partial(_dsa_kernel, dil=d, interleaved=(g == 1),
                          slopes=_alibi_slopes(ATT_HEADS)),
        grid=grid,
        in_specs=[
            pl.BlockSpec(blk(ATT_WIDTH), at(q_col, cur)),
            pl.BlockSpec(blk(ATT_WIDTH), at(0, prev)),
            pl.BlockSpec(blk(ATT_WIDTH), at(0, cur)),
            pl.BlockSpec(blk(ATT_WIDTH), at(1, prev)),
            pl.BlockSpec(blk(ATT_WIDTH), at(1, cur)),
        ],
        out_specs=[pl.BlockSpec(blk(ATT_WIDTH), at(0, cur)),
                   pl.BlockSpec(blk(LANES), at(0, cur))],
        out_shape=[jax.ShapeDtypeStruct(qv.shape[:-1] + (ATT_WIDTH,), BF16),
                   jax.ShapeDtypeStruct(qv.shape[:-1] + (LANES,), F32)],
        compiler_params=pltpu.CompilerParams(
            dimension_semantics=("arbitrary", "arbitrary", "arbitrary")),
        name=f"dsa_branch{g}",
    )(qv, kvv, kvv, kvv, kvv)
    return o.reshape(t, ATT_WIDTH), lse.reshape(t, LANES)


def _merge_kernel(pt_ref, o0_ref, o1_ref, o2_ref, l0_ref, l1_ref, l2_ref, o_ref):
    pt = pt_ref[...]
    l0 = l0_ref[...]
    l1 = _dot01(pt, l1_ref[...])
    l2 = _dot01(pt, l2_ref[...])
    m = jnp.maximum(jnp.maximum(l0, l1), l2)
    e0, e1, e2 = jnp.exp(l0 - m), jnp.exp(l1 - m), jnp.exp(l2 - m)
    den = e0 + e1 + e2
    w0, w1, w2 = e0 / den, e1 / den, e2 / den
    for h in range(ATT_HEADS):
        sl = slice(h * HEAD_DIM, (h + 1) * HEAD_DIM)
        o1 = jnp.dot(pt, o1_ref[:, sl], preferred_element_type=F32)
        o2 = jnp.dot(pt, o2_ref[:, sl], preferred_element_type=F32)
        o = (w0[:, h:h + 1] * o0_ref[:, sl].astype(F32)
             + w1[:, h:h + 1] * o1 + w2[:, h:h + 1] * o2)
        o_ref[:, sl] = o.astype(o_ref.dtype)


def _dsa_merge(outs, lses, perm_t):
    t, hd = outs[0].shape
    tm = PERM_TILE
    o_spec = pl.BlockSpec((tm, hd), lambda i: (i, 0))
    l_spec = pl.BlockSpec((tm, LANES), lambda i: (i, 0))
    return pl.pallas_call(
        _merge_kernel,
        grid=(t // tm,),
        in_specs=[pl.BlockSpec((tm, tm), lambda i: (0, 0)),
                  o_spec, o_spec, o_spec, l_spec, l_spec, l_spec],
        out_specs=o_spec,
        out_shape=jax.ShapeDtypeStruct((t, hd), BF16),
        compiler_params=pltpu.CompilerParams(dimension_semantics=("arbitrary",)),
        name="dsa_merge",
    )(perm_t, *outs, *lses)


def _ffn_up_kernel(x_ref, xh_ref, wu_ref, wg_ref, cw_ref, cb_ref, o_ref, *, tm, s_len):
    x = x_ref[...]
    u = jnp.dot(x, wu_ref[...], preferred_element_type=F32)
    g = jnp.dot(x, wg_ref[...], preferred_element_type=F32)
    gh = jnp.dot(xh_ref[...], wg_ref[...], preferred_element_type=F32)
    seq_start = (pl.program_id(1) * tm) % s_len == 0
    gh = jnp.where(seq_start, 0.0, gh)
    g_m1 = gh[HALO_ROWS - 1:HALO_ROWS]
    g_m2 = gh[HALO_ROWS - 2:HALO_ROWS - 1]
    row = lax.broadcasted_iota(jnp.int32, g.shape, 0)
    g1 = jnp.where(row == 0, g_m1, pltpu.roll(g, 1, axis=0))
    g2 = jnp.where(row == 0, g_m2, jnp.where(row == 1, g_m1, pltpu.roll(g, 2, axis=0)))
    cw = cw_ref[...]
    conv = cw[0:1] * g2 + cw[1:2] * g1 + cw[2:3] * g + cb_ref[...]
    act = 0.5 * conv * (1.0 + lax.erf(conv * (2.0 ** -0.5)))
    o_ref[...] = (act * u).astype(o_ref.dtype)


def _ffn_up(xn, w_up, conv_w, conv_b, s_len, tm=512, tn=512):
    m, k = xn.shape
    nn = D_FF // tn
    hb = tm // HALO_ROWS
    return pl.pallas_call(
        functools.partial(_ffn_up_kernel, tm=tm, s_len=s_len),
        grid=(nn, m // tm),
        in_specs=[
            pl.BlockSpec((tm, k), lambda j, i: (i, 0)),
            pl.BlockSpec((HALO_ROWS, k), lambda j, i: (jnp.maximum(i * hb - 1, 0), 0)),
            pl.BlockSpec((k, tn), lambda j, i: (0, j)),
            pl.BlockSpec((k, tn), lambda j, i: (0, nn + j)),
            pl.BlockSpec((3, tn), lambda j, i: (0, j)),
            pl.BlockSpec((1, tn), lambda j, i: (0, j)),
        ],
        out_specs=pl.BlockSpec((tm, tn), lambda j, i: (i, j)),
        out_shape=jax.ShapeDtypeStruct((m, D_FF), BF16),
        compiler_params=pltpu.CompilerParams(
            dimension_semantics=("arbitrary", "arbitrary")),
        name="ffn_up",
    )(xn, xn, w_up, w_up, conv_w, conv_b.reshape(1, D_FF))


def _conv_glu(h, norm_g, w_up, conv_w, conv_b, w_down, s_len):
    xn = _rmsnorm(h, norm_g, BF16)
    act = _ffn_up(xn, w_up.astype(BF16), conv_w, conv_b, s_len)
    return _matmul(act, w_down.astype(BF16), F32, residual=h, tn=512)


def kernel(x, attn_norm, gla_w_in, gla_w_a2, gla_b_a2, gla_head_norm, gla_w_out, kv_norm, w_kv,
           dsa_w_q, dsa_w_out, ffn_norm, ffn_w_up, ffn_conv_w, ffn_conv_b, ffn_w_down, final_norm):
    bsz, s_len, d = x.shape
    t = bsz * s_len
    h = x.reshape(t, d)

    xn = _rmsnorm(h, attn_norm[0], BF16)
    w_in = gla_w_in[0].astype(BF16)
    n_main = 2 * GLA_KEY_DIM + 2 * GLA_VAL_DIM
    proj = _matmul(xn, w_in, BF16, n_cols=n_main)
    w_a1 = jnp.pad(w_in[:, n_main:], ((0, 0), (0, LANES - GATE_RANK)))
    w_a2 = jnp.pad(gla_w_a2[0].astype(BF16), ((0, LANES - GATE_RANK), (0, 0)))
    la = _gla_gate(xn, w_a1, w_a2, gla_b_a2[0])
    o = _gla_recurrence(proj, la, gla_head_norm[0], bsz, s_len)
    h = _matmul(o, gla_w_out[0].astype(BF16), F32, residual=h)
    h = _conv_glu(h, ffn_norm[0], ffn_w_up[0], ffn_conv_w[0], ffn_conv_b[0], ffn_w_down[0], s_len)

    perm_np = _perm_matrix()
    perm = jnp.asarray(perm_np, BF16)
    perm_t = jnp.asarray(perm_np.T, BF16)
    kv = _matmul(_rmsnorm(h, kv_norm, BF16), w_kv.astype(BF16), BF16)
    kv_perm = _permute_rows(kv, perm)

    xq = _rmsnorm(h, attn_norm[1], BF16)
    w_q = dsa_w_q[0].astype(BF16)
    q0 = _matmul(xq, w_q, BF16, n_cols=ATT_WIDTH)
    q12 = _matmul(_permute_rows(xq, perm), w_q, BF16, n_cols=2 * ATT_WIDTH, col_start=ATT_WIDTH)
    o0, lse0 = _dsa_branch(q0, 0, kv, 0, bsz, s_len)
    o1, lse1 = _dsa_branch(q12, 0, kv_perm, 1, bsz, s_len)
    o2, lse2 = _dsa_branch(q12, 1, kv_perm, 2, bsz, s_len)
    o = _dsa_merge([o0, o1, o2], [lse0, lse1, lse2], perm_t)
    h = _matmul(o, dsa_w_out[0].astype(BF16), F32, residual=h)
    h = _conv_glu(h, ffn_norm[1], ffn_w_up[1], ffn_conv_w[1], ffn_conv_b[1], ffn_w_down[1], s_len)

    return _rmsnorm(h, final_norm, F32).reshape(bsz, s_len, d)
```

```python
import functools
import math

import numpy as np
import jax
import jax.numpy as jnp
from jax import lax
from jax.experimental import pallas as pl
from jax.experimental.pallas import tpu as pltpu

D_MODEL = 2048
GLA_HEADS = 4
GLA_KEY_DIM = 1024
GLA_VAL_DIM = 2048
GLA_DK = 256
GLA_DV = 512
GATE_RANK = 16
GATE_NORMALIZER = 16.0
GLA_CHUNK = 64
GLA_BLOCK = 256
ATT_HEADS = 16
HEAD_DIM = 128
ATT_WIDTH = ATT_HEADS * HEAD_DIM
WINDOWS = (128, 512, 2048)
DILATIONS = (1, 4, 16)
ATT_BLOCK = 128
QK_LOOKAHEAD = 2
PERM_TILE = 512
PERM_RUN = PERM_TILE // 16
D_FF = 5632
EPS = 1e-6
LANES = 128
FFN_ROW_SUB = 512
FFN_COL_SUB = 256

F32 = jnp.float32
BF16 = jnp.bfloat16


def _alibi_slopes(n):
    def pow2_slopes(m):
        start = 2.0 ** (-8.0 / m)
        return [start ** (i + 1) for i in range(m)]
    assert math.log2(n).is_integer()
    return [float(v) for v in np.array(pow2_slopes(n), dtype=np.float32)]


def _rmsnorm_kernel(x_ref, g_ref, o_ref):
    x = x_ref[...]
    ms = jnp.mean(x * x, axis=-1, keepdims=True)
    o_ref[...] = ((x * lax.rsqrt(ms + EPS)) * g_ref[...]).astype(o_ref.dtype)


def _rmsnorm(x, g, out_dtype, tm=512):
    m, d = x.shape
    return pl.pallas_call(
        _rmsnorm_kernel,
        grid=(m // tm,),
        in_specs=[pl.BlockSpec((tm, d), lambda i: (i, 0)),
                  pl.BlockSpec((1, d), lambda i: (0, 0))],
        out_specs=pl.BlockSpec((tm, d), lambda i: (i, 0)),
        out_shape=jax.ShapeDtypeStruct((m, d), out_dtype),
        compiler_params=pltpu.CompilerParams(dimension_semantics=("arbitrary",)),
        name="rmsnorm",
    )(x, g.reshape(1, d))


def _cast_weights_once(w_ref, wb_ref):
    @pl.when(pl.program_id(1) == 0)
    def _():
        wb_ref[...] = w_ref[...].astype(BF16)


def _mm_kernel(x_ref, w_ref, o_ref, wb_ref):
    _cast_weights_once(w_ref, wb_ref)
    o_ref[...] = jnp.dot(x_ref[...], wb_ref[...],
                         preferred_element_type=F32).astype(o_ref.dtype)


def _mm_res_kernel(x_ref, w_ref, r_ref, o_ref, wb_ref):
    _cast_weights_once(w_ref, wb_ref)
    acc = jnp.dot(x_ref[...], wb_ref[...], preferred_element_type=F32)
    o_ref[...] = (r_ref[...] + acc).astype(o_ref.dtype)


def _matmul(x, w, layer, out_dtype, n_cols=None, col_start=0, residual=None, tm=512, tn=1024):
    m, k = x.shape
    n = w.shape[2] if n_cols is None else n_cols
    assert w.shape[1] == k and m % tm == 0 and n % tn == 0 and col_start % tn == 0
    j0 = col_start // tn
    in_specs = [pl.BlockSpec((tm, k), lambda j, i: (i, 0)),
                pl.BlockSpec((None, k, tn), lambda j, i: (layer, 0, j0 + j))]
    args = [x, w]
    kern = _mm_kernel
    if residual is not None:
        in_specs.append(pl.BlockSpec((tm, tn), lambda j, i: (i, j)))
        args.append(residual)
        kern = _mm_res_kernel
    return pl.pallas_call(
        kern,
        grid=(n // tn, m // tm),
        in_specs=in_specs,
        out_specs=pl.BlockSpec((tm, tn), lambda j, i: (i, j)),
        out_shape=jax.ShapeDtypeStruct((m, n), out_dtype),
        scratch_shapes=[pltpu.VMEM((k, tn), BF16)],
        compiler_params=pltpu.CompilerParams(
            dimension_semantics=("arbitrary", "arbitrary")),
        name="matmul",
    )(*args)


def _mm_res_norm_kernel(x_ref, w_ref, r_ref, g_ref, h_ref, xn_ref, wb_ref):
    @pl.when(pl.program_id(0) == 0)
    def _():
        wb_ref[...] = w_ref[...].astype(BF16)

    h = r_ref[...] + jnp.dot(x_ref[...], wb_ref[...], preferred_element_type=F32)
    h_ref[...] = h
    ms = jnp.mean(h * h, axis=-1, keepdims=True)
    xn_ref[...] = ((h * lax.rsqrt(ms + EPS)) * g_ref[...]).astype(xn_ref.dtype)


def _matmul_res_norm(x, w, layer, residual, norm_g, tm=512):
    m, k = x.shape
    n = w.shape[2]
    assert w.shape[1] == k and residual.shape == (m, n) and m % tm == 0
    row = pl.BlockSpec((tm, n), lambda i: (i, 0))
    return pl.pallas_call(
        _mm_res_norm_kernel,
        grid=(m // tm,),
        in_specs=[pl.BlockSpec((tm, k), lambda i: (i, 0)),
                  pl.BlockSpec((None, k, n), lambda i: (layer, 0, 0),
                               pipeline_mode=pl.Buffered(1)),
                  row,
                  pl.BlockSpec((1, n), lambda i: (0, 0))],
        out_specs=[row, row],
        out_shape=[jax.ShapeDtypeStruct((m, n), F32), jax.ShapeDtypeStruct((m, n), BF16)],
        scratch_shapes=[pltpu.VMEM((k, n), BF16)],
        compiler_params=pltpu.CompilerParams(dimension_semantics=("arbitrary",)),
        name="matmul_res_norm",
    )(x, w, residual, norm_g.reshape(1, n))


def _norm2_perm_kernel(x_ref, ga_ref, gb_ref, p_ref, a_ref, b_ref, bp_ref):
    x = x_ref[...]
    ms = jnp.mean(x * x, axis=-1, keepdims=True)
    y = x * lax.rsqrt(ms + EPS)
    a_ref[...] = (y * ga_ref[...]).astype(BF16)
    b = (y * gb_ref[...]).astype(BF16)
    b_ref[...] = b
    bp_ref[...] = jnp.dot(p_ref[...], b, preferred_element_type=F32).astype(BF16)


def _rmsnorm2_perm(x, g_a, g_b, perm):
    m, d = x.shape
    tm = PERM_TILE
    row = pl.BlockSpec((tm, d), lambda i: (i, 0))
    vec = pl.BlockSpec((1, d), lambda i: (0, 0))
    out = jax.ShapeDtypeStruct((m, d), BF16)
    return pl.pallas_call(
        _norm2_perm_kernel,
        grid=(m // tm,),
        in_specs=[row, vec, vec, pl.BlockSpec((tm, tm), lambda i: (0, 0))],
        out_specs=[row, row, row],
        out_shape=[out, out, out],
        compiler_params=pltpu.CompilerParams(dimension_semantics=("arbitrary",)),
        name="rmsnorm2_perm",
    )(x, g_a.reshape(1, d), g_b.reshape(1, d), perm)


def _gate_kernel(x_ref, w1_ref, w2_ref, b_ref, o_ref):
    a = jnp.dot(x_ref[...], w1_ref[...], preferred_element_type=F32)
    z = jnp.dot(a.astype(BF16), w2_ref[...], preferred_element_type=F32) + b_ref[...]
    log_sig = jnp.minimum(z, 0.0) - jnp.log1p(jnp.exp(-jnp.abs(z)))
    o_ref[...] = log_sig / GATE_NORMALIZER


def _gla_gate(xn, w_a1, w_a2, b_a2, tm=512):
    m, k = xn.shape
    return pl.pallas_call(
        _gate_kernel,
        grid=(m // tm,),
        in_specs=[pl.BlockSpec((tm, k), lambda i: (i, 0)),
                  pl.BlockSpec((k, LANES), lambda i: (0, 0)),
                  pl.BlockSpec((LANES, GLA_KEY_DIM), lambda i: (0, 0)),
                  pl.BlockSpec((1, GLA_KEY_DIM), lambda i: (0, 0))],
        out_specs=pl.BlockSpec((tm, GLA_KEY_DIM), lambda i: (i, 0)),
        out_shape=jax.ShapeDtypeStruct((m, GLA_KEY_DIM), F32),
        compiler_params=pltpu.CompilerParams(dimension_semantics=("arbitrary",)),
        name="gla_gate",
    )(xn, w_a1, w_a2, b_a2.reshape(1, GLA_KEY_DIM))


def _split3(x):
    hi = x.astype(BF16)
    r1 = x - hi.astype(F32)
    mid = r1.astype(BF16)
    lo = (r1 - mid.astype(F32)).astype(BF16)
    return hi, mid, lo


def _dot01(mat01, x):
    hi, mid, lo = _split3(x)
    return (jnp.dot(mat01, hi, preferred_element_type=F32)
            + jnp.dot(mat01, mid, preferred_element_type=F32)
            + jnp.dot(mat01, lo, preferred_element_type=F32))


def _gla_kernel(q_ref, k_ref, v_ref, r_ref, la_ref, hn_ref, o_ref, st_ref):
    @pl.when(pl.program_id(2) == 0)
    def _():
        st_ref[...] = jnp.zeros_like(st_ref)

    nb = GLA_BLOCK
    row = lax.broadcasted_iota(jnp.int32, (nb, nb), 0)
    col = lax.broadcasted_iota(jnp.int32, (nb, nb), 1)
    same_chunk = (row // GLA_CHUNK) == (col // GLA_CHUNK)
    causal = same_chunk & (col <= row)
    tri = jnp.where(causal, 1.0, 0.0).astype(BF16)

    cum = _dot01(tri, la_ref[...])

    q = q_ref[...].astype(F32) * (GLA_DK ** -0.5)
    k = k_ref[...].astype(F32)
    v = v_ref[...]
    q_dec = (q * jnp.exp(cum)).astype(BF16)
    k_inv = (k * jnp.exp(-cum)).astype(BF16)

    scores = lax.dot_general(q_dec, k_inv, (((1,), (1,)), ((), ())),
                             preferred_element_type=F32)
    scores = jnp.where(causal, scores, 0.0).astype(BF16)
    o_intra = jnp.dot(scores, v, preferred_element_type=F32)

    o_parts = []
    for c in range(nb // GLA_CHUNK):
        lo_r, hi_r = c * GLA_CHUNK, (c + 1) * GLA_CHUNK
        cum_c = cum[lo_r:hi_r]
        last_c = cum[hi_r - 1:hi_r]
        k_end = (k[lo_r:hi_r] * jnp.exp(last_c - cum_c)).astype(BF16)
        st = st_ref[...]
        o_inter = lax.dot_general(q_dec[lo_r:hi_r], st.astype(BF16),
                                  (((1,), (1,)), ((), ())),
                                  preferred_element_type=F32)
        upd = lax.dot_general(v[lo_r:hi_r], k_end, (((0,), (0,)), ((), ())),
                              preferred_element_type=F32)
        st_ref[...] = st * jnp.exp(last_c) + upd
        o_parts.append(o_intra[lo_r:hi_r] + o_inter)
    o = jnp.concatenate(o_parts, axis=0)

    ms = jnp.mean(o * o, axis=-1, keepdims=True)
    o = (o * lax.rsqrt(ms + EPS)) * hn_ref[...]
    r = r_ref[...].astype(F32)
    gate = r * (1.0 / (1.0 + jnp.exp(-r)))
    o_ref[...] = (o * gate).astype(o_ref.dtype)


def _gla_recurrence(proj, la, head_norm, bsz, s_len):
    t = bsz * s_len
    nblk = s_len // GLA_BLOCK
    kq = GLA_KEY_DIM // GLA_DK
    row = lambda b, h, i: b * nblk + i
    return pl.pallas_call(
        _gla_kernel,
        grid=(bsz, GLA_HEADS, nblk),
        in_specs=[
            pl.BlockSpec((GLA_BLOCK, GLA_DK), lambda b, h, i: (row(b, h, i), h)),
            pl.BlockSpec((GLA_BLOCK, GLA_DK), lambda b, h, i: (row(b, h, i), kq + h)),
            pl.BlockSpec((GLA_BLOCK, GLA_DV), lambda b, h, i: (row(b, h, i), kq + h)),
            pl.BlockSpec((GLA_BLOCK, GLA_DV), lambda b, h, i: (row(b, h, i), 2 * kq + h)),
            pl.BlockSpec((GLA_BLOCK, GLA_DK), lambda b, h, i: (row(b, h, i), h)),
            pl.BlockSpec((1, GLA_DV), lambda b, h, i: (0, 0)),
        ],
        out_specs=pl.BlockSpec((GLA_BLOCK, GLA_DV), lambda b, h, i: (row(b, h, i), h)),
        out_shape=jax.ShapeDtypeStruct((t, GLA_VAL_DIM), BF16),
        scratch_shapes=[pltpu.VMEM((GLA_DV, GLA_DK), F32)],
        compiler_params=pltpu.CompilerParams(
            dimension_semantics=("arbitrary", "arbitrary", "arbitrary")),
        name="gla_recurrence",
    )(proj, proj, proj, proj, la, head_norm.reshape(1, GLA_DV))


def _perm_matrix():
    p = np.zeros((PERM_TILE, PERM_TILE), np.float32)
    for i in range(PERM_RUN):
        for a in range(4):
            for r4 in range(4):
                p[(4 * r4 + a) * PERM_RUN + i, 16 * i + 4 * a + r4] = 1.0
    return p


def _permute_kernel(p_ref, x_ref, o_ref):
    o_ref[...] = jnp.dot(p_ref[...], x_ref[...],
                         preferred_element_type=F32).astype(o_ref.dtype)


def _permute_rows(x, perm, tn=1024):
    m, n = x.shape
    return pl.pallas_call(
        _permute_kernel,
        grid=(m // PERM_TILE, n // tn),
        in_specs=[pl.BlockSpec((PERM_TILE, PERM_TILE), lambda i, j: (0, 0)),
                  pl.BlockSpec((PERM_TILE, tn), lambda i, j: (i, j))],
        out_specs=pl.BlockSpec((PERM_TILE, tn), lambda i, j: (i, j)),
        out_shape=jax.ShapeDtypeStruct((m, n), x.dtype),
        compiler_params=pltpu.CompilerParams(
            dimension_semantics=("arbitrary", "arbitrary")),
        name="permute_rows",
    )(perm, x)


def _dsa_kernel(q_ref, kp_ref, kc_ref, vp_ref, vc_ref, o_ref, lse_ref, *, dil, interleaved,
                slopes):
    n = pl.program_id(2)
    nk = 2 * ATT_BLOCK
    rq = lax.broadcasted_iota(jnp.int32, (ATT_BLOCK, nk), 0)
    ck = lax.broadcasted_iota(jnp.int32, (ATT_BLOCK, nk), 1)
    ck_in = ck & (ATT_BLOCK - 1)
    if interleaved:
        sub_q = 4 * (rq & (PERM_RUN - 1)) + (rq >> 5)
        sub_k = 4 * (ck_in & (PERM_RUN - 1)) + (ck_in >> 5)
    else:
        sub_q, sub_k = rq, ck_in
    is_cur = ck >= ATT_BLOCK
    j = sub_q - sub_k + jnp.where(is_cur, 0, ATT_BLOCK)
    valid = (j >= 0) & (j <= ATT_BLOCK) & (is_cur | (n > 0))
    neg_dist = jnp.where(valid, -(j * dil).astype(F32), -jnp.inf)

    lane = lax.broadcasted_iota(jnp.int32, (ATT_BLOCK, LANES), 1)
    ones = jnp.ones((nk, HEAD_DIM), BF16)
    scale = HEAD_DIM ** -0.5
    nt = (((1,), (1,)), ((), ()))

    def head_slice(ref, h):
        x = ref[..., h * HEAD_DIM:(h + 1) * HEAD_DIM]
        return x.reshape(ATT_BLOCK, HEAD_DIM)

    def scores(h):
        kcat = jnp.concatenate([head_slice(kp_ref, h), head_slice(kc_ref, h)], axis=0)
        return lax.dot_general(head_slice(q_ref, h), kcat, nt, preferred_element_type=F32)

    pending = [scores(h) for h in range(QK_LOOKAHEAD)]
    lse_tile = jnp.zeros((ATT_BLOCK, LANES), F32)
    for h in range(ATT_HEADS):
        s = pending.pop(0)
        if h + QK_LOOKAHEAD < ATT_HEADS:
            pending.append(scores(h + QK_LOOKAHEAD))
        z = s * scale + slopes[h] * neg_dist
        m = jnp.max(z, axis=-1, keepdims=True)
        p = jnp.exp(z - m).astype(BF16)
        vcat = jnp.concatenate([head_slice(vp_ref, h), head_slice(vc_ref, h)], axis=0)
        acc = jnp.dot(p, jnp.concatenate([vcat, ones], axis=1), preferred_element_type=F32)
        l = acc[:, HEAD_DIM:]
        o = (acc[:, :HEAD_DIM] / l).astype(o_ref.dtype)
        o_ref[..., h * HEAD_DIM:(h + 1) * HEAD_DIM] = o.reshape(o_ref.shape[:-1] + (HEAD_DIM,))
        lse_tile = jnp.where(lane == h, m + jnp.log(l), lse_tile)
    lse_ref[...] = lse_tile.reshape(lse_ref.shape)


def _dsa_branch(q, q_col, kv, g, bsz, s_len):
    d = DILATIONS[g]
    assert WINDOWS[g] // d == ATT_BLOCK
    t = bsz * s_len
    ntile = s_len // PERM_TILE
    prev = lambda n: jnp.maximum(n - 1, 0)
    if g == 0:
        view = lambda x: x.reshape(bsz, s_len, x.shape[-1])
        blk = lambda w: (None, ATT_BLOCK, w)
        grid = (bsz, 1, s_len // ATT_BLOCK)
        at = lambda c, n_of: (lambda b, r, n: (b, n_of(n), c))
    elif g == 1:
        view = lambda x: x.reshape(bsz, ntile, 4, 4, PERM_RUN, x.shape[-1])
        blk = lambda w: (None, None, None, 4, PERM_RUN, w)
        grid = (bsz, 4, ntile)
        at = lambda c, n_of: (lambda b, r, n: (b, n_of(n), r, 0, 0, c))
    else:
        view = lambda x: x.reshape(bsz, ntile, 16, PERM_RUN, x.shape[-1])
        blk = lambda w: (None, 4, None, PERM_RUN, w)
        grid = (bsz, 16, ntile // 4)
        at = lambda c, n_of: (lambda b, r, n: (b, n_of(n), r, 0, c))
    cur = lambda n: n
    qv, kvv = view(q), view(kv)
    o, lse = pl.pallas_call(
        functools.partial(_dsa_kernel, dil=d, interleaved=(g == 1),
                          slopes=_alibi_slopes(ATT_HEADS)),
        grid=grid,
        in_specs=[
            pl.BlockSpec(blk(ATT_WIDTH), at(q_col, cur)),
            pl.BlockSpec(blk(ATT_WIDTH), at(0, prev)),
            pl.BlockSpec(blk(ATT_WIDTH), at(0, cur)),
            pl.BlockSpec(blk(ATT_WIDTH), at(1, prev)),
            pl.BlockSpec(blk(ATT_WIDTH), at(1, cur)),
        ],
        out_specs=[pl.BlockSpec(blk(ATT_WIDTH), at(0, cur)),
                   pl.BlockSpec(blk(LANES), at(0, cur))],
        out_shape=[jax.ShapeDtypeStruct(qv.shape[:-1] + (ATT_WIDTH,), BF16),
                   jax.ShapeDtypeStruct(qv.shape[:-1] + (LANES,), F32)],
        compiler_params=pltpu.CompilerParams(
            dimension_semantics=("arbitrary", "arbitrary", "arbitrary")),
        name=f"dsa_branch{g}",
    )(qv, kvv, kvv, kvv, kvv)
    return o.reshape(t, ATT_WIDTH), lse.reshape(t, LANES)


def _merge_kernel(pt_ref, o0_ref, o1_ref, o2_ref, l0_ref, l1_ref, l2_ref, o_ref):
    pt = pt_ref[...]
    l0 = l0_ref[...]
    l1 = _dot01(pt, l1_ref[...])
    l2 = _dot01(pt, l2_ref[...])
    m = jnp.maximum(jnp.maximum(l0, l1), l2)
    e0, e1, e2 = jnp.exp(l0 - m), jnp.exp(l1 - m), jnp.exp(l2 - m)
    den = e0 + e1 + e2
    w0, w1, w2 = e0 / den, e1 / den, e2 / den
    for h in range(ATT_HEADS):
        sl = slice(h * HEAD_DIM, (h + 1) * HEAD_DIM)
        o1 = jnp.dot(pt, o1_ref[:, sl], preferred_element_type=F32)
        o2 = jnp.dot(pt, o2_ref[:, sl], preferred_element_type=F32)
        o = (w0[:, h:h + 1] * o0_ref[:, sl].astype(F32)
             + w1[:, h:h + 1] * o1 + w2[:, h:h + 1] * o2)
        o_ref[:, sl] = o.astype(o_ref.dtype)


def _dsa_merge(outs, lses, perm_t):
    t, hd = outs[0].shape
    tm = PERM_TILE
    o_spec = pl.BlockSpec((tm, hd), lambda i: (i, 0))
    l_spec = pl.BlockSpec((tm, LANES), lambda i: (i, 0))
    return pl.pallas_call(
        _merge_kernel,
        grid=(t // tm,),
        in_specs=[pl.BlockSpec((tm, tm), lambda i: (0, 0)),
                  o_spec, o_spec, o_spec, l_spec, l_spec, l_spec],
        out_specs=o_spec,
        out_shape=jax.ShapeDtypeStruct((t, hd), BF16),
        compiler_params=pltpu.CompilerParams(dimension_semantics=("arbitrary",)),
        name="dsa_merge",
    )(perm_t, *outs, *lses)


def _ffn_up_kernel(x_ref, wu_ref, wg_ref, cw_ref, cb_ref, o_ref, wub_ref, wgb_ref,
                   us_ref, gs_ref, carry_ref, *, tm, tn, s_len):
    @pl.when(pl.program_id(1) == 0)
    def _():
        wub_ref[...] = wu_ref[...].astype(BF16)
        wgb_ref[...] = wg_ref[...].astype(BF16)

    @pl.when((pl.program_id(1) * tm) % s_len == 0)
    def _():
        carry_ref[...] = jnp.zeros_like(carry_ref)

    n_rb = tm // FFN_ROW_SUB
    subs = [(cb, rb) for cb in range(tn // FFN_COL_SUB) for rb in range(n_rb)]

    def window(idx):
        cb, rb = subs[idx]
        return (slice(rb * FFN_ROW_SUB, (rb + 1) * FFN_ROW_SUB),
                slice(cb * FFN_COL_SUB, (cb + 1) * FFN_COL_SUB))

    def project(idx):
        rows, cols = window(idx)
        slot = idx % 2
        xr = x_ref[rows, :]
        us_ref[slot] = jnp.dot(xr, wub_ref[:, cols], preferred_element_type=F32)
        gs_ref[slot, 8:] = jnp.dot(xr, wgb_ref[:, cols], preferred_element_type=F32)
        if subs[idx][1] == 0:
            gs_ref[slot, :8] = carry_ref[:, cols]
        else:
            gs_ref[slot, :8] = gs_ref[1 - slot, FFN_ROW_SUB:]
        if subs[idx][1] == n_rb - 1:
            carry_ref[:, cols] = gs_ref[slot, FFN_ROW_SUB:]

    def epilogue(idx):
        rows, cols = window(idx)
        slot = idx % 2
        g = gs_ref[slot, 8:]
        g1 = gs_ref[slot, 7:7 + FFN_ROW_SUB]
        g2 = gs_ref[slot, 6:6 + FFN_ROW_SUB]
        cw = cw_ref[:, cols]
        conv = cw[0:1] * g2 + cw[1:2] * g1 + cw[2:3] * g + cb_ref[:, cols]
        act = 0.5 * conv * (1.0 + lax.erf(conv * (2.0 ** -0.5)))
        o_ref[rows, cols] = (act * us_ref[slot]).astype(o_ref.dtype)

    project(0)
    for idx in range(1, len(subs)):
        project(idx)
        epilogue(idx - 1)
    epilogue(len(subs) - 1)


def _ffn_up(xn, w_up, layer, conv_w, conv_b, s_len, tm=1024, tn=512):
    m, k = xn.shape
    nn = D_FF // tn
    assert s_len % tm == 0 and tm % FFN_ROW_SUB == 0 and tn % FFN_COL_SUB == 0
    return pl.pallas_call(
        functools.partial(_ffn_up_kernel, tm=tm, tn=tn, s_len=s_len),
        grid=(nn, m // tm),
        in_specs=[
            pl.BlockSpec((tm, k), lambda j, i: (i, 0)),
            pl.BlockSpec((None, k, tn), lambda j, i: (layer, 0, j)),
            pl.BlockSpec((None, k, tn), lambda j, i: (layer, 0, nn + j)),
            pl.BlockSpec((3, tn), lambda j, i: (0, j)),
            pl.BlockSpec((1, tn), lambda j, i: (0, j)),
        ],
        out_specs=pl.BlockSpec((tm, tn), lambda j, i: (i, j)),
        out_shape=jax.ShapeDtypeStruct((m, D_FF), BF16),
        scratch_shapes=[pltpu.VMEM((k, tn), BF16), pltpu.VMEM((k, tn), BF16),
                        pltpu.VMEM((2, FFN_ROW_SUB, FFN_COL_SUB), F32),
                        pltpu.VMEM((2, 8 + FFN_ROW_SUB, FFN_COL_SUB), F32),
                        pltpu.VMEM((8, tn), F32)],
        compiler_params=pltpu.CompilerParams(
            dimension_semantics=("arbitrary", "arbitrary")),
        name="ffn_up",
    )(xn, w_up, w_up, conv_w, conv_b.reshape(1, D_FF))


def _conv_glu(h, xn, w_up, layer, conv_w, conv_b, w_down, s_len):
    act = _ffn_up(xn, w_up, layer, conv_w, conv_b, s_len)
    return _matmul(act, w_down, layer, F32, residual=h, tn=512)


def kernel(x, attn_norm, gla_w_in, gla_w_a2, gla_b_a2, gla_head_norm, gla_w_out, kv_norm, w_kv,
           dsa_w_q, dsa_w_out, ffn_norm, ffn_w_up, ffn_conv_w, ffn_conv_b, ffn_w_down, final_norm):
    bsz, s_len, d = x.shape
    t = bsz * s_len
    h = x.reshape(t, d)

    xn = _rmsnorm(h, attn_norm[0], BF16)
    n_main = 2 * GLA_KEY_DIM + 2 * GLA_VAL_DIM
    proj = _matmul(xn, gla_w_in, 0, BF16, n_cols=n_main)
    w_a1 = jnp.pad(gla_w_in[0, :, n_main:].astype(BF16), ((0, 0), (0, LANES - GATE_RANK)))
    w_a2 = jnp.pad(gla_w_a2[0].astype(BF16), ((0, LANES - GATE_RANK), (0, 0)))
    la = _gla_gate(xn, w_a1, w_a2, gla_b_a2[0])
    o = _gla_recurrence(proj, la, gla_head_norm[0], bsz, s_len)
    h, xn = _matmul_res_norm(o, gla_w_out, 0, h, ffn_norm[0])
    h = _conv_glu(h, xn, ffn_w_up, 0, ffn_conv_w[0], ffn_conv_b[0], ffn_w_down, s_len)

    perm_np = _perm_matrix()
    perm = jnp.asarray(perm_np, BF16)
    perm_t = jnp.asarray(perm_np.T, BF16)
    xkv, xq, xq_perm = _rmsnorm2_perm(h, kv_norm, attn_norm[1], perm)
    kv = _matmul(xkv, w_kv[None], 0, BF16)
    kv_perm = _permute_rows(kv, perm)

    q0 = _matmul(xq, dsa_w_q, 0, BF16, n_cols=ATT_WIDTH)
    q12 = _matmul(xq_perm, dsa_w_q, 0, BF16, n_cols=2 * ATT_WIDTH, col_start=ATT_WIDTH)
    o0, lse0 = _dsa_branch(q0, 0, kv, 0, bsz, s_len)
    o1, lse1 = _dsa_branch(q12, 0, kv_perm, 1, bsz, s_len)
    o2, lse2 = _dsa_branch(q12, 1, kv_perm, 2, bsz, s_len)
    o = _dsa_merge([o0, o1, o2], [lse0, lse1, lse2], perm_t)
    h, xn = _matmul_res_norm(o, dsa_w_out, 0, h, ffn_norm[1])
    h = _conv_glu(h, xn, ffn_w_up, 1, ffn_conv_w[1], ffn_conv_b[1], ffn_w_down, s_len)

    return _rmsnorm(h, final_norm, F32).reshape(bsz, s_len, d)
```

```python
import functools
import math

import numpy as np
import jax
import jax.numpy as jnp
from jax import lax
from jax.experimental import pallas as pl
from jax.experimental.pallas import tpu as pltpu

D_MODEL = 2048
GLA_HEADS = 4
GLA_KEY_DIM = 1024
GLA_VAL_DIM = 2048
GLA_DK = 256
GLA_DV = 512
GATE_RANK = 16
GATE_NORMALIZER = 16.0
GLA_CHUNK = 64
GLA_BLOCK = 256
ATT_HEADS = 16
HEAD_DIM = 128
ATT_WIDTH = ATT_HEADS * HEAD_DIM
WINDOWS = (128, 512, 2048)
DILATIONS = (1, 4, 16)
ATT_BLOCK = 128
QK_LOOKAHEAD = 2
PERM_TILE = 512
PERM_RUN = PERM_TILE // 16
D_FF = 5632
EPS = 1e-6
LANES = 128
FFN_ROW_SUB = 512
FFN_COL_SUB = 256

F32 = jnp.float32
BF16 = jnp.bfloat16


def _alibi_slopes(n):
    def pow2_slopes(m):
        start = 2.0 ** (-8.0 / m)
        return [start ** (i + 1) for i in range(m)]
    assert math.log2(n).is_integer()
    return [float(v) for v in np.array(pow2_slopes(n), dtype=np.float32)]


def _rmsnorm_kernel(x_ref, g_ref, o_ref):
    x = x_ref[...]
    ms = jnp.mean(x * x, axis=-1, keepdims=True)
    o_ref[...] = ((x * lax.rsqrt(ms + EPS)) * g_ref[...]).astype(o_ref.dtype)


def _rmsnorm(x, g, out_dtype, tm=512):
    m, d = x.shape
    return pl.pallas_call(
        _rmsnorm_kernel,
        grid=(m // tm,),
        in_specs=[pl.BlockSpec((tm, d), lambda i: (i, 0)),
                  pl.BlockSpec((1, d), lambda i: (0, 0))],
        out_specs=pl.BlockSpec((tm, d), lambda i: (i, 0)),
        out_shape=jax.ShapeDtypeStruct((m, d), out_dtype),
        compiler_params=pltpu.CompilerParams(dimension_semantics=("arbitrary",)),
        name="rmsnorm",
    )(x, g.reshape(1, d))


def _cast_weights_once(w_ref, wb_ref):
    @pl.when(pl.program_id(1) == 0)
    def _():
        wb_ref[...] = w_ref[...].astype(BF16)


def _mm_kernel(x_ref, w_ref, *rest, w_is_nk):
    o_ref, wb_ref = rest[-2:]
    _cast_weights_once(w_ref, wb_ref)
    contract = (((1,), (1 if w_is_nk else 0,)), ((), ()))
    acc = lax.dot_general(x_ref[...], wb_ref[...], contract, preferred_element_type=F32)
    if len(rest) == 3:
        acc = rest[0][...] + acc
    o_ref[...] = acc.astype(o_ref.dtype)


def _matmul(x, w, layer, out_dtype, n_cols=None, col_start=0, residual=None, w_is_nk=False,
            tm=512, tn=1024):
    m, k = x.shape
    n_all, k_w = (w.shape[1], w.shape[2]) if w_is_nk else (w.shape[2], w.shape[1])
    n = n_all if n_cols is None else n_cols
    assert k_w == k and m % tm == 0 and n % tn == 0 and col_start % tn == 0
    j0 = col_start // tn
    if w_is_nk:
        w_spec = pl.BlockSpec((None, tn, k), lambda j, i: (layer, j0 + j, 0))
        wb_shape = (tn, k)
    else:
        w_spec = pl.BlockSpec((None, k, tn), lambda j, i: (layer, 0, j0 + j))
        wb_shape = (k, tn)
    in_specs = [pl.BlockSpec((tm, k), lambda j, i: (i, 0)), w_spec]
    args = [x, w]
    if residual is not None:
        in_specs.append(pl.BlockSpec((tm, tn), lambda j, i: (i, j)))
        args.append(residual)
    return pl.pallas_call(
        functools.partial(_mm_kernel, w_is_nk=w_is_nk),
        grid=(n // tn, m // tm),
        in_specs=in_specs,
        out_specs=pl.BlockSpec((tm, tn), lambda j, i: (i, j)),
        out_shape=jax.ShapeDtypeStruct((m, n), out_dtype),
        scratch_shapes=[pltpu.VMEM(wb_shape, BF16)],
        compiler_params=pltpu.CompilerParams(
            dimension_semantics=("arbitrary", "arbitrary")),
        name="matmul",
    )(*args)


def _mm_perm_kernel(x_ref, w_ref, p_ref, o_ref, op_ref, wb_ref):
    _cast_weights_once(w_ref, wb_ref)
    out = jnp.dot(x_ref[...], wb_ref[...], preferred_element_type=F32).astype(BF16)
    o_ref[...] = out
    op_ref[...] = jnp.dot(p_ref[...], out, preferred_element_type=F32).astype(BF16)


def _matmul_and_permuted(x, w, layer, perm, tn=1024):
    m, k = x.shape
    n = w.shape[2]
    tm = PERM_TILE
    assert w.shape[1] == k and m % tm == 0 and n % tn == 0
    out_spec = pl.BlockSpec((tm, tn), lambda j, i: (i, j))
    out = jax.ShapeDtypeStruct((m, n), BF16)
    return pl.pallas_call(
        _mm_perm_kernel,
        grid=(n // tn, m // tm),
        in_specs=[pl.BlockSpec((tm, k), lambda j, i: (i, 0)),
                  pl.BlockSpec((None, k, tn), lambda j, i: (layer, 0, j)),
                  pl.BlockSpec((tm, tm), lambda j, i: (0, 0))],
        out_specs=[out_spec, out_spec],
        out_shape=[out, out],
        scratch_shapes=[pltpu.VMEM((k, tn), BF16)],
        compiler_params=pltpu.CompilerParams(
            dimension_semantics=("arbitrary", "arbitrary")),
        name="matmul_and_permuted",
    )(x, w, perm)


def _mm_res_norm_kernel(x_ref, w_ref, r_ref, g_ref, h_ref, xn_ref, wb_ref):
    @pl.when(pl.program_id(0) == 0)
    def _():
        wb_ref[...] = w_ref[...].astype(BF16)

    h = r_ref[...] + jnp.dot(x_ref[...], wb_ref[...], preferred_element_type=F32)
    h_ref[...] = h
    ms = jnp.mean(h * h, axis=-1, keepdims=True)
    xn_ref[...] = ((h * lax.rsqrt(ms + EPS)) * g_ref[...]).astype(xn_ref.dtype)


def _matmul_res_norm(x, w, layer, residual, norm_g, tm=512):
    m, k = x.shape
    n = w.shape[2]
    assert w.shape[1] == k and residual.shape == (m, n) and m % tm == 0
    row = pl.BlockSpec((tm, n), lambda i: (i, 0))
    return pl.pallas_call(
        _mm_res_norm_kernel,
        grid=(m // tm,),
        in_specs=[pl.BlockSpec((tm, k), lambda i: (i, 0)),
                  pl.BlockSpec((None, k, n), lambda i: (layer, 0, 0),
                               pipeline_mode=pl.Buffered(1)),
                  row,
                  pl.BlockSpec((1, n), lambda i: (0, 0))],
        out_specs=[row, row],
        out_shape=[jax.ShapeDtypeStruct((m, n), F32), jax.ShapeDtypeStruct((m, n), BF16)],
        scratch_shapes=[pltpu.VMEM((k, n), BF16)],
        compiler_params=pltpu.CompilerParams(dimension_semantics=("arbitrary",)),
        name="matmul_res_norm",
    )(x, w, residual, norm_g.reshape(1, n))


def _norm2_perm_kernel(x_ref, ga_ref, gb_ref, p_ref, a_ref, b_ref, bp_ref):
    x = x_ref[...]
    ms = jnp.mean(x * x, axis=-1, keepdims=True)
    y = x * lax.rsqrt(ms + EPS)
    a_ref[...] = (y * ga_ref[...]).astype(BF16)
    b = (y * gb_ref[...]).astype(BF16)
    b_ref[...] = b
    bp_ref[...] = jnp.dot(p_ref[...], b, preferred_element_type=F32).astype(BF16)


def _rmsnorm2_perm(x, g_a, g_b, perm):
    m, d = x.shape
    tm = PERM_TILE
    row = pl.BlockSpec((tm, d), lambda i: (i, 0))
    vec = pl.BlockSpec((1, d), lambda i: (0, 0))
    out = jax.ShapeDtypeStruct((m, d), BF16)
    return pl.pallas_call(
        _norm2_perm_kernel,
        grid=(m // tm,),
        in_specs=[row, vec, vec, pl.BlockSpec((tm, tm), lambda i: (0, 0))],
        out_specs=[row, row, row],
        out_shape=[out, out, out],
        compiler_params=pltpu.CompilerParams(dimension_semantics=("arbitrary",)),
        name="rmsnorm2_perm",
    )(x, g_a.reshape(1, d), g_b.reshape(1, d), perm)


def _gate_kernel(x_ref, w1_ref, w2_ref, b_ref, o_ref):
    a = jnp.dot(x_ref[...], w1_ref[...], preferred_element_type=F32)
    z = jnp.dot(a.astype(BF16), w2_ref[...], preferred_element_type=F32) + b_ref[...]
    log_sig = jnp.minimum(z, 0.0) - jnp.log1p(jnp.exp(-jnp.abs(z)))
    o_ref[...] = log_sig / GATE_NORMALIZER


def _gla_gate(xn, w_a1, w_a2, b_a2, tm=512):
    m, k = xn.shape
    return pl.pallas_call(
        _gate_kernel,
        grid=(m // tm,),
        in_specs=[pl.BlockSpec((tm, k), lambda i: (i, 0)),
                  pl.BlockSpec((k, LANES), lambda i: (0, 0)),
                  pl.BlockSpec((LANES, GLA_KEY_DIM), lambda i: (0, 0)),
                  pl.BlockSpec((1, GLA_KEY_DIM), lambda i: (0, 0))],
        out_specs=pl.BlockSpec((tm, GLA_KEY_DIM), lambda i: (i, 0)),
        out_shape=jax.ShapeDtypeStruct((m, GLA_KEY_DIM), F32),
        compiler_params=pltpu.CompilerParams(dimension_semantics=("arbitrary",)),
        name="gla_gate",
    )(xn, w_a1, w_a2, b_a2.reshape(1, GLA_KEY_DIM))


def _split3(x):
    hi = x.astype(BF16)
    r1 = x - hi.astype(F32)
    mid = r1.astype(BF16)
    lo = (r1 - mid.astype(F32)).astype(BF16)
    return hi, mid, lo


def _dot01(mat01, x):
    hi, mid, lo = _split3(x)
    return (jnp.dot(mat01, hi, preferred_element_type=F32)
            + jnp.dot(mat01, mid, preferred_element_type=F32)
            + jnp.dot(mat01, lo, preferred_element_type=F32))


def _gla_kernel(q_ref, k_ref, v_ref, r_ref, la_ref, hn_ref, o_ref, st_ref):
    @pl.when(pl.program_id(2) == 0)
    def _():
        st_ref[...] = jnp.zeros_like(st_ref)

    nb = GLA_BLOCK
    row = lax.broadcasted_iota(jnp.int32, (nb, nb), 0)
    col = lax.broadcasted_iota(jnp.int32, (nb, nb), 1)
    same_chunk = (row // GLA_CHUNK) == (col // GLA_CHUNK)
    causal = same_chunk & (col <= row)
    tri = jnp.where(causal, 1.0, 0.0).astype(BF16)

    cum = _dot01(tri, la_ref[...])

    q = q_ref[...].astype(F32) * (GLA_DK ** -0.5)
    k = k_ref[...].astype(F32)
    v = v_ref[...]
    q_dec = (q * jnp.exp(cum)).astype(BF16)
    k_inv = (k * jnp.exp(-cum)).astype(BF16)

    scores = lax.dot_general(q_dec, k_inv, (((1,), (1,)), ((), ())),
                             preferred_element_type=F32)
    scores = jnp.where(causal, scores, 0.0).astype(BF16)
    o_intra = jnp.dot(scores, v, preferred_element_type=F32)

    o_parts = []
    for c in range(nb // GLA_CHUNK):
        lo_r, hi_r = c * GLA_CHUNK, (c + 1) * GLA_CHUNK
        cum_c = cum[lo_r:hi_r]
        last_c = cum[hi_r - 1:hi_r]
        k_end = (k[lo_r:hi_r] * jnp.exp(last_c - cum_c)).astype(BF16)
        st = st_ref[...]
        o_inter = lax.dot_general(q_dec[lo_r:hi_r], st.astype(BF16),
                                  (((1,), (1,)), ((), ())),
                                  preferred_element_type=F32)
        upd = lax.dot_general(v[lo_r:hi_r], k_end, (((0,), (0,)), ((), ())),
                              preferred_element_type=F32)
        st_ref[...] = st * jnp.exp(last_c) + upd
        o_parts.append(o_intra[lo_r:hi_r] + o_inter)
    o = jnp.concatenate(o_parts, axis=0)

    ms = jnp.mean(o * o, axis=-1, keepdims=True)
    o = (o * lax.rsqrt(ms + EPS)) * hn_ref[...]
    r = r_ref[...].astype(F32)
    gate = r * (1.0 / (1.0 + jnp.exp(-r)))
    o_ref[...] = (o * gate).astype(o_ref.dtype)


def _gla_recurrence(proj, la, head_norm, bsz, s_len):
    t = bsz * s_len
    nblk = s_len // GLA_BLOCK
    kq = GLA_KEY_DIM // GLA_DK
    row = lambda b, h, i: b * nblk + i
    return pl.pallas_call(
        _gla_kernel,
        grid=(bsz, GLA_HEADS, nblk),
        in_specs=[
            pl.BlockSpec((GLA_BLOCK, GLA_DK), lambda b, h, i: (row(b, h, i), h)),
            pl.BlockSpec((GLA_BLOCK, GLA_DK), lambda b, h, i: (row(b, h, i), kq + h)),
            pl.BlockSpec((GLA_BLOCK, GLA_DV), lambda b, h, i: (row(b, h, i), kq + h)),
            pl.BlockSpec((GLA_BLOCK, GLA_DV), lambda b, h, i: (row(b, h, i), 2 * kq + h)),
            pl.BlockSpec((GLA_BLOCK, GLA_DK), lambda b, h, i: (row(b, h, i), h)),
            pl.BlockSpec((1, GLA_DV), lambda b, h, i: (0, 0)),
        ],
        out_specs=pl.BlockSpec((GLA_BLOCK, GLA_DV), lambda b, h, i: (row(b, h, i), h)),
        out_shape=jax.ShapeDtypeStruct((t, GLA_VAL_DIM), BF16),
        scratch_shapes=[pltpu.VMEM((GLA_DV, GLA_DK), F32)],
        compiler_params=pltpu.CompilerParams(
            dimension_semantics=("arbitrary", "arbitrary", "arbitrary")),
        name="gla_recurrence",
    )(proj, proj, proj, proj, la, head_norm.reshape(1, GLA_DV))


def _perm_matrix():
    p = np.zeros((PERM_TILE, PERM_TILE), np.float32)
    for i in range(PERM_RUN):
        for a in range(4):
            for r4 in range(4):
                p[(4 * r4 + a) * PERM_RUN + i, 16 * i + 4 * a + r4] = 1.0
    return p


def _dsa_kernel(q_ref, kc_ref, vc_ref, o_ref, lse_ref, kp_ref, vp_ref, *, dil, interleaved,
                slopes):
    n = pl.program_id(2)

    @pl.when(n == 0)
    def _():
        kp_ref[...] = jnp.zeros_like(kp_ref)
        vp_ref[...] = jnp.zeros_like(vp_ref)

    nk = 2 * ATT_BLOCK
    rq = lax.broadcasted_iota(jnp.int32, (ATT_BLOCK, nk), 0)
    ck = lax.broadcasted_iota(jnp.int32, (ATT_BLOCK, nk), 1)
    ck_in = ck & (ATT_BLOCK - 1)
    if interleaved:
        sub_q = 4 * (rq & (PERM_RUN - 1)) + (rq >> 5)
        sub_k = 4 * (ck_in & (PERM_RUN - 1)) + (ck_in >> 5)
    else:
        sub_q, sub_k = rq, ck_in
    is_cur = ck >= ATT_BLOCK
    j = sub_q - sub_k + jnp.where(is_cur, 0, ATT_BLOCK)
    valid = (j >= 0) & (j <= ATT_BLOCK) & (is_cur | (n > 0))
    neg_dist = jnp.where(valid, -(j * dil).astype(F32), -jnp.inf)

    lane = lax.broadcasted_iota(jnp.int32, (ATT_BLOCK, LANES), 1)
    ones = jnp.ones((nk, HEAD_DIM), BF16)
    scale = HEAD_DIM ** -0.5
    nt = (((1,), (1,)), ((), ()))

    def head_slice(ref, h):
        x = ref[..., h * HEAD_DIM:(h + 1) * HEAD_DIM]
        return x.reshape(ATT_BLOCK, HEAD_DIM)

    def scores(h):
        kcat = jnp.concatenate([head_slice(kp_ref, h), head_slice(kc_ref, h)], axis=0)
        return lax.dot_general(head_slice(q_ref, h), kcat, nt, preferred_element_type=F32)

    pending = [scores(h) for h in range(QK_LOOKAHEAD)]
    lse_tile = jnp.zeros((ATT_BLOCK, LANES), F32)
    for h in range(ATT_HEADS):
        s = pending.pop(0)
        if h + QK_LOOKAHEAD < ATT_HEADS:
            pending.append(scores(h + QK_LOOKAHEAD))
        z = s * scale + slopes[h] * neg_dist
        m = jnp.max(z, axis=-1, keepdims=True)
        p = jnp.exp(z - m).astype(BF16)
        vcat = jnp.concatenate([head_slice(vp_ref, h), head_slice(vc_ref, h)], axis=0)
        acc = jnp.dot(p, jnp.concatenate([vcat, ones], axis=1), preferred_element_type=F32)
        l = acc[:, HEAD_DIM:]
        o = (acc[:, :HEAD_DIM] / l).astype(o_ref.dtype)
        o_ref[..., h * HEAD_DIM:(h + 1) * HEAD_DIM] = o.reshape(o_ref.shape[:-1] + (HEAD_DIM,))
        lse_tile = jnp.where(lane == h, m + jnp.log(l), lse_tile)
    lse_ref[...] = lse_tile.reshape(lse_ref.shape)
    kp_ref[...] = kc_ref[...].reshape(kp_ref.shape)
    vp_ref[...] = vc_ref[...].reshape(vp_ref.shape)


def _dsa_branch(q, q_col, kv, g, bsz, s_len):
    d = DILATIONS[g]
    assert WINDOWS[g] // d == ATT_BLOCK
    t = bsz * s_len
    ntile = s_len // PERM_TILE
    if g == 0:
        view = lambda x: x.reshape(bsz, s_len, x.shape[-1])
        blk = lambda w: (None, ATT_BLOCK, w)
        grid = (bsz, 1, s_len // ATT_BLOCK)
        at = lambda c: (lambda b, r, n: (b, n, c))
    elif g == 1:
        view = lambda x: x.reshape(bsz, ntile, 4, 4, PERM_RUN, x.shape[-1])
        blk = lambda w: (None, None, None, 4, PERM_RUN, w)
        grid = (bsz, 4, ntile)
        at = lambda c: (lambda b, r, n: (b, n, r, 0, 0, c))
    else:
        view = lambda x: x.reshape(bsz, ntile, 16, PERM_RUN, x.shape[-1])
        blk = lambda w: (None, 4, None, PERM_RUN, w)
        grid = (bsz, 16, ntile // 4)
        at = lambda c: (lambda b, r, n: (b, n, r, 0, c))
    qv, kvv = view(q), view(kv)
    o, lse = pl.pallas_call(
        functools.partial(_dsa_kernel, dil=d, interleaved=(g == 1),
                          slopes=_alibi_slopes(ATT_HEADS)),
        grid=grid,
        in_specs=[
            pl.BlockSpec(blk(ATT_WIDTH), at(q_col)),
            pl.BlockSpec(blk(ATT_WIDTH), at(0)),
            pl.BlockSpec(blk(ATT_WIDTH), at(1)),
        ],
        out_specs=[pl.BlockSpec(blk(ATT_WIDTH), at(0)),
                   pl.BlockSpec(blk(LANES), at(0))],
        out_shape=[jax.ShapeDtypeStruct(qv.shape[:-1] + (ATT_WIDTH,), BF16),
                   jax.ShapeDtypeStruct(qv.shape[:-1] + (LANES,), F32)],
        scratch_shapes=[pltpu.VMEM((ATT_BLOCK, ATT_WIDTH), BF16),
                        pltpu.VMEM((ATT_BLOCK, ATT_WIDTH), BF16)],
        compiler_params=pltpu.CompilerParams(
            dimension_semantics=("arbitrary", "arbitrary", "arbitrary")),
        name=f"dsa_branch{g}",
    )(qv, kvv, kvv)
    return o.reshape(t, ATT_WIDTH), lse.reshape(t, LANES)


def _merge_kernel(pt_ref, o0_ref, o1_ref, o2_ref, l0_ref, l1_ref, l2_ref, o_ref):
    pt = pt_ref[...]
    l0 = l0_ref[...]
    l1 = _dot01(pt, l1_ref[...])
    l2 = _dot01(pt, l2_ref[...])
    m = jnp.maximum(jnp.maximum(l0, l1), l2)
    e0, e1, e2 = jnp.exp(l0 - m), jnp.exp(l1 - m), jnp.exp(l2 - m)
    den = e0 + e1 + e2
    w0, w1, w2 = e0 / den, e1 / den, e2 / den
    for hp in range(ATT_HEADS // 2):
        pair = slice(2 * hp * HEAD_DIM, (2 * hp + 2) * HEAD_DIM)
        o1 = jnp.dot(pt, o1_ref[:, pair], preferred_element_type=F32)
        o2 = jnp.dot(pt, o2_ref[:, pair], preferred_element_type=F32)
        for k in range(2):
            h = 2 * hp + k
            sl = slice(h * HEAD_DIM, (h + 1) * HEAD_DIM)
            in_pair = slice(k * HEAD_DIM, (k + 1) * HEAD_DIM)
            o = (w0[:, h:h + 1] * o0_ref[:, sl].astype(F32)
                 + w1[:, h:h + 1] * o1[:, in_pair] + w2[:, h:h + 1] * o2[:, in_pair])
            o_ref[:, sl] = o.astype(o_ref.dtype)


def _dsa_merge(outs, lses, perm_t):
    t, hd = outs[0].shape
    tm = PERM_TILE
    o_spec = pl.BlockSpec((tm, hd), lambda i: (i, 0))
    l_spec = pl.BlockSpec((tm, LANES), lambda i: (i, 0))
    return pl.pallas_call(
        _merge_kernel,
        grid=(t // tm,),
        in_specs=[pl.BlockSpec((tm, tm), lambda i: (0, 0)),
                  o_spec, o_spec, o_spec, l_spec, l_spec, l_spec],
        out_specs=o_spec,
        out_shape=jax.ShapeDtypeStruct((t, hd), BF16),
        compiler_params=pltpu.CompilerParams(dimension_semantics=("arbitrary",)),
        name="dsa_merge",
    )(perm_t, *outs, *lses)


def _ffn_up_kernel(x_ref, wu_ref, wg_ref, cw_ref, cb_ref, o_ref, wub_ref, wgb_ref,
                   us_ref, gs_ref, carry_ref, *, tm, tn, s_len):
    @pl.when(pl.program_id(1) == 0)
    def _():
        wub_ref[...] = wu_ref[...].astype(BF16)
        wgb_ref[...] = wg_ref[...].astype(BF16)

    @pl.when((pl.program_id(1) * tm) % s_len == 0)
    def _():
        carry_ref[...] = jnp.zeros_like(carry_ref)

    n_rb = tm // FFN_ROW_SUB
    subs = [(cb, rb) for cb in range(tn // FFN_COL_SUB) for rb in range(n_rb)]

    def window(idx):
        cb, rb = subs[idx]
        return (slice(rb * FFN_ROW_SUB, (rb + 1) * FFN_ROW_SUB),
                slice(cb * FFN_COL_SUB, (cb + 1) * FFN_COL_SUB))

    def project(idx):
        rows, cols = window(idx)
        slot = idx % 2
        xr = x_ref[rows, :]
        us_ref[slot] = jnp.dot(xr, wub_ref[:, cols], preferred_element_type=F32)
        gs_ref[slot, 8:] = jnp.dot(xr, wgb_ref[:, cols], preferred_element_type=F32)
        if subs[idx][1] == 0:
            gs_ref[slot, :8] = carry_ref[:, cols]
        else:
            gs_ref[slot, :8] = gs_ref[1 - slot, FFN_ROW_SUB:]
        if subs[idx][1] == n_rb - 1:
            carry_ref[:, cols] = gs_ref[slot, FFN_ROW_SUB:]

    def epilogue(idx):
        rows, cols = window(idx)
        slot = idx % 2
        g = gs_ref[slot, 8:]
        g1 = gs_ref[slot, 7:7 + FFN_ROW_SUB]
        g2 = gs_ref[slot, 6:6 + FFN_ROW_SUB]
        cw = cw_ref[:, cols]
        conv = cw[0:1] * g2 + cw[1:2] * g1 + cw[2:3] * g + cb_ref[:, cols]
        act = 0.5 * conv * (1.0 + lax.erf(conv * (2.0 ** -0.5)))
        o_ref[rows, cols] = (act * us_ref[slot]).astype(o_ref.dtype)

    project(0)
    for idx in range(1, len(subs)):
        project(idx)
        epilogue(idx - 1)
    epilogue(len(subs) - 1)


def _ffn_up(xn, w_up, layer, conv_w, conv_b, s_len, tm=1024, tn=512):
    m, k = xn.shape
    nn = D_FF // tn
    assert s_len % tm == 0 and tm % FFN_ROW_SUB == 0 and tn % FFN_COL_SUB == 0
    return pl.pallas_call(
        functools.partial(_ffn_up_kernel, tm=tm, tn=tn, s_len=s_len),
        grid=(nn, m // tm),
        in_specs=[
            pl.BlockSpec((tm, k), lambda j, i: (i, 0)),
            pl.BlockSpec((None, k, tn), lambda j, i: (layer, 0, j)),
            pl.BlockSpec((None, k, tn), lambda j, i: (layer, 0, nn + j)),
            pl.BlockSpec((3, tn), lambda j, i: (0, j)),
            pl.BlockSpec((1, tn), lambda j, i: (0, j)),
        ],
        out_specs=pl.BlockSpec((tm, tn), lambda j, i: (i, j)),
        out_shape=jax.ShapeDtypeStruct((m, D_FF), BF16),
        scratch_shapes=[pltpu.VMEM((k, tn), BF16), pltpu.VMEM((k, tn), BF16),
                        pltpu.VMEM((2, FFN_ROW_SUB, FFN_COL_SUB), F32),
                        pltpu.VMEM((2, 8 + FFN_ROW_SUB, FFN_COL_SUB), F32),
                        pltpu.VMEM((8, tn), F32)],
        compiler_params=pltpu.CompilerParams(
            dimension_semantics=("arbitrary", "arbitrary")),
        name="ffn_up",
    )(xn, w_up, w_up, conv_w, conv_b.reshape(1, D_FF))


def _conv_glu(h, xn, w_up, layer, conv_w, conv_b, w_down, s_len):
    act = _ffn_up(xn, w_up, layer, conv_w, conv_b, s_len)
    return _matmul(act, w_down, layer, F32, residual=h, tn=512)


def kernel(x, attn_norm, gla_w_in, gla_w_a2, gla_b_a2, gla_head_norm, gla_w_out, kv_norm, w_kv,
           dsa_w_q, dsa_w_out, ffn_norm, ffn_w_up, ffn_conv_w, ffn_conv_b, ffn_w_down, final_norm):
    bsz, s_len, d = x.shape
    t = bsz * s_len
    h = x.reshape(t, d)

    xn = _rmsnorm(h, attn_norm[0], BF16)
    n_main = 2 * GLA_KEY_DIM + 2 * GLA_VAL_DIM
    w_in_t = jnp.swapaxes(gla_w_in, 1, 2)
    proj = _matmul(xn, w_in_t, 0, BF16, n_cols=n_main, w_is_nk=True)
    w_a1 = jnp.pad(w_in_t[0, n_main:].T.astype(BF16), ((0, 0), (0, LANES - GATE_RANK)))
    w_a2 = jnp.pad(gla_w_a2[0].astype(BF16), ((0, LANES - GATE_RANK), (0, 0)))
    la = _gla_gate(xn, w_a1, w_a2, gla_b_a2[0])
    o = _gla_recurrence(proj, la, gla_head_norm[0], bsz, s_len)
    h, xn = _matmul_res_norm(o, gla_w_out, 0, h, ffn_norm[0])
    h = _conv_glu(h, xn, ffn_w_up, 0, ffn_conv_w[0], ffn_conv_b[0], ffn_w_down, s_len)

    perm_np = _perm_matrix()
    perm = jnp.asarray(perm_np, BF16)
    perm_t = jnp.asarray(perm_np.T, BF16)
    xkv, xq, xq_perm = _rmsnorm2_perm(h, kv_norm, attn_norm[1], perm)
    kv, kv_perm = _matmul_and_permuted(xkv, w_kv[None], 0, perm)

    q0 = _matmul(xq, dsa_w_q, 0, BF16, n_cols=ATT_WIDTH)
    q12 = _matmul(xq_perm, dsa_w_q, 0, BF16, n_cols=2 * ATT_WIDTH, col_start=ATT_WIDTH)
    o0, lse0 = _dsa_branch(q0, 0, kv, 0, bsz, s_len)
    o1, lse1 = _dsa_branch(q12, 0, kv_perm, 1, bsz, s_len)
    o2, lse2 = _dsa_branch(q12, 1, kv_perm, 2, bsz, s_len)
    o = _dsa_merge([o0, o1, o2], [lse0, lse1, lse2], perm_t)
    h, xn = _matmul_res_norm(o, dsa_w_out, 0, h, ffn_norm[1])
    h = _conv_glu(h, xn, ffn_w_up, 1, ffn_conv_w[1], ffn_conv_b[1], ffn_w_down, s_len)

    return _rmsnorm(h, final_norm, F32).reshape(bsz, s_len, d)
```

```python
import functools
import math

import numpy as np
import jax
import jax.numpy as jnp
from jax import lax
from jax.experimental import pallas as pl
from jax.experimental.pallas import tpu as pltpu

D_MODEL = 2048
GLA_HEADS = 4
GLA_KEY_DIM = 1024
GLA_VAL_DIM = 2048
GLA_DK = 256
GLA_DV = 512
GATE_RANK = 16
GATE_NORMALIZER = 16.0
GLA_CHUNK = 64
GLA_BLOCK = 256
ATT_HEADS = 16
HEAD_DIM = 128
ATT_WIDTH = ATT_HEADS * HEAD_DIM
WINDOWS = (128, 512, 2048)
DILATIONS = (1, 4, 16)
ATT_BLOCK = 128
QK_LOOKAHEAD = 2
PERM_TILE = 512
PERM_RUN = PERM_TILE // 16
D_FF = 5632
EPS = 1e-6
LOG2_E = math.log2(math.e)
LN_2 = math.log(2.0)
LANES = 128
FFN_ROW_SUB = 128
FFN_COL_SUB = 256
FFN_SLOTS = 2

F32 = jnp.float32
BF16 = jnp.bfloat16


def _alibi_slopes(n):
    def pow2_slopes(m):
        start = 2.0 ** (-8.0 / m)
        return [start ** (i + 1) for i in range(m)]
    assert math.log2(n).is_integer()
    return [float(v) for v in np.array(pow2_slopes(n), dtype=np.float32)]


def _rmsnorm_kernel(x_ref, g_ref, o_ref):
    x = x_ref[...]
    ms = jnp.mean(x * x, axis=-1, keepdims=True)
    o_ref[...] = ((x * lax.rsqrt(ms + EPS)) * g_ref[...]).astype(o_ref.dtype)


def _rmsnorm(x, g, out_dtype, tm=512):
    m, d = x.shape
    return pl.pallas_call(
        _rmsnorm_kernel,
        grid=(m // tm,),
        in_specs=[pl.BlockSpec((tm, d), lambda i: (i, 0)),
                  pl.BlockSpec((1, d), lambda i: (0, 0))],
        out_specs=pl.BlockSpec((tm, d), lambda i: (i, 0)),
        out_shape=jax.ShapeDtypeStruct((m, d), out_dtype),
        compiler_params=pltpu.CompilerParams(dimension_semantics=("arbitrary",)),
        name="rmsnorm",
    )(x, g.reshape(1, d))


def _cast_weights_once(w_ref, wb_ref):
    @pl.when(pl.program_id(1) == 0)
    def _():
        wb_ref[...] = w_ref[...].astype(BF16)


def _mm_kernel(x_ref, w_ref, *rest, out_scale):
    o_ref, wb_ref = rest[-2:]
    _cast_weights_once(w_ref, wb_ref)
    acc = jnp.dot(x_ref[...], wb_ref[...], preferred_element_type=F32)
    if len(rest) == 3:
        acc = rest[0][...] + acc
    if out_scale is not None:
        acc = acc * out_scale
    o_ref[...] = acc.astype(o_ref.dtype)


def _matmul(x, w, layer, out_dtype, n_cols=None, col_start=0, residual=None, out_scale=None,
            tm=512, tn=1024):
    m, k = x.shape
    n = w.shape[2] if n_cols is None else n_cols
    assert w.shape[1] == k and m % tm == 0 and n % tn == 0 and col_start % tn == 0
    j0 = col_start // tn
    in_specs = [pl.BlockSpec((tm, k), lambda j, i: (i, 0)),
                pl.BlockSpec((None, k, tn), lambda j, i: (layer, 0, j0 + j))]
    args = [x, w]
    if residual is not None:
        in_specs.append(pl.BlockSpec((tm, tn), lambda j, i: (i, j)))
        args.append(residual)
    return pl.pallas_call(
        functools.partial(_mm_kernel, out_scale=out_scale),
        grid=(n // tn, m // tm),
        in_specs=in_specs,
        out_specs=pl.BlockSpec((tm, tn), lambda j, i: (i, j)),
        out_shape=jax.ShapeDtypeStruct((m, n), out_dtype),
        scratch_shapes=[pltpu.VMEM((k, tn), BF16)],
        compiler_params=pltpu.CompilerParams(
            dimension_semantics=("arbitrary", "arbitrary")),
        name="matmul",
    )(*args)


def _mm_perm_kernel(x_ref, w_ref, p_ref, o_ref, op_ref, wb_ref):
    _cast_weights_once(w_ref, wb_ref)
    out = jnp.dot(x_ref[...], wb_ref[...], preferred_element_type=F32).astype(BF16)
    o_ref[...] = out
    op_ref[...] = jnp.dot(p_ref[...], out, preferred_element_type=F32).astype(BF16)


def _matmul_and_permuted(x, w, layer, perm, tn=1024):
    m, k = x.shape
    n = w.shape[2]
    tm = PERM_TILE
    assert w.shape[1] == k and m % tm == 0 and n % tn == 0
    out_spec = pl.BlockSpec((tm, tn), lambda j, i: (i, j))
    out = jax.ShapeDtypeStruct((m, n), BF16)
    return pl.pallas_call(
        _mm_perm_kernel,
        grid=(n // tn, m // tm),
        in_specs=[pl.BlockSpec((tm, k), lambda j, i: (i, 0)),
                  pl.BlockSpec((None, k, tn), lambda j, i: (layer, 0, j)),
                  pl.BlockSpec((tm, tm), lambda j, i: (0, 0))],
        out_specs=[out_spec, out_spec],
        out_shape=[out, out],
        scratch_shapes=[pltpu.VMEM((k, tn), BF16)],
        compiler_params=pltpu.CompilerParams(
            dimension_semantics=("arbitrary", "arbitrary")),
        name="matmul_and_permuted",
    )(x, w, perm)


def _mm_res_norm_kernel(x_ref, w_ref, r_ref, g_ref, h_ref, xn_ref, wb_ref):
    @pl.when(pl.program_id(0) == 0)
    def _():
        wb_ref[...] = w_ref[...].astype(BF16)

    h = r_ref[...] + jnp.dot(x_ref[...], wb_ref[...], preferred_element_type=F32)
    h_ref[...] = h
    ms = jnp.mean(h * h, axis=-1, keepdims=True)
    xn_ref[...] = ((h * lax.rsqrt(ms + EPS)) * g_ref[...]).astype(xn_ref.dtype)


def _matmul_res_norm(x, w, layer, residual, norm_g, tm=512):
    m, k = x.shape
    n = w.shape[2]
    assert w.shape[1] == k and residual.shape == (m, n) and m % tm == 0
    row = pl.BlockSpec((tm, n), lambda i: (i, 0))
    return pl.pallas_call(
        _mm_res_norm_kernel,
        grid=(m // tm,),
        in_specs=[pl.BlockSpec((tm, k), lambda i: (i, 0)),
                  pl.BlockSpec((None, k, n), lambda i: (layer, 0, 0),
                               pipeline_mode=pl.Buffered(1)),
                  row,
                  pl.BlockSpec((1, n), lambda i: (0, 0))],
        out_specs=[row, row],
        out_shape=[jax.ShapeDtypeStruct((m, n), F32), jax.ShapeDtypeStruct((m, n), BF16)],
        scratch_shapes=[pltpu.VMEM((k, n), BF16)],
        compiler_params=pltpu.CompilerParams(dimension_semantics=("arbitrary",)),
        name="matmul_res_norm",
    )(x, w, residual, norm_g.reshape(1, n))


def _norm2_perm_kernel(x_ref, ga_ref, gb_ref, p_ref, a_ref, b_ref, bp_ref):
    x = x_ref[...]
    ms = jnp.mean(x * x, axis=-1, keepdims=True)
    y = x * lax.rsqrt(ms + EPS)
    a_ref[...] = (y * ga_ref[...]).astype(BF16)
    b = (y * gb_ref[...]).astype(BF16)
    b_ref[...] = b
    bp_ref[...] = jnp.dot(p_ref[...], b, preferred_element_type=F32).astype(BF16)


def _rmsnorm2_perm(x, g_a, g_b, perm):
    m, d = x.shape
    tm = PERM_TILE
    row = pl.BlockSpec((tm, d), lambda i: (i, 0))
    vec = pl.BlockSpec((1, d), lambda i: (0, 0))
    out = jax.ShapeDtypeStruct((m, d), BF16)
    return pl.pallas_call(
        _norm2_perm_kernel,
        grid=(m // tm,),
        in_specs=[row, vec, vec, pl.BlockSpec((tm, tm), lambda i: (0, 0))],
        out_specs=[row, row, row],
        out_shape=[out, out, out],
        compiler_params=pltpu.CompilerParams(dimension_semantics=("arbitrary",)),
        name="rmsnorm2_perm",
    )(x, g_a.reshape(1, d), g_b.reshape(1, d), perm)


def _gate_kernel(x_ref, w1_ref, w2_ref, b_ref, o_ref):
    a = jnp.dot(x_ref[...], w1_ref[...], preferred_element_type=F32)
    z = jnp.dot(a.astype(BF16), w2_ref[...], preferred_element_type=F32) + b_ref[...]
    log_sig = jnp.minimum(z, 0.0) - jnp.log1p(jnp.exp(-jnp.abs(z)))
    o_ref[...] = log_sig / GATE_NORMALIZER


def _gla_gate(xn, w_a1, w_a2, b_a2, tm=512):
    m, k = xn.shape
    return pl.pallas_call(
        _gate_kernel,
        grid=(m // tm,),
        in_specs=[pl.BlockSpec((tm, k), lambda i: (i, 0)),
                  pl.BlockSpec((k, LANES), lambda i: (0, 0)),
                  pl.BlockSpec((LANES, GLA_KEY_DIM), lambda i: (0, 0)),
                  pl.BlockSpec((1, GLA_KEY_DIM), lambda i: (0, 0))],
        out_specs=pl.BlockSpec((tm, GLA_KEY_DIM), lambda i: (i, 0)),
        out_shape=jax.ShapeDtypeStruct((m, GLA_KEY_DIM), F32),
        compiler_params=pltpu.CompilerParams(dimension_semantics=("arbitrary",)),
        name="gla_gate",
    )(xn, w_a1, w_a2, b_a2.reshape(1, GLA_KEY_DIM))


def _split3(x):
    hi = x.astype(BF16)
    r1 = x - hi.astype(F32)
    mid = r1.astype(BF16)
    lo = (r1 - mid.astype(F32)).astype(BF16)
    return hi, mid, lo


def _dot01(mat01, x):
    hi, mid, lo = _split3(x)
    return (jnp.dot(mat01, hi, preferred_element_type=F32)
            + jnp.dot(mat01, mid, preferred_element_type=F32)
            + jnp.dot(mat01, lo, preferred_element_type=F32))


def _gla_kernel(q_ref, k_ref, v_ref, r_ref, la_ref, hn_ref, o_ref, st_ref):
    @pl.when(pl.program_id(2) == 0)
    def _():
        st_ref[...] = jnp.zeros_like(st_ref)

    nb = GLA_BLOCK
    row = lax.broadcasted_iota(jnp.int32, (nb, nb), 0)
    col = lax.broadcasted_iota(jnp.int32, (nb, nb), 1)
    same_chunk = (row // GLA_CHUNK) == (col // GLA_CHUNK)
    causal = same_chunk & (col <= row)
    tri = jnp.where(causal, 1.0, 0.0).astype(BF16)

    cum = _dot01(tri, la_ref[...])

    q = q_ref[...].astype(F32) * (GLA_DK ** -0.5)
    k = k_ref[...].astype(F32)
    v = v_ref[...]
    q_dec = (q * jnp.exp(cum)).astype(BF16)
    k_inv = (k * jnp.exp(-cum)).astype(BF16)

    scores = lax.dot_general(q_dec, k_inv, (((1,), (1,)), ((), ())),
                             preferred_element_type=F32)
    scores = jnp.where(causal, scores, 0.0).astype(BF16)
    o_intra = jnp.dot(scores, v, preferred_element_type=F32)

    o_parts = []
    for c in range(nb // GLA_CHUNK):
        lo_r, hi_r = c * GLA_CHUNK, (c + 1) * GLA_CHUNK
        cum_c = cum[lo_r:hi_r]
        last_c = cum[hi_r - 1:hi_r]
        k_end = (k[lo_r:hi_r] * jnp.exp(last_c - cum_c)).astype(BF16)
        st = st_ref[...]
        o_inter = lax.dot_general(q_dec[lo_r:hi_r], st.astype(BF16),
                                  (((1,), (1,)), ((), ())),
                                  preferred_element_type=F32)
        upd = lax.dot_general(v[lo_r:hi_r], k_end, (((0,), (0,)), ((), ())),
                              preferred_element_type=F32)
        st_ref[...] = st * jnp.exp(last_c) + upd
        o_parts.append(o_intra[lo_r:hi_r] + o_inter)
    o = jnp.concatenate(o_parts, axis=0)

    ms = jnp.mean(o * o, axis=-1, keepdims=True)
    o = (o * lax.rsqrt(ms + EPS)) * hn_ref[...]
    r = r_ref[...].astype(F32)
    gate = r * (1.0 / (1.0 + jnp.exp(-r)))
    o_ref[...] = (o * gate).astype(o_ref.dtype)


def _gla_recurrence(proj, la, head_norm, bsz, s_len):
    t = bsz * s_len
    nblk = s_len // GLA_BLOCK
    kq = GLA_KEY_DIM // GLA_DK
    row = lambda b, h, i: b * nblk + i
    return pl.pallas_call(
        _gla_kernel,
        grid=(bsz, GLA_HEADS, nblk),
        in_specs=[
            pl.BlockSpec((GLA_BLOCK, GLA_DK), lambda b, h, i: (row(b, h, i), h)),
            pl.BlockSpec((GLA_BLOCK, GLA_DK), lambda b, h, i: (row(b, h, i), kq + h)),
            pl.BlockSpec((GLA_BLOCK, GLA_DV), lambda b, h, i: (row(b, h, i), kq + h)),
            pl.BlockSpec((GLA_BLOCK, GLA_DV), lambda b, h, i: (row(b, h, i), 2 * kq + h)),
            pl.BlockSpec((GLA_BLOCK, GLA_DK), lambda b, h, i: (row(b, h, i), h)),
            pl.BlockSpec((1, GLA_DV), lambda b, h, i: (0, 0)),
        ],
        out_specs=pl.BlockSpec((GLA_BLOCK, GLA_DV), lambda b, h, i: (row(b, h, i), h)),
        out_shape=jax.ShapeDtypeStruct((t, GLA_VAL_DIM), BF16),
        scratch_shapes=[pltpu.VMEM((GLA_DV, GLA_DK), F32)],
        compiler_params=pltpu.CompilerParams(
            dimension_semantics=("arbitrary", "arbitrary", "arbitrary")),
        name="gla_recurrence",
    )(proj, proj, proj, proj, la, head_norm.reshape(1, GLA_DV))


def _perm_matrix():
    p = np.zeros((PERM_TILE, PERM_TILE), np.float32)
    for i in range(PERM_RUN):
        for a in range(4):
            for r4 in range(4):
                p[(4 * r4 + a) * PERM_RUN + i, 16 * i + 4 * a + r4] = 1.0
    return p


def _dsa_kernel(q_ref, kc_ref, vc_ref, o_ref, lse_ref, kp_ref, vp_ref, *, dil, interleaved,
                slopes):
    n = pl.program_id(2)

    @pl.when(n == 0)
    def _():
        kp_ref[...] = jnp.zeros_like(kp_ref)
        vp_ref[...] = jnp.zeros_like(vp_ref)

    nk = 2 * ATT_BLOCK
    rq = lax.broadcasted_iota(jnp.int32, (ATT_BLOCK, nk), 0)
    ck = lax.broadcasted_iota(jnp.int32, (ATT_BLOCK, nk), 1)
    ck_in = ck & (ATT_BLOCK - 1)
    if interleaved:
        sub_q = 4 * (rq & (PERM_RUN - 1)) + (rq >> 5)
        sub_k = 4 * (ck_in & (PERM_RUN - 1)) + (ck_in >> 5)
    else:
        sub_q, sub_k = rq, ck_in
    is_cur = ck >= ATT_BLOCK
    j = sub_q - sub_k + jnp.where(is_cur, 0, ATT_BLOCK)
    valid = (j >= 0) & (j <= ATT_BLOCK) & (is_cur | (n > 0))
    neg_dist = jnp.where(valid, -(j * dil).astype(F32), -jnp.inf)

    lane = lax.broadcasted_iota(jnp.int32, (ATT_BLOCK, LANES), 1)
    ones = jnp.ones((nk, HEAD_DIM), BF16)
    nt = (((1,), (1,)), ((), ()))

    def head_slice(ref, h):
        x = ref[..., h * HEAD_DIM:(h + 1) * HEAD_DIM]
        return x.reshape(ATT_BLOCK, HEAD_DIM)

    def scores(h):
        kcat = jnp.concatenate([head_slice(kp_ref, h), head_slice(kc_ref, h)], axis=0)
        return lax.dot_general(head_slice(q_ref, h), kcat, nt, preferred_element_type=F32)

    pending = [scores(h) for h in range(QK_LOOKAHEAD)]
    lse_tile = jnp.zeros((ATT_BLOCK, LANES), F32)
    for h in range(ATT_HEADS):
        s = pending.pop(0)
        if h + QK_LOOKAHEAD < ATT_HEADS:
            pending.append(scores(h + QK_LOOKAHEAD))
        z = s + (slopes[h] * LOG2_E) * neg_dist
        m = jnp.max(z, axis=-1, keepdims=True)
        p = jnp.exp2(z - m).astype(BF16)
        vcat = jnp.concatenate([head_slice(vp_ref, h), head_slice(vc_ref, h)], axis=0)
        acc = jnp.dot(p, jnp.concatenate([vcat, ones], axis=1), preferred_element_type=F32)
        l = acc[:, HEAD_DIM:]
        o = (acc[:, :HEAD_DIM] / l).astype(o_ref.dtype)
        o_ref[..., h * HEAD_DIM:(h + 1) * HEAD_DIM] = o.reshape(o_ref.shape[:-1] + (HEAD_DIM,))
        lse_tile = jnp.where(lane == h, m * LN_2 + jnp.log(l), lse_tile)
    lse_ref[...] = lse_tile.reshape(lse_ref.shape)
    kp_ref[...] = kc_ref[...].reshape(kp_ref.shape)
    vp_ref[...] = vc_ref[...].reshape(vp_ref.shape)


def _dsa_branch(q, q_col, kv, g, bsz, s_len):
    d = DILATIONS[g]
    assert WINDOWS[g] // d == ATT_BLOCK
    t = bsz * s_len
    ntile = s_len // PERM_TILE
    if g == 0:
        view = lambda x: x.reshape(bsz, s_len, x.shape[-1])
        blk = lambda w: (None, ATT_BLOCK, w)
        grid = (bsz, 1, s_len // ATT_BLOCK)
        at = lambda c: (lambda b, r, n: (b, n, c))
    elif g == 1:
        view = lambda x: x.reshape(bsz, ntile, 4, 4, PERM_RUN, x.shape[-1])
        blk = lambda w: (None, None, None, 4, PERM_RUN, w)
        grid = (bsz, 4, ntile)
        at = lambda c: (lambda b, r, n: (b, n, r, 0, 0, c))
    else:
        view = lambda x: x.reshape(bsz, ntile, 16, PERM_RUN, x.shape[-1])
        blk = lambda w: (None, 4, None, PERM_RUN, w)
        grid = (bsz, 16, ntile // 4)
        at = lambda c: (lambda b, r, n: (b, n, r, 0, c))
    qv, kvv = view(q), view(kv)
    o, lse = pl.pallas_call(
        functools.partial(_dsa_kernel, dil=d, interleaved=(g == 1),
                          slopes=_alibi_slopes(ATT_HEADS)),
        grid=grid,
        in_specs=[
            pl.BlockSpec(blk(ATT_WIDTH), at(q_col)),
            pl.BlockSpec(blk(ATT_WIDTH), at(0)),
            pl.BlockSpec(blk(ATT_WIDTH), at(1)),
        ],
        out_specs=[pl.BlockSpec(blk(ATT_WIDTH), at(0)),
                   pl.BlockSpec(blk(LANES), at(0))],
        out_shape=[jax.ShapeDtypeStruct(qv.shape[:-1] + (ATT_WIDTH,), BF16),
                   jax.ShapeDtypeStruct(qv.shape[:-1] + (LANES,), F32)],
        scratch_shapes=[pltpu.VMEM((ATT_BLOCK, ATT_WIDTH), BF16),
                        pltpu.VMEM((ATT_BLOCK, ATT_WIDTH), BF16)],
        compiler_params=pltpu.CompilerParams(
            dimension_semantics=("arbitrary", "arbitrary", "arbitrary")),
        name=f"dsa_branch{g}",
    )(qv, kvv, kvv)
    return o.reshape(t, ATT_WIDTH), lse.reshape(t, LANES)


def _merge_kernel(pt_ref, o0_ref, o1_ref, o2_ref, l0_ref, l1_ref, l2_ref, o_ref):
    pt = pt_ref[...]
    l0 = l0_ref[...]
    l1 = _dot01(pt, l1_ref[...])
    l2 = _dot01(pt, l2_ref[...])
    m = jnp.maximum(jnp.maximum(l0, l1), l2)
    e0, e1, e2 = jnp.exp(l0 - m), jnp.exp(l1 - m), jnp.exp(l2 - m)
    den = e0 + e1 + e2
    w0, w1, w2 = e0 / den, e1 / den, e2 / den
    for hp in range(ATT_HEADS // 2):
        pair = slice(2 * hp * HEAD_DIM, (2 * hp + 2) * HEAD_DIM)
        o1 = jnp.dot(pt, o1_ref[:, pair], preferred_element_type=F32)
        o2 = jnp.dot(pt, o2_ref[:, pair], preferred_element_type=F32)
        for k in range(2):
            h = 2 * hp + k
            sl = slice(h * HEAD_DIM, (h + 1) * HEAD_DIM)
            in_pair = slice(k * HEAD_DIM, (k + 1) * HEAD_DIM)
            o = (w0[:, h:h + 1] * o0_ref[:, sl].astype(F32)
                 + w1[:, h:h + 1] * o1[:, in_pair] + w2[:, h:h + 1] * o2[:, in_pair])
            o_ref[:, sl] = o.astype(o_ref.dtype)


def _dsa_merge(outs, lses, perm_t):
    t, hd = outs[0].shape
    tm = PERM_TILE
    o_spec = pl.BlockSpec((tm, hd), lambda i: (i, 0))
    l_spec = pl.BlockSpec((tm, LANES), lambda i: (i, 0))
    return pl.pallas_call(
        _merge_kernel,
        grid=(t // tm,),
        in_specs=[pl.BlockSpec((tm, tm), lambda i: (0, 0)),
                  o_spec, o_spec, o_spec, l_spec, l_spec, l_spec],
        out_specs=o_spec,
        out_shape=jax.ShapeDtypeStruct((t, hd), BF16),
        compiler_params=pltpu.CompilerParams(dimension_semantics=("arbitrary",)),
        name="dsa_merge",
    )(perm_t, *outs, *lses)


def _ffn_up_kernel(x_ref, wu_ref, wg_ref, cw_ref, cb_ref, o_ref, wub_ref, wgb_ref,
                   us_ref, gs_ref, carry_ref, *, tm, tn, s_len):
    @pl.when(pl.program_id(1) == 0)
    def _():
        wub_ref[...] = wu_ref[...].astype(BF16)
        wgb_ref[...] = wg_ref[...].astype(BF16)

    @pl.when((pl.program_id(1) * tm) % s_len == 0)
    def _():
        carry_ref[...] = jnp.zeros_like(carry_ref)

    n_rb = tm // FFN_ROW_SUB
    subs = [(cb, rb) for cb in range(tn // FFN_COL_SUB) for rb in range(n_rb)]

    def window(idx):
        cb, rb = subs[idx]
        return (slice(rb * FFN_ROW_SUB, (rb + 1) * FFN_ROW_SUB),
                slice(cb * FFN_COL_SUB, (cb + 1) * FFN_COL_SUB))

    def project(idx):
        rows, cols = window(idx)
        slot = idx % FFN_SLOTS
        xr = x_ref[rows, :]
        us_ref[slot] = jnp.dot(xr, wub_ref[:, cols], preferred_element_type=F32)
        gs_ref[slot, 8:] = jnp.dot(xr, wgb_ref[:, cols], preferred_element_type=F32)
        if subs[idx][1] == 0:
            gs_ref[slot, :8] = carry_ref[:, cols]
        else:
            gs_ref[slot, :8] = gs_ref[(idx - 1) % FFN_SLOTS, FFN_ROW_SUB:]
        if subs[idx][1] == n_rb - 1:
            carry_ref[:, cols] = gs_ref[slot, FFN_ROW_SUB:]

    def epilogue(idx):
        rows, cols = window(idx)
        slot = idx % FFN_SLOTS
        g = gs_ref[slot, 8:]
        g1 = gs_ref[slot, 7:7 + FFN_ROW_SUB]
        g2 = gs_ref[slot, 6:6 + FFN_ROW_SUB]
        cw = cw_ref[:, cols]
        conv = cw[0:1] * g2 + cw[1:2] * g1 + cw[2:3] * g + cb_ref[:, cols]
        act = 0.5 * conv * (1.0 + lax.erf(conv * (2.0 ** -0.5)))
        o_ref[rows, cols] = (act * us_ref[slot]).astype(o_ref.dtype)

    ahead = FFN_SLOTS - 1
    for idx in range(ahead):
        project(idx)
    for idx in range(len(subs)):
        if idx + ahead < len(subs):
            project(idx + ahead)
        epilogue(idx)


def _ffn_up(xn, w_up, layer, conv_w, conv_b, s_len, tm=1024, tn=512):
    m, k = xn.shape
    nn = D_FF // tn
    assert s_len % tm == 0 and tm % FFN_ROW_SUB == 0 and tn % FFN_COL_SUB == 0
    return pl.pallas_call(
        functools.partial(_ffn_up_kernel, tm=tm, tn=tn, s_len=s_len),
        grid=(nn, m // tm),
        in_specs=[
            pl.BlockSpec((tm, k), lambda j, i: (i, 0)),
            pl.BlockSpec((None, k, tn), lambda j, i: (layer, 0, j)),
            pl.BlockSpec((None, k, tn), lambda j, i: (layer, 0, nn + j)),
            pl.BlockSpec((3, tn), lambda j, i: (0, j)),
            pl.BlockSpec((1, tn), lambda j, i: (0, j)),
        ],
        out_specs=pl.BlockSpec((tm, tn), lambda j, i: (i, j)),
        out_shape=jax.ShapeDtypeStruct((m, D_FF), BF16),
        scratch_shapes=[pltpu.VMEM((k, tn), BF16), pltpu.VMEM((k, tn), BF16),
                        pltpu.VMEM((FFN_SLOTS, FFN_ROW_SUB, FFN_COL_SUB), F32),
                        pltpu.VMEM((FFN_SLOTS, 8 + FFN_ROW_SUB, FFN_COL_SUB), F32),
                        pltpu.VMEM((8, tn), F32)],
        compiler_params=pltpu.CompilerParams(
            dimension_semantics=("arbitrary", "arbitrary")),
        name="ffn_up",
    )(xn, w_up, w_up, conv_w, conv_b.reshape(1, D_FF))


def _conv_glu(h, xn, w_up, layer, conv_w, conv_b, w_down, s_len):
    act = _ffn_up(xn, w_up, layer, conv_w, conv_b, s_len)
    return _matmul(act, w_down, layer, F32, residual=h, tn=512)


def kernel(x, attn_norm, gla_w_in, gla_w_a2, gla_b_a2, gla_head_norm, gla_w_out, kv_norm, w_kv,
           dsa_w_q, dsa_w_out, ffn_norm, ffn_w_up, ffn_conv_w, ffn_conv_b, ffn_w_down, final_norm):
    bsz, s_len, d = x.shape
    t = bsz * s_len
    h = x.reshape(t, d)

    xn = _rmsnorm(h, attn_norm[0], BF16)
    n_main = 2 * GLA_KEY_DIM + 2 * GLA_VAL_DIM
    proj = _matmul(xn, gla_w_in, 0, BF16, n_cols=n_main)
    w_a1 = jnp.pad(gla_w_in[0, :, n_main:].astype(BF16), ((0, 0), (0, LANES - GATE_RANK)))
    w_a2 = jnp.pad(gla_w_a2[0].astype(BF16), ((0, LANES - GATE_RANK), (0, 0)))
    la = _gla_gate(xn, w_a1, w_a2, gla_b_a2[0])
    o = _gla_recurrence(proj, la, gla_head_norm[0], bsz, s_len)
    h, xn = _matmul_res_norm(o, gla_w_out, 0, h, ffn_norm[0])
    h = _conv_glu(h, xn, ffn_w_up, 0, ffn_conv_w[0], ffn_conv_b[0], ffn_w_down, s_len)

    perm_np = _perm_matrix()
    perm = jnp.asarray(perm_np, BF16)
    perm_t = jnp.asarray(perm_np.T, BF16)
    xkv, xq, xq_perm = _rmsnorm2_perm(h, kv_norm, attn_norm[1], perm)
    kv, kv_perm = _matmul_and_permuted(xkv, w_kv[None], 0, perm)

    q_scale = HEAD_DIM ** -0.5 * LOG2_E
    q0 = _matmul(xq, dsa_w_q, 0, BF16, n_cols=ATT_WIDTH, out_scale=q_scale)
    q12 = _matmul(xq_perm, dsa_w_q, 0, BF16, n_cols=2 * ATT_WIDTH, col_start=ATT_WIDTH,
                  out_scale=q_scale)
    o0, lse0 = _dsa_branch(q0, 0, kv, 0, bsz, s_len)
    o1, lse1 = _dsa_branch(q12, 0, kv_perm, 1, bsz, s_len)
    o2, lse2 = _dsa_branch(q12, 1, kv_perm, 2, bsz, s_len)
    o = _dsa_merge([o0, o1, o2], [lse0, lse1, lse2], perm_t)
    h, xn = _matmul_res_norm(o, dsa_w_out, 0, h, ffn_norm[1])
    h = _conv_glu(h, xn, ffn_w_up, 1, ffn_conv_w[1], ffn_conv_b[1], ffn_w_down, s_len)

    return _rmsnorm(h, final_norm, F32).reshape(bsz, s_len, d)
```

```python
import functools
import math

import numpy as np
import jax
import jax.numpy as jnp
from jax import lax
from jax.experimental import pallas as pl
from jax.experimental.pallas import tpu as pltpu

D_MODEL = 2048
GLA_HEADS = 4
GLA_KEY_DIM = 1024
GLA_VAL_DIM = 2048
GLA_DK = 256
GLA_DV = 512
GATE_RANK = 16
GATE_NORMALIZER = 16.0
GLA_CHUNK = 64
GLA_BLOCK = 256
GLA_HEADS_PER_STEP = 4
ATT_HEADS = 16
HEAD_DIM = 128
ATT_WIDTH = ATT_HEADS * HEAD_DIM
WINDOWS = (128, 512, 2048)
DILATIONS = (1, 4, 16)
ATT_BLOCK = 128
QK_LOOKAHEAD = 2
PERM_TILE = 512
PERM_RUN = PERM_TILE // 16
D_FF = 5632
EPS = 1e-6
LOG2_E = math.log2(math.e)
LN_2 = math.log(2.0)
LANES = 128
FFN_ROW_SUB = 512
FFN_COL_SUB = 256
FFN_SLOTS = 2

F32 = jnp.float32
BF16 = jnp.bfloat16


def _alibi_slopes(n):
    def pow2_slopes(m):
        start = 2.0 ** (-8.0 / m)
        return [start ** (i + 1) for i in range(m)]
    assert math.log2(n).is_integer()
    return [float(v) for v in np.array(pow2_slopes(n), dtype=np.float32)]


def _rmsnorm_kernel(x_ref, g_ref, o_ref):
    x = x_ref[...]
    ms = jnp.mean(x * x, axis=-1, keepdims=True)
    o_ref[...] = ((x * lax.rsqrt(ms + EPS)) * g_ref[...]).astype(o_ref.dtype)


def _rmsnorm(x, g, out_dtype, tm=512):
    m, d = x.shape
    return pl.pallas_call(
        _rmsnorm_kernel,
        grid=(m // tm,),
        in_specs=[pl.BlockSpec((tm, d), lambda i: (i, 0)),
                  pl.BlockSpec((1, d), lambda i: (0, 0))],
        out_specs=pl.BlockSpec((tm, d), lambda i: (i, 0)),
        out_shape=jax.ShapeDtypeStruct((m, d), out_dtype),
        compiler_params=pltpu.CompilerParams(dimension_semantics=("arbitrary",)),
        name="rmsnorm",
    )(x, g.reshape(1, d))


def _cast_weights_once(w_ref, wb_ref):
    @pl.when(pl.program_id(1) == 0)
    def _():
        wb_ref[...] = w_ref[...].astype(BF16)


def _mm_kernel(x_ref, w_ref, *rest, out_scale):
    o_ref, wb_ref = rest[-2:]
    _cast_weights_once(w_ref, wb_ref)
    acc = jnp.dot(x_ref[...], wb_ref[...], preferred_element_type=F32)
    if len(rest) == 3:
        acc = rest[0][...] + acc
    if out_scale is not None:
        acc = acc * out_scale
    o_ref[...] = acc.astype(o_ref.dtype)


def _matmul(x, w, layer, out_dtype, n_cols=None, col_start=0, residual=None, out_scale=None,
            tm=512, tn=1024):
    m, k = x.shape
    n = w.shape[2] if n_cols is None else n_cols
    assert w.shape[1] == k and m % tm == 0 and n % tn == 0 and col_start % tn == 0
    j0 = col_start // tn
    in_specs = [pl.BlockSpec((tm, k), lambda j, i: (i, 0)),
                pl.BlockSpec((None, k, tn), lambda j, i: (layer, 0, j0 + j))]
    args = [x, w]
    if residual is not None:
        in_specs.append(pl.BlockSpec((tm, tn), lambda j, i: (i, j)))
        args.append(residual)
    return pl.pallas_call(
        functools.partial(_mm_kernel, out_scale=out_scale),
        grid=(n // tn, m // tm),
        in_specs=in_specs,
        out_specs=pl.BlockSpec((tm, tn), lambda j, i: (i, j)),
        out_shape=jax.ShapeDtypeStruct((m, n), out_dtype),
        scratch_shapes=[pltpu.VMEM((k, tn), BF16)],
        compiler_params=pltpu.CompilerParams(
            dimension_semantics=("arbitrary", "arbitrary")),
        name="matmul",
    )(*args)


def _mm_perm_kernel(x_ref, w_ref, p_ref, o_ref, op_ref, wb_ref):
    _cast_weights_once(w_ref, wb_ref)
    out = jnp.dot(x_ref[...], wb_ref[...], preferred_element_type=F32).astype(BF16)
    o_ref[...] = out
    op_ref[...] = jnp.dot(p_ref[...], out, preferred_element_type=F32).astype(BF16)


def _matmul_and_permuted(x, w, layer, perm, tn=1024):
    m, k = x.shape
    n = w.shape[2]
    tm = PERM_TILE
    assert w.shape[1] == k and m % tm == 0 and n % tn == 0
    out_spec = pl.BlockSpec((tm, tn), lambda j, i: (i, j))
    out = jax.ShapeDtypeStruct((m, n), BF16)
    return pl.pallas_call(
        _mm_perm_kernel,
        grid=(n // tn, m // tm),
        in_specs=[pl.BlockSpec((tm, k), lambda j, i: (i, 0)),
                  pl.BlockSpec((None, k, tn), lambda j, i: (layer, 0, j)),
                  pl.BlockSpec((tm, tm), lambda j, i: (0, 0))],
        out_specs=[out_spec, out_spec],
        out_shape=[out, out],
        scratch_shapes=[pltpu.VMEM((k, tn), BF16)],
        compiler_params=pltpu.CompilerParams(
            dimension_semantics=("arbitrary", "arbitrary")),
        name="matmul_and_permuted",
    )(x, w, perm)


def _mm_res_norm_kernel(x_ref, w_ref, r_ref, g_ref, h_ref, xn_ref, wb_ref):
    @pl.when(pl.program_id(0) == 0)
    def _():
        wb_ref[...] = w_ref[...].astype(BF16)

    h = r_ref[...] + jnp.dot(x_ref[...], wb_ref[...], preferred_element_type=F32)
    h_ref[...] = h
    ms = jnp.mean(h * h, axis=-1, keepdims=True)
    xn_ref[...] = ((h * lax.rsqrt(ms + EPS)) * g_ref[...]).astype(xn_ref.dtype)


def _matmul_res_norm(x, w, layer, residual, norm_g, tm=512):
    m, k = x.shape
    n = w.shape[2]
    assert w.shape[1] == k and residual.shape == (m, n) and m % tm == 0
    row = pl.BlockSpec((tm, n), lambda i: (i, 0))
    return pl.pallas_call(
        _mm_res_norm_kernel,
        grid=(m // tm,),
        in_specs=[pl.BlockSpec((tm, k), lambda i: (i, 0)),
                  pl.BlockSpec((None, k, n), lambda i: (layer, 0, 0),
                               pipeline_mode=pl.Buffered(1)),
                  row,
                  pl.BlockSpec((1, n), lambda i: (0, 0))],
        out_specs=[row, row],
        out_shape=[jax.ShapeDtypeStruct((m, n), F32), jax.ShapeDtypeStruct((m, n), BF16)],
        scratch_shapes=[pltpu.VMEM((k, n), BF16)],
        compiler_params=pltpu.CompilerParams(dimension_semantics=("arbitrary",)),
        name="matmul_res_norm",
    )(x, w, residual, norm_g.reshape(1, n))


def _norm2_perm_kernel(x_ref, ga_ref, gb_ref, p_ref, a_ref, b_ref, bp_ref):
    x = x_ref[...]
    ms = jnp.mean(x * x, axis=-1, keepdims=True)
    y = x * lax.rsqrt(ms + EPS)
    a_ref[...] = (y * ga_ref[...]).astype(BF16)
    b = (y * gb_ref[...]).astype(BF16)
    b_ref[...] = b
    bp_ref[...] = jnp.dot(p_ref[...], b, preferred_element_type=F32).astype(BF16)


def _rmsnorm2_perm(x, g_a, g_b, perm):
    m, d = x.shape
    tm = PERM_TILE
    row = pl.BlockSpec((tm, d), lambda i: (i, 0))
    vec = pl.BlockSpec((1, d), lambda i: (0, 0))
    out = jax.ShapeDtypeStruct((m, d), BF16)
    return pl.pallas_call(
        _norm2_perm_kernel,
        grid=(m // tm,),
        in_specs=[row, vec, vec, pl.BlockSpec((tm, tm), lambda i: (0, 0))],
        out_specs=[row, row, row],
        out_shape=[out, out, out],
        compiler_params=pltpu.CompilerParams(dimension_semantics=("arbitrary",)),
        name="rmsnorm2_perm",
    )(x, g_a.reshape(1, d), g_b.reshape(1, d), perm)


def _gate_kernel(x_ref, w1_ref, w2_ref, b_ref, o_ref):
    a = jnp.dot(x_ref[...], w1_ref[...], preferred_element_type=F32)
    z = jnp.dot(a.astype(BF16), w2_ref[...], preferred_element_type=F32) + b_ref[...]
    log_sig = jnp.minimum(z, 0.0) - jnp.log1p(jnp.exp(-jnp.abs(z)))
    o_ref[...] = log_sig / GATE_NORMALIZER


def _gla_gate(xn, w_a1, w_a2, b_a2, tm=512):
    m, k = xn.shape
    return pl.pallas_call(
        _gate_kernel,
        grid=(m // tm,),
        in_specs=[pl.BlockSpec((tm, k), lambda i: (i, 0)),
                  pl.BlockSpec((k, LANES), lambda i: (0, 0)),
                  pl.BlockSpec((LANES, GLA_KEY_DIM), lambda i: (0, 0)),
                  pl.BlockSpec((1, GLA_KEY_DIM), lambda i: (0, 0))],
        out_specs=pl.BlockSpec((tm, GLA_KEY_DIM), lambda i: (i, 0)),
        out_shape=jax.ShapeDtypeStruct((m, GLA_KEY_DIM), F32),
        compiler_params=pltpu.CompilerParams(dimension_semantics=("arbitrary",)),
        name="gla_gate",
    )(xn, w_a1, w_a2, b_a2.reshape(1, GLA_KEY_DIM))


def _split3(x):
    hi = x.astype(BF16)
    r1 = x - hi.astype(F32)
    mid = r1.astype(BF16)
    lo = (r1 - mid.astype(F32)).astype(BF16)
    return hi, mid, lo


def _dot01(mat01, x):
    hi, mid, lo = _split3(x)
    return (jnp.dot(mat01, hi, preferred_element_type=F32)
            + jnp.dot(mat01, mid, preferred_element_type=F32)
            + jnp.dot(mat01, lo, preferred_element_type=F32))


def _gla_kernel(q_ref, k_ref, v_ref, r_ref, la_ref, hn_ref, o_ref, st_ref):
    @pl.when(pl.program_id(2) == 0)
    def _():
        st_ref[...] = jnp.zeros_like(st_ref)

    nb = GLA_BLOCK
    heads = range(GLA_HEADS_PER_STEP)
    dk = lambda h: slice(h * GLA_DK, (h + 1) * GLA_DK)
    dv = lambda h: slice(h * GLA_DV, (h + 1) * GLA_DV)
    row = lax.broadcasted_iota(jnp.int32, (nb, nb), 0)
    col = lax.broadcasted_iota(jnp.int32, (nb, nb), 1)
    same_chunk = (row // GLA_CHUNK) == (col // GLA_CHUNK)
    causal = same_chunk & (col <= row)
    tri = jnp.where(causal, 1.0, 0.0).astype(BF16)
    nt = (((1,), (1,)), ((), ()))
    tn = (((0,), (0,)), ((), ()))

    cum = [_dot01(tri, la_ref[:, dk(h)]) for h in heads]
    k = [k_ref[:, dk(h)].astype(F32) for h in heads]
    q_dec = [((q_ref[:, dk(h)].astype(F32) * (GLA_DK ** -0.5)) * jnp.exp(cum[h])).astype(BF16)
             for h in heads]
    k_inv = [(k[h] * jnp.exp(-cum[h])).astype(BF16) for h in heads]
    scores = [lax.dot_general(q_dec[h], k_inv[h], nt, preferred_element_type=F32)
              for h in heads]
    scores = [jnp.where(causal, s, 0.0).astype(BF16) for s in scores]
    o_intra = [jnp.dot(scores[h], v_ref[:, dv(h)], preferred_element_type=F32)
               for h in heads]

    o_parts = [[] for _ in heads]
    for c in range(nb // GLA_CHUNK):
        rows = slice(c * GLA_CHUNK, (c + 1) * GLA_CHUNK)
        for h in heads:
            last_c = cum[h][(c + 1) * GLA_CHUNK - 1:(c + 1) * GLA_CHUNK]
            k_end = (k[h][rows] * jnp.exp(last_c - cum[h][rows])).astype(BF16)
            st = st_ref[h]
            o_inter = lax.dot_general(q_dec[h][rows], st.astype(BF16), nt,
                                      preferred_element_type=F32)
            upd = lax.dot_general(v_ref[rows, dv(h)], k_end, tn,
                                  preferred_element_type=F32)
            st_ref[h] = st * jnp.exp(last_c) + upd
            o_parts[h].append(o_intra[h][rows] + o_inter)

    for h in heads:
        o = jnp.concatenate(o_parts[h], axis=0)
        ms = jnp.mean(o * o, axis=-1, keepdims=True)
        o = (o * lax.rsqrt(ms + EPS)) * hn_ref[...]
        r = r_ref[:, dv(h)].astype(F32)
        gate = r * (1.0 / (1.0 + jnp.exp(-r)))
        o_ref[:, dv(h)] = (o * gate).astype(o_ref.dtype)


def _gla_recurrence(proj, la, head_norm, bsz, s_len):
    t = bsz * s_len
    nblk = s_len // GLA_BLOCK
    hps = GLA_HEADS_PER_STEP
    wk, wv = hps * GLA_DK, hps * GLA_DV
    k_off = GLA_KEY_DIM // wk
    v_off = 2 * GLA_KEY_DIM // wv
    r_off = (2 * GLA_KEY_DIM + GLA_VAL_DIM) // wv
    row = lambda b, g, i: b * nblk + i
    return pl.pallas_call(
        _gla_kernel,
        grid=(bsz, GLA_HEADS // hps, nblk),
        in_specs=[
            pl.BlockSpec((GLA_BLOCK, wk), lambda b, g, i: (row(b, g, i), g)),
            pl.BlockSpec((GLA_BLOCK, wk), lambda b, g, i: (row(b, g, i), k_off + g)),
            pl.BlockSpec((GLA_BLOCK, wv), lambda b, g, i: (row(b, g, i), v_off + g)),
            pl.BlockSpec((GLA_BLOCK, wv), lambda b, g, i: (row(b, g, i), r_off + g)),
            pl.BlockSpec((GLA_BLOCK, wk), lambda b, g, i: (row(b, g, i), g)),
            pl.BlockSpec((1, GLA_DV), lambda b, g, i: (0, 0)),
        ],
        out_specs=pl.BlockSpec((GLA_BLOCK, wv), lambda b, g, i: (row(b, g, i), g)),
        out_shape=jax.ShapeDtypeStruct((t, GLA_VAL_DIM), BF16),
        scratch_shapes=[pltpu.VMEM((hps, GLA_DV, GLA_DK), F32)],
        compiler_params=pltpu.CompilerParams(
            dimension_semantics=("arbitrary", "arbitrary", "arbitrary")),
        name="gla_recurrence",
    )(proj, proj, proj, proj, la, head_norm.reshape(1, GLA_DV))


def _perm_matrix():
    p = np.zeros((PERM_TILE, PERM_TILE), np.float32)
    for i in range(PERM_RUN):
        for a in range(4):
            for r4 in range(4):
                p[(4 * r4 + a) * PERM_RUN + i, 16 * i + 4 * a + r4] = 1.0
    return p


def _dsa_kernel(q_ref, kc_ref, vc_ref, o_ref, lse_ref, kp_ref, vp_ref, *, dil, interleaved,
                slopes):
    n = pl.program_id(2)

    @pl.when(n == 0)
    def _():
        kp_ref[...] = jnp.zeros_like(kp_ref)
        vp_ref[...] = jnp.zeros_like(vp_ref)

    nk = 2 * ATT_BLOCK
    rq = lax.broadcasted_iota(jnp.int32, (ATT_BLOCK, nk), 0)
    ck = lax.broadcasted_iota(jnp.int32, (ATT_BLOCK, nk), 1)
    ck_in = ck & (ATT_BLOCK - 1)
    if interleaved:
        sub_q = 4 * (rq & (PERM_RUN - 1)) + (rq >> 5)
        sub_k = 4 * (ck_in & (PERM_RUN - 1)) + (ck_in >> 5)
    else:
        sub_q, sub_k = rq, ck_in
    is_cur = ck >= ATT_BLOCK
    j = sub_q - sub_k + jnp.where(is_cur, 0, ATT_BLOCK)
    valid = (j >= 0) & (j <= ATT_BLOCK) & (is_cur | (n > 0))
    neg_dist = jnp.where(valid, -(j * dil).astype(F32), -jnp.inf)

    lane = lax.broadcasted_iota(jnp.int32, (ATT_BLOCK, LANES), 1)
    ones = jnp.ones((nk, HEAD_DIM), BF16)
    nt = (((1,), (1,)), ((), ()))

    def head_slice(ref, h):
        x = ref[..., h * HEAD_DIM:(h + 1) * HEAD_DIM]
        return x.reshape(ATT_BLOCK, HEAD_DIM)

    def scores(h):
        kcat = jnp.concatenate([head_slice(kp_ref, h), head_slice(kc_ref, h)], axis=0)
        return lax.dot_general(head_slice(q_ref, h), kcat, nt, preferred_element_type=F32)

    pending = [scores(h) for h in range(QK_LOOKAHEAD)]
    lse_tile = jnp.zeros((ATT_BLOCK, LANES), F32)
    for h in range(ATT_HEADS):
        s = pending.pop(0)
        if h + QK_LOOKAHEAD < ATT_HEADS:
            pending.append(scores(h + QK_LOOKAHEAD))
        z = s + (slopes[h] * LOG2_E) * neg_dist
        m = jnp.max(z, axis=-1, keepdims=True)
        p = jnp.exp2(z - m).astype(BF16)
        vcat = jnp.concatenate([head_slice(vp_ref, h), head_slice(vc_ref, h)], axis=0)
        acc = jnp.dot(p, jnp.concatenate([vcat, ones], axis=1), preferred_element_type=F32)
        l = acc[:, HEAD_DIM:]
        o = (acc[:, :HEAD_DIM] / l).astype(o_ref.dtype)
        o_ref[..., h * HEAD_DIM:(h + 1) * HEAD_DIM] = o.reshape(o_ref.shape[:-1] + (HEAD_DIM,))
        lse_tile = jnp.where(lane == h, m * LN_2 + jnp.log(l), lse_tile)
    lse_ref[...] = lse_tile.reshape(lse_ref.shape)
    kp_ref[...] = kc_ref[...].reshape(kp_ref.shape)
    vp_ref[...] = vc_ref[...].reshape(vp_ref.shape)


def _dsa_branch(q, q_col, kv, g, bsz, s_len):
    d = DILATIONS[g]
    assert WINDOWS[g] // d == ATT_BLOCK
    t = bsz * s_len
    ntile = s_len // PERM_TILE
    if g == 0:
        view = lambda x: x.reshape(bsz, s_len, x.shape[-1])
        blk = lambda w: (None, ATT_BLOCK, w)
        grid = (bsz, 1, s_len // ATT_BLOCK)
        at = lambda c: (lambda b, r, n: (b, n, c))
    elif g == 1:
        view = lambda x: x.reshape(bsz, ntile, 4, 4, PERM_RUN, x.shape[-1])
        blk = lambda w: (None, None, None, 4, PERM_RUN, w)
        grid = (bsz, 4, ntile)
        at = lambda c: (lambda b, r, n: (b, n, r, 0, 0, c))
    else:
        view = lambda x: x.reshape(bsz, ntile, 16, PERM_RUN, x.shape[-1])
        blk = lambda w: (None, 4, None, PERM_RUN, w)
        grid = (bsz, 16, ntile // 4)
        at = lambda c: (lambda b, r, n: (b, n, r, 0, c))
    qv, kvv = view(q), view(kv)
    o, lse = pl.pallas_call(
        functools.partial(_dsa_kernel, dil=d, interleaved=(g == 1),
                          slopes=_alibi_slopes(ATT_HEADS)),
        grid=grid,
        in_specs=[
            pl.BlockSpec(blk(ATT_WIDTH), at(q_col)),
            pl.BlockSpec(blk(ATT_WIDTH), at(0)),
            pl.BlockSpec(blk(ATT_WIDTH), at(1)),
        ],
        out_specs=[pl.BlockSpec(blk(ATT_WIDTH), at(0)),
                   pl.BlockSpec(blk(LANES), at(0))],
        out_shape=[jax.ShapeDtypeStruct(qv.shape[:-1] + (ATT_WIDTH,), BF16),
                   jax.ShapeDtypeStruct(qv.shape[:-1] + (LANES,), F32)],
        scratch_shapes=[pltpu.VMEM((ATT_BLOCK, ATT_WIDTH), BF16),
                        pltpu.VMEM((ATT_BLOCK, ATT_WIDTH), BF16)],
        compiler_params=pltpu.CompilerParams(
            dimension_semantics=("arbitrary", "arbitrary", "arbitrary")),
        name=f"dsa_branch{g}",
    )(qv, kvv, kvv)
    return o.reshape(t, ATT_WIDTH), lse.reshape(t, LANES)


def _merge_kernel(pt_ref, o0_ref, o1_ref, o2_ref, l0_ref, l1_ref, l2_ref, o_ref):
    pt = pt_ref[...]
    l0 = l0_ref[...]
    l1 = _dot01(pt, l1_ref[...])
    l2 = _dot01(pt, l2_ref[...])
    m = jnp.maximum(jnp.maximum(l0, l1), l2)
    e0, e1, e2 = jnp.exp(l0 - m), jnp.exp(l1 - m), jnp.exp(l2 - m)
    den = e0 + e1 + e2
    w0, w1, w2 = e0 / den, e1 / den, e2 / den
    for hp in range(ATT_HEADS // 2):
        pair = slice(2 * hp * HEAD_DIM, (2 * hp + 2) * HEAD_DIM)
        o1 = jnp.dot(pt, o1_ref[:, pair], preferred_element_type=F32)
        o2 = jnp.dot(pt, o2_ref[:, pair], preferred_element_type=F32)
        for k in range(2):
            h = 2 * hp + k
            sl = slice(h * HEAD_DIM, (h + 1) * HEAD_DIM)
            in_pair = slice(k * HEAD_DIM, (k + 1) * HEAD_DIM)
            o = (w0[:, h:h + 1] * o0_ref[:, sl].astype(F32)
                 + w1[:, h:h + 1] * o1[:, in_pair] + w2[:, h:h + 1] * o2[:, in_pair])
            o_ref[:, sl] = o.astype(o_ref.dtype)


def _dsa_merge(outs, lses, perm_t):
    t, hd = outs[0].shape
    tm = PERM_TILE
    o_spec = pl.BlockSpec((tm, hd), lambda i: (i, 0))
    l_spec = pl.BlockSpec((tm, LANES), lambda i: (i, 0))
    return pl.pallas_call(
        _merge_kernel,
        grid=(t // tm,),
        in_specs=[pl.BlockSpec((tm, tm), lambda i: (0, 0)),
                  o_spec, o_spec, o_spec, l_spec, l_spec, l_spec],
        out_specs=o_spec,
        out_shape=jax.ShapeDtypeStruct((t, hd), BF16),
        compiler_params=pltpu.CompilerParams(dimension_semantics=("arbitrary",)),
        name="dsa_merge",
    )(perm_t, *outs, *lses)


def _ffn_up_kernel(x_ref, wu_ref, wg_ref, cw_ref, cb_ref, o_ref, wub_ref, wgb_ref,
                   us_ref, gs_ref, carry_ref, *, tm, tn, s_len):
    @pl.when(pl.program_id(1) == 0)
    def _():
        wub_ref[...] = wu_ref[...].astype(BF16)
        wgb_ref[...] = wg_ref[...].astype(BF16)

    @pl.when((pl.program_id(1) * tm) % s_len == 0)
    def _():
        carry_ref[...] = jnp.zeros_like(carry_ref)

    n_rb = tm // FFN_ROW_SUB
    subs = [(cb, rb) for cb in range(tn // FFN_COL_SUB) for rb in range(n_rb)]

    def window(idx):
        cb, rb = subs[idx]
        return (slice(rb * FFN_ROW_SUB, (rb + 1) * FFN_ROW_SUB),
                slice(cb * FFN_COL_SUB, (cb + 1) * FFN_COL_SUB))

    def project(idx):
        rows, cols = window(idx)
        slot = idx % FFN_SLOTS
        xr = x_ref[rows, :]
        us_ref[slot] = jnp.dot(xr, wub_ref[:, cols], preferred_element_type=F32)
        gs_ref[slot, 8:] = jnp.dot(xr, wgb_ref[:, cols], preferred_element_type=F32)
        if subs[idx][1] == 0:
            gs_ref[slot, :8] = carry_ref[:, cols]
        else:
            gs_ref[slot, :8] = gs_ref[(idx - 1) % FFN_SLOTS, FFN_ROW_SUB:]
        if subs[idx][1] == n_rb - 1:
            carry_ref[:, cols] = gs_ref[slot, FFN_ROW_SUB:]

    def epilogue(idx):
        rows, cols = window(idx)
        slot = idx % FFN_SLOTS
        g = gs_ref[slot, 8:]
        g1 = gs_ref[slot, 7:7 + FFN_ROW_SUB]
        g2 = gs_ref[slot, 6:6 + FFN_ROW_SUB]
        cw = cw_ref[:, cols]
        conv = cw[0:1] * g2 + cw[1:2] * g1 + cw[2:3] * g + cb_ref[:, cols]
        act = 0.5 * conv * (1.0 + lax.erf(conv * (2.0 ** -0.5)))
        o_ref[rows, cols] = (act * us_ref[slot]).astype(o_ref.dtype)

    ahead = FFN_SLOTS - 1
    for idx in range(ahead):
        project(idx)
    for idx in range(len(subs)):
        if idx + ahead < len(subs):
            project(idx + ahead)
        epilogue(idx)


def _ffn_up(xn, w_up, layer, conv_w, conv_b, s_len, tm=1024, tn=512):
    m, k = xn.shape
    nn = D_FF // tn
    assert s_len % tm == 0 and tm % FFN_ROW_SUB == 0 and tn % FFN_COL_SUB == 0
    return pl.pallas_call(
        functools.partial(_ffn_up_kernel, tm=tm, tn=tn, s_len=s_len),
        grid=(nn, m // tm),
        in_specs=[
            pl.BlockSpec((tm, k), lambda j, i: (i, 0)),
            pl.BlockSpec((None, k, tn), lambda j, i: (layer, 0, j)),
            pl.BlockSpec((None, k, tn), lambda j, i: (layer, 0, nn + j)),
            pl.BlockSpec((3, tn), lambda j, i: (0, j)),
            pl.BlockSpec((1, tn), lambda j, i: (0, j)),
        ],
        out_specs=pl.BlockSpec((tm, tn), lambda j, i: (i, j)),
        out_shape=jax.ShapeDtypeStruct((m, D_FF), BF16),
        scratch_shapes=[pltpu.VMEM((k, tn), BF16), pltpu.VMEM((k, tn), BF16),
                        pltpu.VMEM((FFN_SLOTS, FFN_ROW_SUB, FFN_COL_SUB), F32),
                        pltpu.VMEM((FFN_SLOTS, 8 + FFN_ROW_SUB, FFN_COL_SUB), F32),
                        pltpu.VMEM((8, tn), F32)],
        compiler_params=pltpu.CompilerParams(
            dimension_semantics=("arbitrary", "arbitrary")),
        name="ffn_up",
    )(xn, w_up, w_up, conv_w, conv_b.reshape(1, D_FF))


def _conv_glu(h, xn, w_up, layer, conv_w, conv_b, w_down, s_len):
    act = _ffn_up(xn, w_up, layer, conv_w, conv_b, s_len)
    return _matmul(act, w_down, layer, F32, residual=h, tn=512)


def kernel(x, attn_norm, gla_w_in, gla_w_a2, gla_b_a2, gla_head_norm, gla_w_out, kv_norm, w_kv,
           dsa_w_q, dsa_w_out, ffn_norm, ffn_w_up, ffn_conv_w, ffn_conv_b, ffn_w_down, final_norm):
    bsz, s_len, d = x.shape
    t = bsz * s_len
    h = x.reshape(t, d)

    xn = _rmsnorm(h, attn_norm[0], BF16)
    n_main = 2 * GLA_KEY_DIM + 2 * GLA_VAL_DIM
    proj = _matmul(xn, gla_w_in, 0, BF16, n_cols=n_main)
    w_a1 = jnp.pad(gla_w_in[0, :, n_main:].astype(BF16), ((0, 0), (0, LANES - GATE_RANK)))
    w_a2 = jnp.pad(gla_w_a2[0].astype(BF16), ((0, LANES - GATE_RANK), (0, 0)))
    la = _gla_gate(xn, w_a1, w_a2, gla_b_a2[0])
    o = _gla_recurrence(proj, la, gla_head_norm[0], bsz, s_len)
    h, xn = _matmul_res_norm(o, gla_w_out, 0, h, ffn_norm[0])
    h = _conv_glu(h, xn, ffn_w_up, 0, ffn_conv_w[0], ffn_conv_b[0], ffn_w_down, s_len)

    perm_np = _perm_matrix()
    perm = jnp.asarray(perm_np, BF16)
    perm_t = jnp.asarray(perm_np.T, BF16)
    xkv, xq, xq_perm = _rmsnorm2_perm(h, kv_norm, attn_norm[1], perm)
    kv, kv_perm = _matmul_and_permuted(xkv, w_kv[None], 0, perm)

    q_scale = HEAD_DIM ** -0.5 * LOG2_E
    q0 = _matmul(xq, dsa_w_q, 0, BF16, n_cols=ATT_WIDTH, out_scale=q_scale)
    q12 = _matmul(xq_perm, dsa_w_q, 0, BF16, n_cols=2 * ATT_WIDTH, col_start=ATT_WIDTH,
                  out_scale=q_scale)
    o0, lse0 = _dsa_branch(q0, 0, kv, 0, bsz, s_len)
    o1, lse1 = _dsa_branch(q12, 0, kv_perm, 1, bsz, s_len)
    o2, lse2 = _dsa_branch(q12, 1, kv_perm, 2, bsz, s_len)
    o = _dsa_merge([o0, o1, o2], [lse0, lse1, lse2], perm_t)
    h, xn = _matmul_res_norm(o, dsa_w_out, 0, h, ffn_norm[1])
    h = _conv_glu(h, xn, ffn_w_up, 1, ffn_conv_w[1], ffn_conv_b[1], ffn_w_down, s_len)

    return _rmsnorm(h, final_norm, F32).reshape(bsz, s_len, d)
```

```python
import functools
import math

import numpy as np
import jax
import jax.numpy as jnp
from jax import lax
from jax.experimental import pallas as pl
from jax.experimental.pallas import tpu as pltpu

D_MODEL = 2048
GLA_HEADS = 4
GLA_KEY_DIM = 1024
GLA_VAL_DIM = 2048
GLA_DK = 256
GLA_DV = 512
GATE_RANK = 16
GATE_NORMALIZER = 16.0
GLA_CHUNK = 64
GLA_BLOCK = 256
GLA_HEADS_PER_STEP = 4
ATT_HEADS = 16
HEAD_DIM = 128
ATT_WIDTH = ATT_HEADS * HEAD_DIM
WINDOWS = (128, 512, 2048)
DILATIONS = (1, 4, 16)
ATT_BLOCK = 128
DSA_BLOCKS_PER_STEP = 2
QK_LOOKAHEAD = 2
PERM_TILE = 512
PERM_RUN = PERM_TILE // 16
D_FF = 5632
EPS = 1e-6
LOG2_E = math.log2(math.e)
LN_2 = math.log(2.0)
LANES = 128
FFN_ROW_SUB = 512
FFN_COL_SUB = 256
FFN_SLOTS = 2

F32 = jnp.float32
BF16 = jnp.bfloat16


def _alibi_slopes(n):
    def pow2_slopes(m):
        start = 2.0 ** (-8.0 / m)
        return [start ** (i + 1) for i in range(m)]
    assert math.log2(n).is_integer()
    return [float(v) for v in np.array(pow2_slopes(n), dtype=np.float32)]


def _rmsnorm_kernel(x_ref, g_ref, o_ref):
    x = x_ref[...]
    ms = jnp.mean(x * x, axis=-1, keepdims=True)
    o_ref[...] = ((x * lax.rsqrt(ms + EPS)) * g_ref[...]).astype(o_ref.dtype)


def _rmsnorm(x, g, out_dtype, tm=512):
    m, d = x.shape
    return pl.pallas_call(
        _rmsnorm_kernel,
        grid=(m // tm,),
        in_specs=[pl.BlockSpec((tm, d), lambda i: (i, 0)),
                  pl.BlockSpec((1, d), lambda i: (0, 0))],
        out_specs=pl.BlockSpec((tm, d), lambda i: (i, 0)),
        out_shape=jax.ShapeDtypeStruct((m, d), out_dtype),
        compiler_params=pltpu.CompilerParams(dimension_semantics=("arbitrary",)),
        name="rmsnorm",
    )(x, g.reshape(1, d))


def _cast_weights_once(w_ref, wb_ref):
    @pl.when(pl.program_id(1) == 0)
    def _():
        wb_ref[...] = w_ref[...].astype(BF16)


def _mm_kernel(x_ref, w_ref, *rest, out_scale):
    o_ref, wb_ref = rest[-2:]
    _cast_weights_once(w_ref, wb_ref)
    acc = jnp.dot(x_ref[...], wb_ref[...], preferred_element_type=F32)
    if len(rest) == 3:
        acc = rest[0][...] + acc
    if out_scale is not None:
        acc = acc * out_scale
    o_ref[...] = acc.astype(o_ref.dtype)


def _matmul(x, w, layer, out_dtype, n_cols=None, col_start=0, residual=None, out_scale=None,
            tm=512, tn=1024):
    m, k = x.shape
    n = w.shape[2] if n_cols is None else n_cols
    assert w.shape[1] == k and m % tm == 0 and n % tn == 0 and col_start % tn == 0
    j0 = col_start // tn
    in_specs = [pl.BlockSpec((tm, k), lambda j, i: (i, 0)),
                pl.BlockSpec((None, k, tn), lambda j, i: (layer, 0, j0 + j))]
    args = [x, w]
    if residual is not None:
        in_specs.append(pl.BlockSpec((tm, tn), lambda j, i: (i, j)))
        args.append(residual)
    return pl.pallas_call(
        functools.partial(_mm_kernel, out_scale=out_scale),
        grid=(n // tn, m // tm),
        in_specs=in_specs,
        out_specs=pl.BlockSpec((tm, tn), lambda j, i: (i, j)),
        out_shape=jax.ShapeDtypeStruct((m, n), out_dtype),
        scratch_shapes=[pltpu.VMEM((k, tn), BF16)],
        compiler_params=pltpu.CompilerParams(
            dimension_semantics=("arbitrary", "arbitrary")),
        name="matmul",
    )(*args)


def _mm_perm_kernel(x_ref, w_ref, p_ref, o_ref, op_ref, wb_ref):
    _cast_weights_once(w_ref, wb_ref)
    out = jnp.dot(x_ref[...], wb_ref[...], preferred_element_type=F32).astype(BF16)
    o_ref[...] = out
    op_ref[...] = jnp.dot(p_ref[...], out, preferred_element_type=F32).astype(BF16)


def _matmul_and_permuted(x, w, layer, perm, tn=1024):
    m, k = x.shape
    n = w.shape[2]
    tm = PERM_TILE
    assert w.shape[1] == k and m % tm == 0 and n % tn == 0
    out_spec = pl.BlockSpec((tm, tn), lambda j, i: (i, j))
    out = jax.ShapeDtypeStruct((m, n), BF16)
    return pl.pallas_call(
        _mm_perm_kernel,
        grid=(n // tn, m // tm),
        in_specs=[pl.BlockSpec((tm, k), lambda j, i: (i, 0)),
                  pl.BlockSpec((None, k, tn), lambda j, i: (layer, 0, j)),
                  pl.BlockSpec((tm, tm), lambda j, i: (0, 0))],
        out_specs=[out_spec, out_spec],
        out_shape=[out, out],
        scratch_shapes=[pltpu.VMEM((k, tn), BF16)],
        compiler_params=pltpu.CompilerParams(
            dimension_semantics=("arbitrary", "arbitrary")),
        name="matmul_and_permuted",
    )(x, w, perm)


def _mm_res_norm_kernel(x_ref, w_ref, r_ref, g_ref, h_ref, xn_ref, wb_ref):
    @pl.when(pl.program_id(0) == 0)
    def _():
        wb_ref[...] = w_ref[...].astype(BF16)

    h = r_ref[...] + jnp.dot(x_ref[...], wb_ref[...], preferred_element_type=F32)
    h_ref[...] = h
    ms = jnp.mean(h * h, axis=-1, keepdims=True)
    xn_ref[...] = ((h * lax.rsqrt(ms + EPS)) * g_ref[...]).astype(xn_ref.dtype)


def _matmul_res_norm(x, w, layer, residual, norm_g, tm=512):
    m, k = x.shape
    n = w.shape[2]
    assert w.shape[1] == k and residual.shape == (m, n) and m % tm == 0
    row = pl.BlockSpec((tm, n), lambda i: (i, 0))
    return pl.pallas_call(
        _mm_res_norm_kernel,
        grid=(m // tm,),
        in_specs=[pl.BlockSpec((tm, k), lambda i: (i, 0)),
                  pl.BlockSpec((None, k, n), lambda i: (layer, 0, 0),
                               pipeline_mode=pl.Buffered(1)),
                  row,
                  pl.BlockSpec((1, n), lambda i: (0, 0))],
        out_specs=[row, row],
        out_shape=[jax.ShapeDtypeStruct((m, n), F32), jax.ShapeDtypeStruct((m, n), BF16)],
        scratch_shapes=[pltpu.VMEM((k, n), BF16)],
        compiler_params=pltpu.CompilerParams(dimension_semantics=("arbitrary",)),
        name="matmul_res_norm",
    )(x, w, residual, norm_g.reshape(1, n))


def _norm2_perm_kernel(x_ref, ga_ref, gb_ref, p_ref, a_ref, b_ref, bp_ref):
    x = x_ref[...]
    ms = jnp.mean(x * x, axis=-1, keepdims=True)
    y = x * lax.rsqrt(ms + EPS)
    a_ref[...] = (y * ga_ref[...]).astype(BF16)
    b = (y * gb_ref[...]).astype(BF16)
    b_ref[...] = b
    bp_ref[...] = jnp.dot(p_ref[...], b, preferred_element_type=F32).astype(BF16)


def _rmsnorm2_perm(x, g_a, g_b, perm):
    m, d = x.shape
    tm = PERM_TILE
    row = pl.BlockSpec((tm, d), lambda i: (i, 0))
    vec = pl.BlockSpec((1, d), lambda i: (0, 0))
    out = jax.ShapeDtypeStruct((m, d), BF16)
    return pl.pallas_call(
        _norm2_perm_kernel,
        grid=(m // tm,),
        in_specs=[row, vec, vec, pl.BlockSpec((tm, tm), lambda i: (0, 0))],
        out_specs=[row, row, row],
        out_shape=[out, out, out],
        compiler_params=pltpu.CompilerParams(dimension_semantics=("arbitrary",)),
        name="rmsnorm2_perm",
    )(x, g_a.reshape(1, d), g_b.reshape(1, d), perm)


def _gate_kernel(x_ref, w1_ref, w2_ref, b_ref, o_ref):
    a = jnp.dot(x_ref[...], w1_ref[...], preferred_element_type=F32)
    z = jnp.dot(a.astype(BF16), w2_ref[...], preferred_element_type=F32) + b_ref[...]
    log_sig = jnp.minimum(z, 0.0) - jnp.log1p(jnp.exp(-jnp.abs(z)))
    o_ref[...] = log_sig / GATE_NORMALIZER


def _gla_gate(xn, w_a1, w_a2, b_a2, tm=512):
    m, k = xn.shape
    return pl.pallas_call(
        _gate_kernel,
        grid=(m // tm,),
        in_specs=[pl.BlockSpec((tm, k), lambda i: (i, 0)),
                  pl.BlockSpec((k, LANES), lambda i: (0, 0)),
                  pl.BlockSpec((LANES, GLA_KEY_DIM), lambda i: (0, 0)),
                  pl.BlockSpec((1, GLA_KEY_DIM), lambda i: (0, 0))],
        out_specs=pl.BlockSpec((tm, GLA_KEY_DIM), lambda i: (i, 0)),
        out_shape=jax.ShapeDtypeStruct((m, GLA_KEY_DIM), F32),
        compiler_params=pltpu.CompilerParams(dimension_semantics=("arbitrary",)),
        name="gla_gate",
    )(xn, w_a1, w_a2, b_a2.reshape(1, GLA_KEY_DIM))


def _split3(x):
    hi = x.astype(BF16)
    r1 = x - hi.astype(F32)
    mid = r1.astype(BF16)
    lo = (r1 - mid.astype(F32)).astype(BF16)
    return hi, mid, lo


def _dot01(mat01, x):
    hi, mid, lo = _split3(x)
    return (jnp.dot(mat01, hi, preferred_element_type=F32)
            + jnp.dot(mat01, mid, preferred_element_type=F32)
            + jnp.dot(mat01, lo, preferred_element_type=F32))


def _gla_kernel(q_ref, k_ref, v_ref, r_ref, la_ref, hn_ref, o_ref, st_ref):
    @pl.when(pl.program_id(2) == 0)
    def _():
        st_ref[...] = jnp.zeros_like(st_ref)

    nb = GLA_BLOCK
    heads = range(GLA_HEADS_PER_STEP)
    dk = lambda h: slice(h * GLA_DK, (h + 1) * GLA_DK)
    dv = lambda h: slice(h * GLA_DV, (h + 1) * GLA_DV)
    row = lax.broadcasted_iota(jnp.int32, (nb, nb), 0)
    col = lax.broadcasted_iota(jnp.int32, (nb, nb), 1)
    same_chunk = (row // GLA_CHUNK) == (col // GLA_CHUNK)
    causal = same_chunk & (col <= row)
    tri = jnp.where(causal, 1.0, 0.0).astype(BF16)
    nt = (((1,), (1,)), ((), ()))
    tn = (((0,), (0,)), ((), ()))

    cum = [_dot01(tri, la_ref[:, dk(h)]) for h in heads]
    k = [k_ref[:, dk(h)].astype(F32) for h in heads]
    q_dec = [((q_ref[:, dk(h)].astype(F32) * (GLA_DK ** -0.5)) * jnp.exp(cum[h])).astype(BF16)
             for h in heads]
    k_inv = [(k[h] * jnp.exp(-cum[h])).astype(BF16) for h in heads]
    scores = [lax.dot_general(q_dec[h], k_inv[h], nt, preferred_element_type=F32)
              for h in heads]
    scores = [jnp.where(causal, s, 0.0).astype(BF16) for s in scores]
    o_intra = [jnp.dot(scores[h], v_ref[:, dv(h)], preferred_element_type=F32)
               for h in heads]

    o_parts = [[] for _ in heads]
    for c in range(nb // GLA_CHUNK):
        rows = slice(c * GLA_CHUNK, (c + 1) * GLA_CHUNK)
        for h in heads:
            last_c = cum[h][(c + 1) * GLA_CHUNK - 1:(c + 1) * GLA_CHUNK]
            k_end = (k[h][rows] * jnp.exp(last_c - cum[h][rows])).astype(BF16)
            st = st_ref[h]
            o_inter = lax.dot_general(q_dec[h][rows], st.astype(BF16), nt,
                                      preferred_element_type=F32)
            upd = lax.dot_general(v_ref[rows, dv(h)], k_end, tn,
                                  preferred_element_type=F32)
            st_ref[h] = st * jnp.exp(last_c) + upd
            o_parts[h].append(o_intra[h][rows] + o_inter)

    for h in heads:
        o = jnp.concatenate(o_parts[h], axis=0)
        ms = jnp.mean(o * o, axis=-1, keepdims=True)
        o = (o * lax.rsqrt(ms + EPS)) * hn_ref[...]
        r = r_ref[:, dv(h)].astype(F32)
        gate = r * (1.0 / (1.0 + jnp.exp(-r)))
        o_ref[:, dv(h)] = (o * gate).astype(o_ref.dtype)


def _gla_recurrence(proj, la, head_norm, bsz, s_len):
    t = bsz * s_len
    nblk = s_len // GLA_BLOCK
    hps = GLA_HEADS_PER_STEP
    wk, wv = hps * GLA_DK, hps * GLA_DV
    k_off = GLA_KEY_DIM // wk
    v_off = 2 * GLA_KEY_DIM // wv
    r_off = (2 * GLA_KEY_DIM + GLA_VAL_DIM) // wv
    row = lambda b, g, i: b * nblk + i
    return pl.pallas_call(
        _gla_kernel,
        grid=(bsz, GLA_HEADS // hps, nblk),
        in_specs=[
            pl.BlockSpec((GLA_BLOCK, wk), lambda b, g, i: (row(b, g, i), g)),
            pl.BlockSpec((GLA_BLOCK, wk), lambda b, g, i: (row(b, g, i), k_off + g)),
            pl.BlockSpec((GLA_BLOCK, wv), lambda b, g, i: (row(b, g, i), v_off + g)),
            pl.BlockSpec((GLA_BLOCK, wv), lambda b, g, i: (row(b, g, i), r_off + g)),
            pl.BlockSpec((GLA_BLOCK, wk), lambda b, g, i: (row(b, g, i), g)),
            pl.BlockSpec((1, GLA_DV), lambda b, g, i: (0, 0)),
        ],
        out_specs=pl.BlockSpec((GLA_BLOCK, wv), lambda b, g, i: (row(b, g, i), g)),
        out_shape=jax.ShapeDtypeStruct((t, GLA_VAL_DIM), BF16),
        scratch_shapes=[pltpu.VMEM((hps, GLA_DV, GLA_DK), F32)],
        compiler_params=pltpu.CompilerParams(
            dimension_semantics=("arbitrary", "arbitrary", "arbitrary")),
        name="gla_recurrence",
    )(proj, proj, proj, proj, la, head_norm.reshape(1, GLA_DV))


def _perm_matrix():
    p = np.zeros((PERM_TILE, PERM_TILE), np.float32)
    for i in range(PERM_RUN):
        for a in range(4):
            for r4 in range(4):
                p[(4 * r4 + a) * PERM_RUN + i, 16 * i + 4 * a + r4] = 1.0
    return p


def _dsa_lead(layout, s):
    if layout == "rows":
        return (slice(s * ATT_BLOCK, (s + 1) * ATT_BLOCK),)
    if layout == "tile":
        return (s, slice(None), slice(None))
    assert layout == "tiles4"
    return (slice(4 * s, 4 * s + 4), slice(None))


def _dsa_block_shape(layout, width):
    return (ATT_BLOCK, width) if layout == "rows" else (4, PERM_RUN, width)


def _dsa_kernel(q_ref, kc_ref, vc_ref, o_ref, lse_ref, kp_ref, vp_ref, *, dil, layout, slopes):
    n = pl.program_id(2)

    @pl.when(n == 0)
    def _():
        kp_ref[...] = jnp.zeros_like(kp_ref)
        vp_ref[...] = jnp.zeros_like(vp_ref)

    nk = 2 * ATT_BLOCK
    rq = lax.broadcasted_iota(jnp.int32, (ATT_BLOCK, nk), 0)
    ck = lax.broadcasted_iota(jnp.int32, (ATT_BLOCK, nk), 1)
    ck_in = ck & (ATT_BLOCK - 1)
    if layout == "tile":
        sub_q = 4 * (rq & (PERM_RUN - 1)) + (rq >> 5)
        sub_k = 4 * (ck_in & (PERM_RUN - 1)) + (ck_in >> 5)
    else:
        sub_q, sub_k = rq, ck_in
    is_cur = ck >= ATT_BLOCK
    j = sub_q - sub_k + jnp.where(is_cur, 0, ATT_BLOCK)
    valid = (j >= 0) & (j <= ATT_BLOCK)
    neg_dist_rest = jnp.where(valid, -(j * dil).astype(F32), -jnp.inf)
    neg_dist_first = jnp.where(is_cur | (n > 0), neg_dist_rest, -jnp.inf)

    lane = lax.broadcasted_iota(jnp.int32, (ATT_BLOCK, LANES), 1)
    ones = jnp.ones((nk, HEAD_DIM), BF16)
    nt = (((1,), (1,)), ((), ()))

    def head(ref, s, h):
        x = ref[_dsa_lead(layout, s) + (slice(h * HEAD_DIM, (h + 1) * HEAD_DIM),)]
        return x.reshape(ATT_BLOCK, HEAD_DIM)

    def prev_head(ref, carry_ref, s, h):
        if s == 0:
            return carry_ref[:, h * HEAD_DIM:(h + 1) * HEAD_DIM]
        return head(ref, s - 1, h)

    for s in range(DSA_BLOCKS_PER_STEP):
        neg_dist = neg_dist_first if s == 0 else neg_dist_rest
        lead = _dsa_lead(layout, s)

        def scores(h, s=s):
            kcat = jnp.concatenate([prev_head(kc_ref, kp_ref, s, h), head(kc_ref, s, h)], axis=0)
            return lax.dot_general(head(q_ref, s, h), kcat, nt, preferred_element_type=F32)

        pending = [scores(h) for h in range(QK_LOOKAHEAD)]
        lse_tile = jnp.zeros((ATT_BLOCK, LANES), F32)
        for h in range(ATT_HEADS):
            sc = pending.pop(0)
            if h + QK_LOOKAHEAD < ATT_HEADS:
                pending.append(scores(h + QK_LOOKAHEAD))
            z = sc + (slopes[h] * LOG2_E) * neg_dist
            m = jnp.max(z, axis=-1, keepdims=True)
            p = jnp.exp2(z - m).astype(BF16)
            vcat = jnp.concatenate([prev_head(vc_ref, vp_ref, s, h), head(vc_ref, s, h)], axis=0)
            acc = jnp.dot(p, jnp.concatenate([vcat, ones], axis=1),
                          preferred_element_type=F32)
            l = acc[:, HEAD_DIM:]
            o = (acc[:, :HEAD_DIM] / l).astype(o_ref.dtype)
            o_ref[lead + (slice(h * HEAD_DIM, (h + 1) * HEAD_DIM),)] = o.reshape(
                _dsa_block_shape(layout, HEAD_DIM))
            lse_tile = jnp.where(lane == h, m * LN_2 + jnp.log(l), lse_tile)
        lse_ref[lead + (slice(None),)] = lse_tile.reshape(_dsa_block_shape(layout, LANES))

    last = _dsa_lead(layout, DSA_BLOCKS_PER_STEP - 1) + (slice(None),)
    kp_ref[...] = kc_ref[last].reshape(kp_ref.shape)
    vp_ref[...] = vc_ref[last].reshape(vp_ref.shape)


def _dsa_branch(q, q_col, kv, g, bsz, s_len):
    d = DILATIONS[g]
    assert WINDOWS[g] // d == ATT_BLOCK
    t = bsz * s_len
    ntile = s_len // PERM_TILE
    qb = DSA_BLOCKS_PER_STEP
    if g == 0:
        layout = "rows"
        view = lambda x: x.reshape(bsz, s_len, x.shape[-1])
        blk = lambda w: (None, qb * ATT_BLOCK, w)
        grid = (bsz, 1, s_len // (qb * ATT_BLOCK))
        at = lambda c: (lambda b, r, n: (b, n, c))
    elif g == 1:
        layout = "tile"
        view = lambda x: x.reshape(bsz, ntile, 4, 4, PERM_RUN, x.shape[-1])
        blk = lambda w: (None, qb, None, 4, PERM_RUN, w)
        grid = (bsz, 4, ntile // qb)
        at = lambda c: (lambda b, r, n: (b, n, r, 0, 0, c))
    else:
        layout = "tiles4"
        view = lambda x: x.reshape(bsz, ntile, 16, PERM_RUN, x.shape[-1])
        blk = lambda w: (None, 4 * qb, None, PERM_RUN, w)
        grid = (bsz, 16, ntile // (4 * qb))
        at = lambda c: (lambda b, r, n: (b, n, r, 0, c))
    qv, kvv = view(q), view(kv)
    o, lse = pl.pallas_call(
        functools.partial(_dsa_kernel, dil=d, layout=layout, slopes=_alibi_slopes(ATT_HEADS)),
        grid=grid,
        in_specs=[
            pl.BlockSpec(blk(ATT_WIDTH), at(q_col)),
            pl.BlockSpec(blk(ATT_WIDTH), at(0)),
            pl.BlockSpec(blk(ATT_WIDTH), at(1)),
        ],
        out_specs=[pl.BlockSpec(blk(ATT_WIDTH), at(0)),
                   pl.BlockSpec(blk(LANES), at(0))],
        out_shape=[jax.ShapeDtypeStruct(qv.shape[:-1] + (ATT_WIDTH,), BF16),
                   jax.ShapeDtypeStruct(qv.shape[:-1] + (LANES,), F32)],
        scratch_shapes=[pltpu.VMEM((ATT_BLOCK, ATT_WIDTH), BF16),
                        pltpu.VMEM((ATT_BLOCK, ATT_WIDTH), BF16)],
        compiler_params=pltpu.CompilerParams(
            dimension_semantics=("arbitrary", "arbitrary", "arbitrary")),
        name=f"dsa_branch{g}",
    )(qv, kvv, kvv)
    return o.reshape(t, ATT_WIDTH), lse.reshape(t, LANES)


def _merge_kernel(pt_ref, o0_ref, o1_ref, o2_ref, l0_ref, l1_ref, l2_ref, o_ref):
    pt = pt_ref[...]
    l0 = l0_ref[...]
    l12 = _dot01(pt, jnp.concatenate([l1_ref[...], l2_ref[...]], axis=1))
    l1, l2 = l12[:, :LANES], l12[:, LANES:]
    m = jnp.maximum(jnp.maximum(l0, l1), l2)
    e0, e1, e2 = jnp.exp(l0 - m), jnp.exp(l1 - m), jnp.exp(l2 - m)
    den = e0 + e1 + e2
    w1, w2 = e1 / den, e2 / den
    for hp in range(ATT_HEADS // 2):
        pair = slice(2 * hp * HEAD_DIM, (2 * hp + 2) * HEAD_DIM)
        o1 = jnp.dot(pt, o1_ref[:, pair], preferred_element_type=F32)
        o2 = jnp.dot(pt, o2_ref[:, pair], preferred_element_type=F32)
        for k in range(2):
            h = 2 * hp + k
            sl = slice(h * HEAD_DIM, (h + 1) * HEAD_DIM)
            in_pair = slice(k * HEAD_DIM, (k + 1) * HEAD_DIM)
            o0 = o0_ref[:, sl].astype(F32)
            o = (o0 + w1[:, h:h + 1] * (o1[:, in_pair] - o0)
                 + w2[:, h:h + 1] * (o2[:, in_pair] - o0))
            o_ref[:, sl] = o.astype(o_ref.dtype)


def _dsa_merge(outs, lses, perm_t):
    t, hd = outs[0].shape
    tm = PERM_TILE
    o_spec = pl.BlockSpec((tm, hd), lambda i: (i, 0))
    l_spec = pl.BlockSpec((tm, LANES), lambda i: (i, 0))
    return pl.pallas_call(
        _merge_kernel,
        grid=(t // tm,),
        in_specs=[pl.BlockSpec((tm, tm), lambda i: (0, 0)),
                  o_spec, o_spec, o_spec, l_spec, l_spec, l_spec],
        out_specs=o_spec,
        out_shape=jax.ShapeDtypeStruct((t, hd), BF16),
        compiler_params=pltpu.CompilerParams(dimension_semantics=("arbitrary",)),
        name="dsa_merge",
    )(perm_t, *outs, *lses)


def _ffn_up_kernel(x_ref, wu_ref, wg_ref, cw_ref, cb_ref, o_ref, wub_ref, wgb_ref,
                   us_ref, gs_ref, carry_ref, *, tm, tn, s_len):
    @pl.when(pl.program_id(1) == 0)
    def _():
        wub_ref[...] = wu_ref[...].astype(BF16)
        wgb_ref[...] = wg_ref[...].astype(BF16)

    @pl.when((pl.program_id(1) * tm) % s_len == 0)
    def _():
        carry_ref[...] = jnp.zeros_like(carry_ref)

    n_rb = tm // FFN_ROW_SUB
    subs = [(cb, rb) for cb in range(tn // FFN_COL_SUB) for rb in range(n_rb)]

    def window(idx):
        cb, rb = subs[idx]
        return (slice(rb * FFN_ROW_SUB, (rb + 1) * FFN_ROW_SUB),
                slice(cb * FFN_COL_SUB, (cb + 1) * FFN_COL_SUB))

    def project(idx):
        rows, cols = window(idx)
        slot = idx % FFN_SLOTS
        xr = x_ref[rows, :]
        us_ref[slot] = jnp.dot(xr, wub_ref[:, cols], preferred_element_type=F32)
        gs_ref[slot, 8:] = jnp.dot(xr, wgb_ref[:, cols], preferred_element_type=F32)
        if subs[idx][1] == 0:
            gs_ref[slot, :8] = carry_ref[:, cols]
        else:
            gs_ref[slot, :8] = gs_ref[(idx - 1) % FFN_SLOTS, FFN_ROW_SUB:]
        if subs[idx][1] == n_rb - 1:
            carry_ref[:, cols] = gs_ref[slot, FFN_ROW_SUB:]

    def epilogue(idx):
        rows, cols = window(idx)
        slot = idx % FFN_SLOTS
        g = gs_ref[slot, 8:]
        g1 = gs_ref[slot, 7:7 + FFN_ROW_SUB]
        g2 = gs_ref[slot, 6:6 + FFN_ROW_SUB]
        cw = cw_ref[:, cols]
        conv = cw[0:1] * g2 + cw[1:2] * g1 + cw[2:3] * g + cb_ref[:, cols]
        act = 0.5 * conv * (1.0 + lax.erf(conv * (2.0 ** -0.5)))
        o_ref[rows, cols] = (act * us_ref[slot]).astype(o_ref.dtype)

    ahead = FFN_SLOTS - 1
    for idx in range(ahead):
        project(idx)
    for idx in range(len(subs)):
        if idx + ahead < len(subs):
            project(idx + ahead)
        epilogue(idx)


def _ffn_up(xn, w_up, layer, conv_w, conv_b, s_len, tm=2048, tn=512):
    m, k = xn.shape
    nn = D_FF // tn
    assert s_len % tm == 0 and tm % FFN_ROW_SUB == 0 and tn % FFN_COL_SUB == 0
    return pl.pallas_call(
        functools.partial(_ffn_up_kernel, tm=tm, tn=tn, s_len=s_len),
        grid=(nn, m // tm),
        in_specs=[
            pl.BlockSpec((tm, k), lambda j, i: (i, 0)),
            pl.BlockSpec((None, k, tn), lambda j, i: (layer, 0, j)),
            pl.BlockSpec((None, k, tn), lambda j, i: (layer, 0, nn + j)),
            pl.BlockSpec((3, tn), lambda j, i: (0, j)),
            pl.BlockSpec((1, tn), lambda j, i: (0, j)),
        ],
        out_specs=pl.BlockSpec((tm, tn), lambda j, i: (i, j)),
        out_shape=jax.ShapeDtypeStruct((m, D_FF), BF16),
        scratch_shapes=[pltpu.VMEM((k, tn), BF16), pltpu.VMEM((k, tn), BF16),
                        pltpu.VMEM((FFN_SLOTS, FFN_ROW_SUB, FFN_COL_SUB), F32),
                        pltpu.VMEM((FFN_SLOTS, 8 + FFN_ROW_SUB, FFN_COL_SUB), F32),
                        pltpu.VMEM((8, tn), F32)],
        compiler_params=pltpu.CompilerParams(
            dimension_semantics=("arbitrary", "arbitrary")),
        name="ffn_up",
    )(xn, w_up, w_up, conv_w, conv_b.reshape(1, D_FF))


def _conv_glu(h, xn, w_up, layer, conv_w, conv_b, w_down, s_len):
    act = _ffn_up(xn, w_up, layer, conv_w, conv_b, s_len)
    return _matmul(act, w_down, layer, F32, residual=h, tn=512)


def kernel(x, attn_norm, gla_w_in, gla_w_a2, gla_b_a2, gla_head_norm, gla_w_out, kv_norm, w_kv,
           dsa_w_q, dsa_w_out, ffn_norm, ffn_w_up, ffn_conv_w, ffn_conv_b, ffn_w_down, final_norm):
    bsz, s_len, d = x.shape
    t = bsz * s_len
    h = x.reshape(t, d)

    xn = _rmsnorm(h, attn_norm[0], BF16)
    n_main = 2 * GLA_KEY_DIM + 2 * GLA_VAL_DIM
    proj = _matmul(xn, gla_w_in, 0, BF16, n_cols=n_main, tm=1024)
    w_a1 = jnp.pad(gla_w_in[0, :, n_main:].astype(BF16), ((0, 0), (0, LANES - GATE_RANK)))
    w_a2 = jnp.pad(gla_w_a2[0].astype(BF16), ((0, LANES - GATE_RANK), (0, 0)))
    la = _gla_gate(xn, w_a1, w_a2, gla_b_a2[0])
    o = _gla_recurrence(proj, la, gla_head_norm[0], bsz, s_len)
    h, xn = _matmul_res_norm(o, gla_w_out, 0, h, ffn_norm[0])
    h = _conv_glu(h, xn, ffn_w_up, 0, ffn_conv_w[0], ffn_conv_b[0], ffn_w_down, s_len)

    perm_np = _perm_matrix()
    perm = jnp.asarray(perm_np, BF16)
    perm_t = jnp.asarray(perm_np.T, BF16)
    xkv, xq, xq_perm = _rmsnorm2_perm(h, kv_norm, attn_norm[1], perm)
    kv, kv_perm = _matmul_and_permuted(xkv, w_kv[None], 0, perm)

    q_scale = HEAD_DIM ** -0.5 * LOG2_E
    q0 = _matmul(xq, dsa_w_q, 0, BF16, n_cols=ATT_WIDTH, out_scale=q_scale, tm=1024)
    q12 = _matmul(xq_perm, dsa_w_q, 0, BF16, n_cols=2 * ATT_WIDTH, col_start=ATT_WIDTH,
                  out_scale=q_scale, tm=1024)
    o0, lse0 = _dsa_branch(q0, 0, kv, 0, bsz, s_len)
    o1, lse1 = _dsa_branch(q12, 0, kv_perm, 1, bsz, s_len)
    o2, lse2 = _dsa_branch(q12, 1, kv_perm, 2, bsz, s_len)
    o = _dsa_merge([o0, o1, o2], [lse0, lse1, lse2], perm_t)
    h, xn = _matmul_res_norm(o, dsa_w_out, 0, h, ffn_norm[1])
    h = _conv_glu(h, xn, ffn_w_up, 1, ffn_conv_w[1], ffn_conv_b[1], ffn_w_down, s_len)

    return _rmsnorm(h, final_norm, F32).reshape(bsz, s_len, d)
```

```python
import functools
import math

import numpy as np
import jax
import jax.numpy as jnp
from jax import lax
from jax.experimental import pallas as pl
from jax.experimental.pallas import tpu as pltpu

D_MODEL = 2048
GLA_HEADS = 4
GLA_KEY_DIM = 1024
GLA_VAL_DIM = 2048
GLA_DK = 256
GLA_DV = 512
GATE_RANK = 16
GATE_NORMALIZER = 16.0
GLA_CHUNK = 64
GLA_BLOCK = 256
GLA_HEADS_PER_STEP = 4
ATT_HEADS = 16
HEAD_DIM = 128
ATT_WIDTH = ATT_HEADS * HEAD_DIM
WINDOWS = (128, 512, 2048)
DILATIONS = (1, 4, 16)
ATT_BLOCK = 128
DSA_BLOCKS_PER_STEP = 4
QK_LOOKAHEAD = 2
PERM_TILE = 512
PERM_RUN = PERM_TILE // 16
D_FF = 5632
EPS = 1e-6
LOG2_E = math.log2(math.e)
LN_2 = math.log(2.0)
LANES = 128
FFN_ROW_SUB = 512
FFN_COL_SUB = 256
FFN_SLOTS = 2

F32 = jnp.float32
BF16 = jnp.bfloat16


def _alibi_slopes(n):
    def pow2_slopes(m):
        start = 2.0 ** (-8.0 / m)
        return [start ** (i + 1) for i in range(m)]
    assert math.log2(n).is_integer()
    return [float(v) for v in np.array(pow2_slopes(n), dtype=np.float32)]


def _rmsnorm_kernel(x_ref, g_ref, o_ref):
    x = x_ref[...]
    ms = jnp.mean(x * x, axis=-1, keepdims=True)
    o_ref[...] = ((x * lax.rsqrt(ms + EPS)) * g_ref[...]).astype(o_ref.dtype)


def _rmsnorm(x, g, out_dtype, tm=512):
    m, d = x.shape
    return pl.pallas_call(
        _rmsnorm_kernel,
        grid=(m // tm,),
        in_specs=[pl.BlockSpec((tm, d), lambda i: (i, 0)),
                  pl.BlockSpec((1, d), lambda i: (0, 0))],
        out_specs=pl.BlockSpec((tm, d), lambda i: (i, 0)),
        out_shape=jax.ShapeDtypeStruct((m, d), out_dtype),
        compiler_params=pltpu.CompilerParams(dimension_semantics=("arbitrary",)),
        name="rmsnorm",
    )(x, g.reshape(1, d))


def _cast_weights_once(w_ref, wb_ref):
    @pl.when(pl.program_id(1) == 0)
    def _():
        wb_ref[...] = w_ref[...].astype(BF16)


def _mm_kernel(x_ref, w_ref, *rest, out_scale):
    o_ref, wb_ref = rest[-2:]
    _cast_weights_once(w_ref, wb_ref)
    acc = jnp.dot(x_ref[...], wb_ref[...], preferred_element_type=F32)
    if len(rest) == 3:
        acc = rest[0][...] + acc
    if out_scale is not None:
        acc = acc * out_scale
    o_ref[...] = acc.astype(o_ref.dtype)


def _matmul(x, w, layer, out_dtype, n_cols=None, col_start=0, residual=None, out_scale=None,
            tm=512, tn=1024):
    m, k = x.shape
    n = w.shape[2] if n_cols is None else n_cols
    assert w.shape[1] == k and m % tm == 0 and n % tn == 0 and col_start % tn == 0
    j0 = col_start // tn
    in_specs = [pl.BlockSpec((tm, k), lambda j, i: (i, 0)),
                pl.BlockSpec((None, k, tn), lambda j, i: (layer, 0, j0 + j))]
    args = [x, w]
    if residual is not None:
        in_specs.append(pl.BlockSpec((tm, tn), lambda j, i: (i, j)))
        args.append(residual)
    return pl.pallas_call(
        functools.partial(_mm_kernel, out_scale=out_scale),
        grid=(n // tn, m // tm),
        in_specs=in_specs,
        out_specs=pl.BlockSpec((tm, tn), lambda j, i: (i, j)),
        out_shape=jax.ShapeDtypeStruct((m, n), out_dtype),
        scratch_shapes=[pltpu.VMEM((k, tn), BF16)],
        compiler_params=pltpu.CompilerParams(
            dimension_semantics=("arbitrary", "arbitrary")),
        name="matmul",
    )(*args)


def _mm_perm_kernel(x_ref, w_ref, p_ref, o_ref, op_ref, wb_ref):
    _cast_weights_once(w_ref, wb_ref)
    out = jnp.dot(x_ref[...], wb_ref[...], preferred_element_type=F32).astype(BF16)
    o_ref[...] = out
    op_ref[...] = jnp.dot(p_ref[...], out, preferred_element_type=F32).astype(BF16)


def _matmul_and_permuted(x, w, layer, perm, tn=1024):
    m, k = x.shape
    n = w.shape[2]
    tm = PERM_TILE
    assert w.shape[1] == k and m % tm == 0 and n % tn == 0
    out_spec = pl.BlockSpec((tm, tn), lambda j, i: (i, j))
    out = jax.ShapeDtypeStruct((m, n), BF16)
    return pl.pallas_call(
        _mm_perm_kernel,
        grid=(n // tn, m // tm),
        in_specs=[pl.BlockSpec((tm, k), lambda j, i: (i, 0)),
                  pl.BlockSpec((None, k, tn), lambda j, i: (layer, 0, j)),
                  pl.BlockSpec((tm, tm), lambda j, i: (0, 0))],
        out_specs=[out_spec, out_spec],
        out_shape=[out, out],
        scratch_shapes=[pltpu.VMEM((k, tn), BF16)],
        compiler_params=pltpu.CompilerParams(
            dimension_semantics=("arbitrary", "arbitrary")),
        name="matmul_and_permuted",
    )(x, w, perm)


def _mm_res_norm_kernel(x_ref, w_ref, r_ref, g_ref, h_ref, xn_ref, wb_ref):
    @pl.when(pl.program_id(0) == 0)
    def _():
        wb_ref[...] = w_ref[...].astype(BF16)

    h = r_ref[...] + jnp.dot(x_ref[...], wb_ref[...], preferred_element_type=F32)
    h_ref[...] = h
    ms = jnp.mean(h * h, axis=-1, keepdims=True)
    xn_ref[...] = ((h * lax.rsqrt(ms + EPS)) * g_ref[...]).astype(xn_ref.dtype)


def _matmul_res_norm(x, w, layer, residual, norm_g, tm=512):
    m, k = x.shape
    n = w.shape[2]
    assert w.shape[1] == k and residual.shape == (m, n) and m % tm == 0
    row = pl.BlockSpec((tm, n), lambda i: (i, 0))
    return pl.pallas_call(
        _mm_res_norm_kernel,
        grid=(m // tm,),
        in_specs=[pl.BlockSpec((tm, k), lambda i: (i, 0)),
                  pl.BlockSpec((None, k, n), lambda i: (layer, 0, 0),
                               pipeline_mode=pl.Buffered(1)),
                  row,
                  pl.BlockSpec((1, n), lambda i: (0, 0))],
        out_specs=[row, row],
        out_shape=[jax.ShapeDtypeStruct((m, n), F32), jax.ShapeDtypeStruct((m, n), BF16)],
        scratch_shapes=[pltpu.VMEM((k, n), BF16)],
        compiler_params=pltpu.CompilerParams(dimension_semantics=("arbitrary",)),
        name="matmul_res_norm",
    )(x, w, residual, norm_g.reshape(1, n))


def _norm2_perm_kernel(x_ref, ga_ref, gb_ref, p_ref, a_ref, b_ref, bp_ref):
    x = x_ref[...]
    ms = jnp.mean(x * x, axis=-1, keepdims=True)
    y = x * lax.rsqrt(ms + EPS)
    a_ref[...] = (y * ga_ref[...]).astype(BF16)
    b = (y * gb_ref[...]).astype(BF16)
    b_ref[...] = b
    bp_ref[...] = jnp.dot(p_ref[...], b, preferred_element_type=F32).astype(BF16)


def _rmsnorm2_perm(x, g_a, g_b, perm):
    m, d = x.shape
    tm = PERM_TILE
    row = pl.BlockSpec((tm, d), lambda i: (i, 0))
    vec = pl.BlockSpec((1, d), lambda i: (0, 0))
    out = jax.ShapeDtypeStruct((m, d), BF16)
    return pl.pallas_call(
        _norm2_perm_kernel,
        grid=(m // tm,),
        in_specs=[row, vec, vec, pl.BlockSpec((tm, tm), lambda i: (0, 0))],
        out_specs=[row, row, row],
        out_shape=[out, out, out],
        compiler_params=pltpu.CompilerParams(dimension_semantics=("arbitrary",)),
        name="rmsnorm2_perm",
    )(x, g_a.reshape(1, d), g_b.reshape(1, d), perm)


def _norm_gate_in_kernel(x_ref, g_ref, w1_ref, xn_ref, a_ref):
    x = x_ref[...]
    ms = jnp.mean(x * x, axis=-1, keepdims=True)
    xn = ((x * lax.rsqrt(ms + EPS)) * g_ref[...]).astype(BF16)
    xn_ref[...] = xn
    a_ref[...] = jnp.dot(xn, w1_ref[...], preferred_element_type=F32).astype(BF16)


def _rmsnorm_gate_in(x, g, w_a1, tm=512):
    m, d = x.shape
    return pl.pallas_call(
        _norm_gate_in_kernel,
        grid=(m // tm,),
        in_specs=[pl.BlockSpec((tm, d), lambda i: (i, 0)),
                  pl.BlockSpec((1, d), lambda i: (0, 0)),
                  pl.BlockSpec((d, LANES), lambda i: (0, 0))],
        out_specs=[pl.BlockSpec((tm, d), lambda i: (i, 0)),
                   pl.BlockSpec((tm, LANES), lambda i: (i, 0))],
        out_shape=[jax.ShapeDtypeStruct((m, d), BF16), jax.ShapeDtypeStruct((m, LANES), BF16)],
        compiler_params=pltpu.CompilerParams(dimension_semantics=("arbitrary",)),
        name="rmsnorm_gate_in",
    )(x, g.reshape(1, d), w_a1)


def _split3(x):
    hi = x.astype(BF16)
    r1 = x - hi.astype(F32)
    mid = r1.astype(BF16)
    lo = (r1 - mid.astype(F32)).astype(BF16)
    return hi, mid, lo


def _dot01(mat01, x):
    hi, mid, lo = _split3(x)
    return (jnp.dot(mat01, hi, preferred_element_type=F32)
            + jnp.dot(mat01, mid, preferred_element_type=F32)
            + jnp.dot(mat01, lo, preferred_element_type=F32))


def _gla_kernel(q_ref, k_ref, v_ref, r_ref, a_ref, w2_ref, b2_ref, hn_ref, o_ref, st_ref):
    @pl.when(pl.program_id(2) == 0)
    def _():
        st_ref[...] = jnp.zeros_like(st_ref)

    nb = GLA_BLOCK
    heads = range(GLA_HEADS_PER_STEP)
    dk = lambda h: slice(h * GLA_DK, (h + 1) * GLA_DK)
    dv = lambda h: slice(h * GLA_DV, (h + 1) * GLA_DV)
    row = lax.broadcasted_iota(jnp.int32, (nb, nb), 0)
    col = lax.broadcasted_iota(jnp.int32, (nb, nb), 1)
    same_chunk = (row // GLA_CHUNK) == (col // GLA_CHUNK)
    causal = same_chunk & (col <= row)
    tri = jnp.where(causal, 1.0, 0.0).astype(BF16)
    nt = (((1,), (1,)), ((), ()))
    tn = (((0,), (0,)), ((), ()))

    a = a_ref[...]
    cum = []
    for h in heads:
        z = jnp.dot(a, w2_ref[:, dk(h)], preferred_element_type=F32) + b2_ref[:, dk(h)]
        log_sig = jnp.minimum(z, 0.0) - jnp.log(1.0 + jnp.exp(-jnp.abs(z)))
        log_alpha = log_sig * (1.0 / GATE_NORMALIZER)
        cum.append(_dot01(tri, log_alpha))
    k = [k_ref[:, dk(h)].astype(F32) for h in heads]
    q_dec = [((q_ref[:, dk(h)].astype(F32) * (GLA_DK ** -0.5)) * jnp.exp(cum[h])).astype(BF16)
             for h in heads]
    k_inv = [(k[h] * jnp.exp(-cum[h])).astype(BF16) for h in heads]
    scores = [lax.dot_general(q_dec[h], k_inv[h], nt, preferred_element_type=F32)
              for h in heads]
    scores = [jnp.where(causal, s, 0.0).astype(BF16) for s in scores]
    o_intra = [jnp.dot(scores[h], v_ref[:, dv(h)], preferred_element_type=F32)
               for h in heads]

    o_parts = [[] for _ in heads]
    for c in range(nb // GLA_CHUNK):
        rows = slice(c * GLA_CHUNK, (c + 1) * GLA_CHUNK)
        for h in heads:
            last_c = cum[h][(c + 1) * GLA_CHUNK - 1:(c + 1) * GLA_CHUNK]
            k_end = (k[h][rows] * jnp.exp(last_c - cum[h][rows])).astype(BF16)
            st = st_ref[h]
            o_inter = lax.dot_general(q_dec[h][rows], st.astype(BF16), nt,
                                      preferred_element_type=F32)
            upd = lax.dot_general(v_ref[rows, dv(h)], k_end, tn,
                                  preferred_element_type=F32)
            st_ref[h] = st * jnp.exp(last_c) + upd
            o_parts[h].append(o_intra[h][rows] + o_inter)

    for h in heads:
        o = jnp.concatenate(o_parts[h], axis=0)
        ms = jnp.mean(o * o, axis=-1, keepdims=True)
        o = (o * lax.rsqrt(ms + EPS)) * hn_ref[...]
        r = r_ref[:, dv(h)].astype(F32)
        gate = r * (1.0 / (1.0 + jnp.exp(-r)))
        o_ref[:, dv(h)] = (o * gate).astype(o_ref.dtype)


def _gla_recurrence(proj, a, w_a2, b_a2, head_norm, bsz, s_len):
    t = bsz * s_len
    nblk = s_len // GLA_BLOCK
    hps = GLA_HEADS_PER_STEP
    wk, wv = hps * GLA_DK, hps * GLA_DV
    k_off = GLA_KEY_DIM // wk
    v_off = 2 * GLA_KEY_DIM // wv
    r_off = (2 * GLA_KEY_DIM + GLA_VAL_DIM) // wv
    row = lambda b, g, i: b * nblk + i
    return pl.pallas_call(
        _gla_kernel,
        grid=(bsz, GLA_HEADS // hps, nblk),
        in_specs=[
            pl.BlockSpec((GLA_BLOCK, wk), lambda b, g, i: (row(b, g, i), g)),
            pl.BlockSpec((GLA_BLOCK, wk), lambda b, g, i: (row(b, g, i), k_off + g)),
            pl.BlockSpec((GLA_BLOCK, wv), lambda b, g, i: (row(b, g, i), v_off + g)),
            pl.BlockSpec((GLA_BLOCK, wv), lambda b, g, i: (row(b, g, i), r_off + g)),
            pl.BlockSpec((GLA_BLOCK, LANES), lambda b, g, i: (row(b, g, i), 0)),
            pl.BlockSpec((LANES, wk), lambda b, g, i: (0, g)),
            pl.BlockSpec((1, wk), lambda b, g, i: (0, g)),
            pl.BlockSpec((1, GLA_DV), lambda b, g, i: (0, 0)),
        ],
        out_specs=pl.BlockSpec((GLA_BLOCK, wv), lambda b, g, i: (row(b, g, i), g)),
        out_shape=jax.ShapeDtypeStruct((t, GLA_VAL_DIM), BF16),
        scratch_shapes=[pltpu.VMEM((hps, GLA_DV, GLA_DK), F32)],
        compiler_params=pltpu.CompilerParams(
            dimension_semantics=("arbitrary", "arbitrary", "arbitrary")),
        name="gla_recurrence",
    )(proj, proj, proj, proj, a, w_a2, b_a2.reshape(1, GLA_KEY_DIM), head_norm.reshape(1, GLA_DV))


def _perm_matrix():
    p = np.zeros((PERM_TILE, PERM_TILE), np.float32)
    for i in range(PERM_RUN):
        for a in range(4):
            for r4 in range(4):
                p[(4 * r4 + a) * PERM_RUN + i, 16 * i + 4 * a + r4] = 1.0
    return p


def _dsa_lead(layout, s):
    if layout == "rows":
        return (slice(s * ATT_BLOCK, (s + 1) * ATT_BLOCK),)
    if layout == "tile":
        return (s, slice(None), slice(None))
    assert layout == "tiles4"
    return (slice(4 * s, 4 * s + 4), slice(None))


def _dsa_block_shape(layout, width):
    return (ATT_BLOCK, width) if layout == "rows" else (4, PERM_RUN, width)


def _dsa_kernel(q_ref, kc_ref, vc_ref, o_ref, lse_ref, kp_ref, vp_ref, *, dil, layout, qb,
                slopes):
    n = pl.program_id(2)

    @pl.when(n == 0)
    def _():
        kp_ref[...] = jnp.zeros_like(kp_ref)
        vp_ref[...] = jnp.zeros_like(vp_ref)

    nk = 2 * ATT_BLOCK
    rq = lax.broadcasted_iota(jnp.int32, (ATT_BLOCK, nk), 0)
    ck = lax.broadcasted_iota(jnp.int32, (ATT_BLOCK, nk), 1)
    ck_in = ck & (ATT_BLOCK - 1)
    if layout == "tile":
        sub_q = 4 * (rq & (PERM_RUN - 1)) + (rq >> 5)
        sub_k = 4 * (ck_in & (PERM_RUN - 1)) + (ck_in >> 5)
    else:
        sub_q, sub_k = rq, ck_in
    is_cur = ck >= ATT_BLOCK
    j = sub_q - sub_k + jnp.where(is_cur, 0, ATT_BLOCK)
    valid = (j >= 0) & (j <= ATT_BLOCK)
    neg_dist_rest = jnp.where(valid, -(j * dil).astype(F32), -jnp.inf)
    neg_dist_first = jnp.where(is_cur | (n > 0), neg_dist_rest, -jnp.inf)

    lane = lax.broadcasted_iota(jnp.int32, (ATT_BLOCK, LANES), 1)
    ones = jnp.ones((nk, HEAD_DIM), BF16)
    nt = (((1,), (1,)), ((), ()))

    def head(ref, s, h):
        x = ref[_dsa_lead(layout, s) + (slice(h * HEAD_DIM, (h + 1) * HEAD_DIM),)]
        return x.reshape(ATT_BLOCK, HEAD_DIM)

    def prev_head(ref, carry_ref, s, h):
        if s == 0:
            return carry_ref[:, h * HEAD_DIM:(h + 1) * HEAD_DIM]
        return head(ref, s - 1, h)

    for s in range(qb):
        neg_dist = neg_dist_first if s == 0 else neg_dist_rest
        lead = _dsa_lead(layout, s)

        def scores(h, s=s):
            kcat = jnp.concatenate([prev_head(kc_ref, kp_ref, s, h), head(kc_ref, s, h)], axis=0)
            return lax.dot_general(head(q_ref, s, h), kcat, nt, preferred_element_type=F32)

        pending = [scores(h) for h in range(QK_LOOKAHEAD)]
        lse_tile = jnp.zeros((ATT_BLOCK, LANES), F32)
        for h in range(ATT_HEADS):
            sc = pending.pop(0)
            if h + QK_LOOKAHEAD < ATT_HEADS:
                pending.append(scores(h + QK_LOOKAHEAD))
            z = sc + (slopes[h] * LOG2_E) * neg_dist
            m = jnp.max(z, axis=-1, keepdims=True)
            p = jnp.exp2(z - m).astype(BF16)
            vcat = jnp.concatenate([prev_head(vc_ref, vp_ref, s, h), head(vc_ref, s, h)], axis=0)
            acc = jnp.dot(p, jnp.concatenate([vcat, ones], axis=1),
                          preferred_element_type=F32)
            l = acc[:, HEAD_DIM:]
            o = (acc[:, :HEAD_DIM] / l).astype(o_ref.dtype)
            o_ref[lead + (slice(h * HEAD_DIM, (h + 1) * HEAD_DIM),)] = o.reshape(
                _dsa_block_shape(layout, HEAD_DIM))
            lse_tile = jnp.where(lane == h, m * LN_2 + jnp.log(l), lse_tile)
        lse_ref[lead + (slice(None),)] = lse_tile.reshape(_dsa_block_shape(layout, LANES))

    last = _dsa_lead(layout, qb - 1) + (slice(None),)
    kp_ref[...] = kc_ref[last].reshape(kp_ref.shape)
    vp_ref[...] = vc_ref[last].reshape(vp_ref.shape)


def _dsa_branch(q, q_col, kv, g, bsz, s_len):
    d = DILATIONS[g]
    assert WINDOWS[g] // d == ATT_BLOCK
    t = bsz * s_len
    ntile = s_len // PERM_TILE
    qb = min(DSA_BLOCKS_PER_STEP, s_len // d // ATT_BLOCK)
    if g == 0:
        layout = "rows"
        view = lambda x: x.reshape(bsz, s_len, x.shape[-1])
        blk = lambda w: (None, qb * ATT_BLOCK, w)
        grid = (bsz, 1, s_len // (qb * ATT_BLOCK))
        at = lambda c: (lambda b, r, n: (b, n, c))
    elif g == 1:
        layout = "tile"
        view = lambda x: x.reshape(bsz, ntile, 4, 4, PERM_RUN, x.shape[-1])
        blk = lambda w: (None, qb, None, 4, PERM_RUN, w)
        grid = (bsz, 4, ntile // qb)
        at = lambda c: (lambda b, r, n: (b, n, r, 0, 0, c))
    else:
        layout = "tiles4"
        view = lambda x: x.reshape(bsz, ntile, 16, PERM_RUN, x.shape[-1])
        blk = lambda w: (None, 4 * qb, None, PERM_RUN, w)
        grid = (bsz, 16, ntile // (4 * qb))
        at = lambda c: (lambda b, r, n: (b, n, r, 0, c))
    qv, kvv = view(q), view(kv)
    o, lse = pl.pallas_call(
        functools.partial(_dsa_kernel, dil=d, layout=layout, qb=qb,
                          slopes=_alibi_slopes(ATT_HEADS)),
        grid=grid,
        in_specs=[
            pl.BlockSpec(blk(ATT_WIDTH), at(q_col)),
            pl.BlockSpec(blk(ATT_WIDTH), at(0)),
            pl.BlockSpec(blk(ATT_WIDTH), at(1)),
        ],
        out_specs=[pl.BlockSpec(blk(ATT_WIDTH), at(0)),
                   pl.BlockSpec(blk(LANES), at(0))],
        out_shape=[jax.ShapeDtypeStruct(qv.shape[:-1] + (ATT_WIDTH,), BF16),
                   jax.ShapeDtypeStruct(qv.shape[:-1] + (LANES,), F32)],
        scratch_shapes=[pltpu.VMEM((ATT_BLOCK, ATT_WIDTH), BF16),
                        pltpu.VMEM((ATT_BLOCK, ATT_WIDTH), BF16)],
        compiler_params=pltpu.CompilerParams(
            dimension_semantics=("arbitrary", "arbitrary", "arbitrary")),
        name=f"dsa_branch{g}",
    )(qv, kvv, kvv)
    return o.reshape(t, ATT_WIDTH), lse.reshape(t, LANES)


def _merge_kernel(pt_ref, o0_ref, o1_ref, o2_ref, l0_ref, l1_ref, l2_ref, o_ref):
    pt = pt_ref[...]
    l0 = l0_ref[...]
    l12 = _dot01(pt, jnp.concatenate([l1_ref[...], l2_ref[...]], axis=1))
    l1, l2 = l12[:, :LANES], l12[:, LANES:]
    m = jnp.maximum(jnp.maximum(l0, l1), l2)
    e0, e1, e2 = jnp.exp(l0 - m), jnp.exp(l1 - m), jnp.exp(l2 - m)
    den = e0 + e1 + e2
    w1, w2 = e1 / den, e2 / den
    for hp in range(ATT_HEADS // 2):
        pair = slice(2 * hp * HEAD_DIM, (2 * hp + 2) * HEAD_DIM)
        o1 = jnp.dot(pt, o1_ref[:, pair], preferred_element_type=F32)
        o2 = jnp.dot(pt, o2_ref[:, pair], preferred_element_type=F32)
        for k in range(2):
            h = 2 * hp + k
            sl = slice(h * HEAD_DIM, (h + 1) * HEAD_DIM)
            in_pair = slice(k * HEAD_DIM, (k + 1) * HEAD_DIM)
            o0 = o0_ref[:, sl].astype(F32)
            o = (o0 + w1[:, h:h + 1] * (o1[:, in_pair] - o0)
                 + w2[:, h:h + 1] * (o2[:, in_pair] - o0))
            o_ref[:, sl] = o.astype(o_ref.dtype)


def _dsa_merge(outs, lses, perm_t):
    t, hd = outs[0].shape
    tm = PERM_TILE
    o_spec = pl.BlockSpec((tm, hd), lambda i: (i, 0))
    l_spec = pl.BlockSpec((tm, LANES), lambda i: (i, 0))
    return pl.pallas_call(
        _merge_kernel,
        grid=(t // tm,),
        in_specs=[pl.BlockSpec((tm, tm), lambda i: (0, 0)),
                  o_spec, o_spec, o_spec, l_spec, l_spec, l_spec],
        out_specs=o_spec,
        out_shape=jax.ShapeDtypeStruct((t, hd), BF16),
        compiler_params=pltpu.CompilerParams(dimension_semantics=("arbitrary",)),
        name="dsa_merge",
    )(perm_t, *outs, *lses)


def _ffn_up_kernel(x_ref, wu_ref, wg_ref, cw_ref, cb_ref, o_ref, wub_ref, wgb_ref,
                   us_ref, gs_ref, carry_ref, *, tm, tn, s_len):
    @pl.when(pl.program_id(1) == 0)
    def _():
        wub_ref[...] = wu_ref[...].astype(BF16)
        wgb_ref[...] = wg_ref[...].astype(BF16)

    @pl.when((pl.program_id(1) * tm) % s_len == 0)
    def _():
        carry_ref[...] = jnp.zeros_like(carry_ref)

    n_rb = tm // FFN_ROW_SUB
    subs = [(cb, rb) for cb in range(tn // FFN_COL_SUB) for rb in range(n_rb)]

    def window(idx):
        cb, rb = subs[idx]
        return (slice(rb * FFN_ROW_SUB, (rb + 1) * FFN_ROW_SUB),
                slice(cb * FFN_COL_SUB, (cb + 1) * FFN_COL_SUB))

    def project(idx):
        rows, cols = window(idx)
        slot = idx % FFN_SLOTS
        xr = x_ref[rows, :]
        us_ref[slot] = jnp.dot(xr, wub_ref[:, cols], preferred_element_type=F32)
        gs_ref[slot, 8:] = jnp.dot(xr, wgb_ref[:, cols], preferred_element_type=F32)
        if subs[idx][1] == 0:
            gs_ref[slot, :8] = carry_ref[:, cols]
        else:
            gs_ref[slot, :8] = gs_ref[(idx - 1) % FFN_SLOTS, FFN_ROW_SUB:]
        if subs[idx][1] == n_rb - 1:
            carry_ref[:, cols] = gs_ref[slot, FFN_ROW_SUB:]

    def epilogue(idx):
        rows, cols = window(idx)
        slot = idx % FFN_SLOTS
        g = gs_ref[slot, 8:]
        g1 = gs_ref[slot, 7:7 + FFN_ROW_SUB]
        g2 = gs_ref[slot, 6:6 + FFN_ROW_SUB]
        cw = cw_ref[:, cols]
        conv = cw[0:1] * g2 + cw[1:2] * g1 + cw[2:3] * g + cb_ref[:, cols]
        act = 0.5 * conv * (1.0 + lax.erf(conv * (2.0 ** -0.5)))
        o_ref[rows, cols] = (act * us_ref[slot]).astype(o_ref.dtype)

    ahead = FFN_SLOTS - 1
    for idx in range(ahead):
        project(idx)
    for idx in range(len(subs)):
        if idx + ahead < len(subs):
            project(idx + ahead)
        epilogue(idx)


def _ffn_up(xn, w_up, layer, conv_w, conv_b, s_len, tm=2048, tn=512):
    m, k = xn.shape
    nn = D_FF // tn
    assert s_len % tm == 0 and tm % FFN_ROW_SUB == 0 and tn % FFN_COL_SUB == 0
    return pl.pallas_call(
        functools.partial(_ffn_up_kernel, tm=tm, tn=tn, s_len=s_len),
        grid=(nn, m // tm),
        in_specs=[
            pl.BlockSpec((tm, k), lambda j, i: (i, 0)),
            pl.BlockSpec((None, k, tn), lambda j, i: (layer, 0, j)),
            pl.BlockSpec((None, k, tn), lambda j, i: (layer, 0, nn + j)),
            pl.BlockSpec((3, tn), lambda j, i: (0, j)),
            pl.BlockSpec((1, tn), lambda j, i: (0, j)),
        ],
        out_specs=pl.BlockSpec((tm, tn), lambda j, i: (i, j)),
        out_shape=jax.ShapeDtypeStruct((m, D_FF), BF16),
        scratch_shapes=[pltpu.VMEM((k, tn), BF16), pltpu.VMEM((k, tn), BF16),
                        pltpu.VMEM((FFN_SLOTS, FFN_ROW_SUB, FFN_COL_SUB), F32),
                        pltpu.VMEM((FFN_SLOTS, 8 + FFN_ROW_SUB, FFN_COL_SUB), F32),
                        pltpu.VMEM((8, tn), F32)],
        compiler_params=pltpu.CompilerParams(
            dimension_semantics=("arbitrary", "arbitrary")),
        name="ffn_up",
    )(xn, w_up, w_up, conv_w, conv_b.reshape(1, D_FF))


def _conv_glu(h, xn, w_up, layer, conv_w, conv_b, w_down, s_len):
    act = _ffn_up(xn, w_up, layer, conv_w, conv_b, s_len)
    return _matmul(act, w_down, layer, F32, residual=h, tn=512)


def kernel(x, attn_norm, gla_w_in, gla_w_a2, gla_b_a2, gla_head_norm, gla_w_out, kv_norm, w_kv,
           dsa_w_q, dsa_w_out, ffn_norm, ffn_w_up, ffn_conv_w, ffn_conv_b, ffn_w_down, final_norm):
    bsz, s_len, d = x.shape
    t = bsz * s_len
    h = x.reshape(t, d)

    n_main = 2 * GLA_KEY_DIM + 2 * GLA_VAL_DIM
    w_a1 = jnp.pad(gla_w_in[0, :, n_main:].astype(BF16), ((0, 0), (0, LANES - GATE_RANK)))
    w_a2 = jnp.pad(gla_w_a2[0].astype(BF16), ((0, LANES - GATE_RANK), (0, 0)))
    xn, a = _rmsnorm_gate_in(h, attn_norm[0], w_a1)
    proj = _matmul(xn, gla_w_in, 0, BF16, n_cols=n_main, tm=1024)
    o = _gla_recurrence(proj, a, w_a2, gla_b_a2[0], gla_head_norm[0], bsz, s_len)
    h, xn = _matmul_res_norm(o, gla_w_out, 0, h, ffn_norm[0])
    h = _conv_glu(h, xn, ffn_w_up, 0, ffn_conv_w[0], ffn_conv_b[0], ffn_w_down, s_len)

    perm_np = _perm_matrix()
    perm = jnp.asarray(perm_np, BF16)
    perm_t = jnp.asarray(perm_np.T, BF16)
    xkv, xq, xq_perm = _rmsnorm2_perm(h, kv_norm, attn_norm[1], perm)
    kv, kv_perm = _matmul_and_permuted(xkv, w_kv[None], 0, perm)

    q_scale = HEAD_DIM ** -0.5 * LOG2_E
    q0 = _matmul(xq, dsa_w_q, 0, BF16, n_cols=ATT_WIDTH, out_scale=q_scale, tm=1024)
    q12 = _matmul(xq_perm, dsa_w_q, 0, BF16, n_cols=2 * ATT_WIDTH, col_start=ATT_WIDTH,
                  out_scale=q_scale, tm=1024)
    o0, lse0 = _dsa_branch(q0, 0, kv, 0, bsz, s_len)
    o1, lse1 = _dsa_branch(q12, 0, kv_perm, 1, bsz, s_len)
    o2, lse2 = _dsa_branch(q12, 1, kv_perm, 2, bsz, s_len)
    o = _dsa_merge([o0, o1, o2], [lse0, lse1, lse2], perm_t)
    h, xn = _matmul_res_norm(o, dsa_w_out, 0, h, ffn_norm[1])
    h = _conv_glu(h, xn, ffn_w_up, 1, ffn_conv_w[1], ffn_conv_b[1], ffn_w_down, s_len)

    return _rmsnorm(h, final_norm, F32).reshape(bsz, s_len, d)
```

```python
import functools
import math

import numpy as np
import jax
import jax.numpy as jnp
from jax import lax
from jax.experimental import pallas as pl
from jax.experimental.pallas import tpu as pltpu

D_MODEL = 2048
GLA_HEADS = 4
GLA_KEY_DIM = 1024
GLA_VAL_DIM = 2048
GLA_DK = 256
GLA_DV = 512
GATE_RANK = 16
GATE_NORMALIZER = 16.0
GLA_CHUNK = 64
GLA_BLOCK = 256
GLA_HEADS_PER_STEP = 4
ATT_HEADS = 16
HEAD_DIM = 128
ATT_WIDTH = ATT_HEADS * HEAD_DIM
WINDOWS = (128, 512, 2048)
DILATIONS = (1, 4, 16)
ATT_BLOCK = 128
DSA_BLOCKS_PER_STEP = 4
QK_LOOKAHEAD = 2
PERM_TILE = 512
PERM_RUN = PERM_TILE // 16
D_FF = 5632
EPS = 1e-6
LOG2_E = math.log2(math.e)
LN_2 = math.log(2.0)
LANES = 128
FFN_ROW_SUB = 1024
FFN_COL_SUB = 256
FFN_SLOTS = 2

F32 = jnp.float32
BF16 = jnp.bfloat16


def _alibi_slopes(n):
    def pow2_slopes(m):
        start = 2.0 ** (-8.0 / m)
        return [start ** (i + 1) for i in range(m)]
    assert math.log2(n).is_integer()
    return [float(v) for v in np.array(pow2_slopes(n), dtype=np.float32)]


def _rmsnorm_kernel(x_ref, g_ref, o_ref):
    x = x_ref[...]
    ms = jnp.mean(x * x, axis=-1, keepdims=True)
    o_ref[...] = ((x * lax.rsqrt(ms + EPS)) * g_ref[...]).astype(o_ref.dtype)


def _rmsnorm(x, g, out_dtype, tm=512):
    m, d = x.shape
    return pl.pallas_call(
        _rmsnorm_kernel,
        grid=(m // tm,),
        in_specs=[pl.BlockSpec((tm, d), lambda i: (i, 0)),
                  pl.BlockSpec((1, d), lambda i: (0, 0))],
        out_specs=pl.BlockSpec((tm, d), lambda i: (i, 0)),
        out_shape=jax.ShapeDtypeStruct((m, d), out_dtype),
        compiler_params=pltpu.CompilerParams(dimension_semantics=("arbitrary",)),
        name="rmsnorm",
    )(x, g.reshape(1, d))


def _cast_weights_once(w_ref, wb_ref):
    @pl.when(pl.program_id(1) == 0)
    def _():
        wb_ref[...] = w_ref[...].astype(BF16)


def _mm_kernel(x_ref, w_ref, *rest, out_scale):
    o_ref, wb_ref = rest[-2:]
    _cast_weights_once(w_ref, wb_ref)
    acc = jnp.dot(x_ref[...], wb_ref[...], preferred_element_type=F32)
    if len(rest) == 3:
        acc = rest[0][...] + acc
    if out_scale is not None:
        acc = acc * out_scale
    o_ref[...] = acc.astype(o_ref.dtype)


def _matmul(x, w, layer, out_dtype, n_cols=None, col_start=0, residual=None, out_scale=None,
            tm=512, tn=1024, weight_buffers=2):
    m, k = x.shape
    n = w.shape[2] if n_cols is None else n_cols
    assert w.shape[1] == k and m % tm == 0 and n % tn == 0 and col_start % tn == 0
    j0 = col_start // tn
    in_specs = [pl.BlockSpec((tm, k), lambda j, i: (i, 0)),
                pl.BlockSpec((None, k, tn), lambda j, i: (layer, 0, j0 + j),
                             pipeline_mode=pl.Buffered(weight_buffers))]
    args = [x, w]
    if residual is not None:
        in_specs.append(pl.BlockSpec((tm, tn), lambda j, i: (i, j)))
        args.append(residual)
    return pl.pallas_call(
        functools.partial(_mm_kernel, out_scale=out_scale),
        grid=(n // tn, m // tm),
        in_specs=in_specs,
        out_specs=pl.BlockSpec((tm, tn), lambda j, i: (i, j)),
        out_shape=jax.ShapeDtypeStruct((m, n), out_dtype),
        scratch_shapes=[pltpu.VMEM((k, tn), BF16)],
        compiler_params=pltpu.CompilerParams(
            dimension_semantics=("arbitrary", "arbitrary")),
        name="matmul",
    )(*args)


def _mm_perm_kernel(x_ref, w_ref, p_ref, o_ref, op_ref, wb_ref):
    _cast_weights_once(w_ref, wb_ref)
    out = jnp.dot(x_ref[...], wb_ref[...], preferred_element_type=F32).astype(BF16)
    o_ref[...] = out
    op_ref[...] = jnp.dot(p_ref[...], out, preferred_element_type=F32).astype(BF16)


def _matmul_and_permuted(x, w, layer, perm, tn=1024):
    m, k = x.shape
    n = w.shape[2]
    tm = PERM_TILE
    assert w.shape[1] == k and m % tm == 0 and n % tn == 0
    out_spec = pl.BlockSpec((tm, tn), lambda j, i: (i, j))
    out = jax.ShapeDtypeStruct((m, n), BF16)
    return pl.pallas_call(
        _mm_perm_kernel,
        grid=(n // tn, m // tm),
        in_specs=[pl.BlockSpec((tm, k), lambda j, i: (i, 0)),
                  pl.BlockSpec((None, k, tn), lambda j, i: (layer, 0, j)),
                  pl.BlockSpec((tm, tm), lambda j, i: (0, 0))],
        out_specs=[out_spec, out_spec],
        out_shape=[out, out],
        scratch_shapes=[pltpu.VMEM((k, tn), BF16)],
        compiler_params=pltpu.CompilerParams(
            dimension_semantics=("arbitrary", "arbitrary")),
        name="matmul_and_permuted",
    )(x, w, perm)


def _mm_res_norm_kernel(x_ref, w_ref, r_ref, g_ref, h_ref, xn_ref, wb_ref):
    @pl.when(pl.program_id(0) == 0)
    def _():
        wb_ref[...] = w_ref[...].astype(BF16)

    h = r_ref[...] + jnp.dot(x_ref[...], wb_ref[...], preferred_element_type=F32)
    h_ref[...] = h
    ms = jnp.mean(h * h, axis=-1, keepdims=True)
    xn_ref[...] = ((h * lax.rsqrt(ms + EPS)) * g_ref[...]).astype(xn_ref.dtype)


def _matmul_res_norm(x, w, layer, residual, norm_g, tm=512):
    m, k = x.shape
    n = w.shape[2]
    assert w.shape[1] == k and residual.shape == (m, n) and m % tm == 0
    row = pl.BlockSpec((tm, n), lambda i: (i, 0))
    return pl.pallas_call(
        _mm_res_norm_kernel,
        grid=(m // tm,),
        in_specs=[pl.BlockSpec((tm, k), lambda i: (i, 0)),
                  pl.BlockSpec((None, k, n), lambda i: (layer, 0, 0),
                               pipeline_mode=pl.Buffered(1)),
                  row,
                  pl.BlockSpec((1, n), lambda i: (0, 0))],
        out_specs=[row, row],
        out_shape=[jax.ShapeDtypeStruct((m, n), F32), jax.ShapeDtypeStruct((m, n), BF16)],
        scratch_shapes=[pltpu.VMEM((k, n), BF16)],
        compiler_params=pltpu.CompilerParams(dimension_semantics=("arbitrary",)),
        name="matmul_res_norm",
    )(x, w, residual, norm_g.reshape(1, n))


def _norm2_perm_kernel(x_ref, ga_ref, gb_ref, p_ref, a_ref, b_ref, bp_ref):
    x = x_ref[...]
    ms = jnp.mean(x * x, axis=-1, keepdims=True)
    y = x * lax.rsqrt(ms + EPS)
    a_ref[...] = (y * ga_ref[...]).astype(BF16)
    b = (y * gb_ref[...]).astype(BF16)
    b_ref[...] = b
    bp_ref[...] = jnp.dot(p_ref[...], b, preferred_element_type=F32).astype(BF16)


def _rmsnorm2_perm(x, g_a, g_b, perm):
    m, d = x.shape
    tm = PERM_TILE
    row = pl.BlockSpec((tm, d), lambda i: (i, 0))
    vec = pl.BlockSpec((1, d), lambda i: (0, 0))
    out = jax.ShapeDtypeStruct((m, d), BF16)
    return pl.pallas_call(
        _norm2_perm_kernel,
        grid=(m // tm,),
        in_specs=[row, vec, vec, pl.BlockSpec((tm, tm), lambda i: (0, 0))],
        out_specs=[row, row, row],
        out_shape=[out, out, out],
        compiler_params=pltpu.CompilerParams(dimension_semantics=("arbitrary",)),
        name="rmsnorm2_perm",
    )(x, g_a.reshape(1, d), g_b.reshape(1, d), perm)


def _norm_gate_in_kernel(x_ref, g_ref, w1_ref, xn_ref, a_ref):
    x = x_ref[...]
    ms = jnp.mean(x * x, axis=-1, keepdims=True)
    xn = ((x * lax.rsqrt(ms + EPS)) * g_ref[...]).astype(BF16)
    xn_ref[...] = xn
    a_ref[...] = jnp.dot(xn, w1_ref[...], preferred_element_type=F32).astype(BF16)


def _rmsnorm_gate_in(x, g, w_a1, tm=512):
    m, d = x.shape
    return pl.pallas_call(
        _norm_gate_in_kernel,
        grid=(m // tm,),
        in_specs=[pl.BlockSpec((tm, d), lambda i: (i, 0)),
                  pl.BlockSpec((1, d), lambda i: (0, 0)),
                  pl.BlockSpec((d, LANES), lambda i: (0, 0))],
        out_specs=[pl.BlockSpec((tm, d), lambda i: (i, 0)),
                   pl.BlockSpec((tm, LANES), lambda i: (i, 0))],
        out_shape=[jax.ShapeDtypeStruct((m, d), BF16), jax.ShapeDtypeStruct((m, LANES), BF16)],
        compiler_params=pltpu.CompilerParams(dimension_semantics=("arbitrary",)),
        name="rmsnorm_gate_in",
    )(x, g.reshape(1, d), w_a1)


def _split3(x):
    hi = x.astype(BF16)
    r1 = x - hi.astype(F32)
    mid = r1.astype(BF16)
    lo = (r1 - mid.astype(F32)).astype(BF16)
    return hi, mid, lo


def _dot01(mat01, x):
    hi, mid, lo = _split3(x)
    return (jnp.dot(mat01, hi, preferred_element_type=F32)
            + jnp.dot(mat01, mid, preferred_element_type=F32)
            + jnp.dot(mat01, lo, preferred_element_type=F32))


def _gla_kernel(q_ref, k_ref, v_ref, r_ref, a_ref, w2_ref, b2_ref, hn_ref, o_ref, st_ref):
    @pl.when(pl.program_id(2) == 0)
    def _():
        st_ref[...] = jnp.zeros_like(st_ref)

    nb = GLA_BLOCK
    heads = range(GLA_HEADS_PER_STEP)
    dk = lambda h: slice(h * GLA_DK, (h + 1) * GLA_DK)
    dv = lambda h: slice(h * GLA_DV, (h + 1) * GLA_DV)
    row = lax.broadcasted_iota(jnp.int32, (nb, nb), 0)
    col = lax.broadcasted_iota(jnp.int32, (nb, nb), 1)
    same_chunk = (row // GLA_CHUNK) == (col // GLA_CHUNK)
    causal = same_chunk & (col <= row)
    tri = jnp.where(causal, 1.0, 0.0).astype(BF16)
    nt = (((1,), (1,)), ((), ()))
    tn = (((0,), (0,)), ((), ()))

    a = a_ref[...]
    cum = []
    for h in heads:
        z = jnp.dot(a, w2_ref[:, dk(h)], preferred_element_type=F32) + b2_ref[:, dk(h)]
        log_sig = jnp.minimum(z, 0.0) - jnp.log(1.0 + jnp.exp(-jnp.abs(z)))
        log_alpha = log_sig * (1.0 / GATE_NORMALIZER)
        cum.append(_dot01(tri, log_alpha))
    k = [k_ref[:, dk(h)].astype(F32) for h in heads]
    q_dec = [((q_ref[:, dk(h)].astype(F32) * (GLA_DK ** -0.5)) * jnp.exp(cum[h])).astype(BF16)
             for h in heads]
    k_inv = [(k[h] * jnp.exp(-cum[h])).astype(BF16) for h in heads]
    scores = [lax.dot_general(q_dec[h], k_inv[h], nt, preferred_element_type=F32)
              for h in heads]
    scores = [jnp.where(causal, s, 0.0).astype(BF16) for s in scores]
    o_intra = [jnp.dot(scores[h], v_ref[:, dv(h)], preferred_element_type=F32)
               for h in heads]

    o_parts = [[] for _ in heads]
    for c in range(nb // GLA_CHUNK):
        rows = slice(c * GLA_CHUNK, (c + 1) * GLA_CHUNK)
        for h in heads:
            last_c = cum[h][(c + 1) * GLA_CHUNK - 1:(c + 1) * GLA_CHUNK]
            k_end = (k[h][rows] * jnp.exp(last_c - cum[h][rows])).astype(BF16)
            st = st_ref[h]
            o_inter = lax.dot_general(q_dec[h][rows], st.astype(BF16), nt,
                                      preferred_element_type=F32)
            upd = lax.dot_general(v_ref[rows, dv(h)], k_end, tn,
                                  preferred_element_type=F32)
            st_ref[h] = st * jnp.exp(last_c) + upd
            o_parts[h].append(o_intra[h][rows] + o_inter)

    for h in heads:
        o = jnp.concatenate(o_parts[h], axis=0)
        ms = jnp.mean(o * o, axis=-1, keepdims=True)
        o = (o * lax.rsqrt(ms + EPS)) * hn_ref[...]
        r = r_ref[:, dv(h)].astype(F32)
        gate = r * (1.0 / (1.0 + jnp.exp(-r)))
        o_ref[:, dv(h)] = (o * gate).astype(o_ref.dtype)


def _gla_recurrence(proj, a, w_a2, b_a2, head_norm, bsz, s_len):
    t = bsz * s_len
    nblk = s_len // GLA_BLOCK
    hps = GLA_HEADS_PER_STEP
    wk, wv = hps * GLA_DK, hps * GLA_DV
    k_off = GLA_KEY_DIM // wk
    v_off = 2 * GLA_KEY_DIM // wv
    r_off = (2 * GLA_KEY_DIM + GLA_VAL_DIM) // wv
    row = lambda b, g, i: b * nblk + i
    return pl.pallas_call(
        _gla_kernel,
        grid=(bsz, GLA_HEADS // hps, nblk),
        in_specs=[
            pl.BlockSpec((GLA_BLOCK, wk), lambda b, g, i: (row(b, g, i), g)),
            pl.BlockSpec((GLA_BLOCK, wk), lambda b, g, i: (row(b, g, i), k_off + g)),
            pl.BlockSpec((GLA_BLOCK, wv), lambda b, g, i: (row(b, g, i), v_off + g)),
            pl.BlockSpec((GLA_BLOCK, wv), lambda b, g, i: (row(b, g, i), r_off + g)),
            pl.BlockSpec((GLA_BLOCK, LANES), lambda b, g, i: (row(b, g, i), 0)),
            pl.BlockSpec((LANES, wk), lambda b, g, i: (0, g)),
            pl.BlockSpec((1, wk), lambda b, g, i: (0, g)),
            pl.BlockSpec((1, GLA_DV), lambda b, g, i: (0, 0)),
        ],
        out_specs=pl.BlockSpec((GLA_BLOCK, wv), lambda b, g, i: (row(b, g, i), g)),
        out_shape=jax.ShapeDtypeStruct((t, GLA_VAL_DIM), BF16),
        scratch_shapes=[pltpu.VMEM((hps, GLA_DV, GLA_DK), F32)],
        compiler_params=pltpu.CompilerParams(
            dimension_semantics=("arbitrary", "arbitrary", "arbitrary")),
        name="gla_recurrence",
    )(proj, proj, proj, proj, a, w_a2, b_a2.reshape(1, GLA_KEY_DIM), head_norm.reshape(1, GLA_DV))


def _perm_matrix():
    p = np.zeros((PERM_TILE, PERM_TILE), np.float32)
    for i in range(PERM_RUN):
        for a in range(4):
            for r4 in range(4):
                p[(4 * r4 + a) * PERM_RUN + i, 16 * i + 4 * a + r4] = 1.0
    return p


def _dsa_lead(layout, s):
    if layout == "rows":
        return (slice(s * ATT_BLOCK, (s + 1) * ATT_BLOCK),)
    if layout == "tile":
        return (s, slice(None), slice(None))
    assert layout == "tiles4"
    return (slice(4 * s, 4 * s + 4), slice(None))


def _dsa_block_shape(layout, width):
    return (ATT_BLOCK, width) if layout == "rows" else (4, PERM_RUN, width)


def _dsa_kernel(q_ref, kc_ref, vc_ref, o_ref, lse_ref, kp_ref, vp_ref, *, dil, layout, qb,
                slopes):
    n = pl.program_id(2)

    @pl.when(n == 0)
    def _():
        kp_ref[...] = jnp.zeros_like(kp_ref)
        vp_ref[...] = jnp.zeros_like(vp_ref)

    nk = 2 * ATT_BLOCK
    rq = lax.broadcasted_iota(jnp.int32, (ATT_BLOCK, nk), 0)
    ck = lax.broadcasted_iota(jnp.int32, (ATT_BLOCK, nk), 1)
    ck_in = ck & (ATT_BLOCK - 1)
    if layout == "tile":
        sub_q = 4 * (rq & (PERM_RUN - 1)) + (rq >> 5)
        sub_k = 4 * (ck_in & (PERM_RUN - 1)) + (ck_in >> 5)
    else:
        sub_q, sub_k = rq, ck_in
    is_cur = ck >= ATT_BLOCK
    j = sub_q - sub_k + jnp.where(is_cur, 0, ATT_BLOCK)
    valid = (j >= 0) & (j <= ATT_BLOCK)
    neg_dist_rest = jnp.where(valid, -(j * dil).astype(F32), -jnp.inf)
    neg_dist_first = jnp.where(is_cur | (n > 0), neg_dist_rest, -jnp.inf)

    lane = lax.broadcasted_iota(jnp.int32, (ATT_BLOCK, LANES), 1)
    ones = jnp.ones((nk, HEAD_DIM), BF16)
    nt = (((1,), (1,)), ((), ()))

    def head(ref, s, h):
        x = ref[_dsa_lead(layout, s) + (slice(h * HEAD_DIM, (h + 1) * HEAD_DIM),)]
        return x.reshape(ATT_BLOCK, HEAD_DIM)

    def prev_head(ref, carry_ref, s, h):
        if s == 0:
            return carry_ref[:, h * HEAD_DIM:(h + 1) * HEAD_DIM]
        return head(ref, s - 1, h)

    for s in range(qb):
        neg_dist = neg_dist_first if s == 0 else neg_dist_rest
        lead = _dsa_lead(layout, s)

        def scores(h, s=s):
            kcat = jnp.concatenate([prev_head(kc_ref, kp_ref, s, h), head(kc_ref, s, h)], axis=0)
            return lax.dot_general(head(q_ref, s, h), kcat, nt, preferred_element_type=F32)

        pending = [scores(h) for h in range(QK_LOOKAHEAD)]
        lse_tile = jnp.zeros((ATT_BLOCK, LANES), F32)
        for h in range(ATT_HEADS):
            sc = pending.pop(0)
            if h + QK_LOOKAHEAD < ATT_HEADS:
                pending.append(scores(h + QK_LOOKAHEAD))
            z = sc + (slopes[h] * LOG2_E) * neg_dist
            m = jnp.max(z, axis=-1, keepdims=True)
            p = jnp.exp2(z - m).astype(BF16)
            vcat = jnp.concatenate([prev_head(vc_ref, vp_ref, s, h), head(vc_ref, s, h)], axis=0)
            acc = jnp.dot(p, jnp.concatenate([vcat, ones], axis=1),
                          preferred_element_type=F32)
            l = acc[:, HEAD_DIM:]
            o = (acc[:, :HEAD_DIM] / l).astype(o_ref.dtype)
            o_ref[lead + (slice(h * HEAD_DIM, (h + 1) * HEAD_DIM),)] = o.reshape(
                _dsa_block_shape(layout, HEAD_DIM))
            lse_tile = jnp.where(lane == h, m * LN_2 + jnp.log(l), lse_tile)
        lse_ref[lead + (slice(None),)] = lse_tile.reshape(_dsa_block_shape(layout, LANES))

    last = _dsa_lead(layout, qb - 1) + (slice(None),)
    kp_ref[...] = kc_ref[last].reshape(kp_ref.shape)
    vp_ref[...] = vc_ref[last].reshape(vp_ref.shape)


def _dsa_branch(q, q_col, kv, g, bsz, s_len):
    d = DILATIONS[g]
    assert WINDOWS[g] // d == ATT_BLOCK
    t = bsz * s_len
    ntile = s_len // PERM_TILE
    qb = min(DSA_BLOCKS_PER_STEP, s_len // d // ATT_BLOCK)
    if g == 0:
        layout = "rows"
        view = lambda x: x.reshape(bsz, s_len, x.shape[-1])
        blk = lambda w: (None, qb * ATT_BLOCK, w)
        grid = (bsz, 1, s_len // (qb * ATT_BLOCK))
        at = lambda c: (lambda b, r, n: (b, n, c))
    elif g == 1:
        layout = "tile"
        view = lambda x: x.reshape(bsz, ntile, 4, 4, PERM_RUN, x.shape[-1])
        blk = lambda w: (None, qb, None, 4, PERM_RUN, w)
        grid = (bsz, 4, ntile // qb)
        at = lambda c: (lambda b, r, n: (b, n, r, 0, 0, c))
    else:
        layout = "tiles4"
        view = lambda x: x.reshape(bsz, ntile, 16, PERM_RUN, x.shape[-1])
        blk = lambda w: (None, 4 * qb, None, PERM_RUN, w)
        grid = (bsz, 16, ntile // (4 * qb))
        at = lambda c: (lambda b, r, n: (b, n, r, 0, c))
    qv, kvv = view(q), view(kv)
    o, lse = pl.pallas_call(
        functools.partial(_dsa_kernel, dil=d, layout=layout, qb=qb,
                          slopes=_alibi_slopes(ATT_HEADS)),
        grid=grid,
        in_specs=[
            pl.BlockSpec(blk(ATT_WIDTH), at(q_col)),
            pl.BlockSpec(blk(ATT_WIDTH), at(0)),
            pl.BlockSpec(blk(ATT_WIDTH), at(1)),
        ],
        out_specs=[pl.BlockSpec(blk(ATT_WIDTH), at(0)),
                   pl.BlockSpec(blk(LANES), at(0))],
        out_shape=[jax.ShapeDtypeStruct(qv.shape[:-1] + (ATT_WIDTH,), BF16),
                   jax.ShapeDtypeStruct(qv.shape[:-1] + (LANES,), F32)],
        scratch_shapes=[pltpu.VMEM((ATT_BLOCK, ATT_WIDTH), BF16),
                        pltpu.VMEM((ATT_BLOCK, ATT_WIDTH), BF16)],
        compiler_params=pltpu.CompilerParams(
            dimension_semantics=("arbitrary", "arbitrary", "arbitrary")),
        name=f"dsa_branch{g}",
    )(qv, kvv, kvv)
    return o.reshape(t, ATT_WIDTH), lse.reshape(t, LANES)


def _merge_kernel(pt_ref, o0_ref, o1_ref, o2_ref, l0_ref, l1_ref, l2_ref, o_ref):
    pt = pt_ref[...]
    l0 = l0_ref[...]
    l12 = _dot01(pt, jnp.concatenate([l1_ref[...], l2_ref[...]], axis=1))
    l1, l2 = l12[:, :LANES], l12[:, LANES:]
    m = jnp.maximum(jnp.maximum(l0, l1), l2)
    e0, e1, e2 = jnp.exp(l0 - m), jnp.exp(l1 - m), jnp.exp(l2 - m)
    den = e0 + e1 + e2
    w1, w2 = e1 / den, e2 / den
    for hp in range(ATT_HEADS // 2):
        pair = slice(2 * hp * HEAD_DIM, (2 * hp + 2) * HEAD_DIM)
        o1 = jnp.dot(pt, o1_ref[:, pair], preferred_element_type=F32)
        o2 = jnp.dot(pt, o2_ref[:, pair], preferred_element_type=F32)
        for k in range(2):
            h = 2 * hp + k
            sl = slice(h * HEAD_DIM, (h + 1) * HEAD_DIM)
            in_pair = slice(k * HEAD_DIM, (k + 1) * HEAD_DIM)
            o0 = o0_ref[:, sl].astype(F32)
            o = (o0 + w1[:, h:h + 1] * (o1[:, in_pair] - o0)
                 + w2[:, h:h + 1] * (o2[:, in_pair] - o0))
            o_ref[:, sl] = o.astype(o_ref.dtype)


def _dsa_merge(outs, lses, perm_t):
    t, hd = outs[0].shape
    tm = PERM_TILE
    o_spec = pl.BlockSpec((tm, hd), lambda i: (i, 0))
    l_spec = pl.BlockSpec((tm, LANES), lambda i: (i, 0))
    return pl.pallas_call(
        _merge_kernel,
        grid=(t // tm,),
        in_specs=[pl.BlockSpec((tm, tm), lambda i: (0, 0)),
                  o_spec, o_spec, o_spec, l_spec, l_spec, l_spec],
        out_specs=o_spec,
        out_shape=jax.ShapeDtypeStruct((t, hd), BF16),
        compiler_params=pltpu.CompilerParams(dimension_semantics=("arbitrary",)),
        name="dsa_merge",
    )(perm_t, *outs, *lses)


def _ffn_up_kernel(x_ref, wu_ref, wg_ref, cw_ref, cb_ref, o_ref, wub_ref, wgb_ref,
                   us_ref, gs_ref, carry_ref, *, tm, tn, s_len):
    @pl.when(pl.program_id(1) == 0)
    def _():
        wub_ref[...] = wu_ref[...].astype(BF16)
        wgb_ref[...] = wg_ref[...].astype(BF16)

    @pl.when((pl.program_id(1) * tm) % s_len == 0)
    def _():
        carry_ref[...] = jnp.zeros_like(carry_ref)

    n_rb = tm // FFN_ROW_SUB
    subs = [(cb, rb) for cb in range(tn // FFN_COL_SUB) for rb in range(n_rb)]

    def window(idx):
        cb, rb = subs[idx]
        return (slice(rb * FFN_ROW_SUB, (rb + 1) * FFN_ROW_SUB),
                slice(cb * FFN_COL_SUB, (cb + 1) * FFN_COL_SUB))

    def project(idx):
        rows, cols = window(idx)
        slot = idx % FFN_SLOTS
        xr = x_ref[rows, :]
        us_ref[slot] = jnp.dot(xr, wub_ref[:, cols], preferred_element_type=F32)
        gs_ref[slot, 8:] = jnp.dot(xr, wgb_ref[:, cols], preferred_element_type=F32)
        if subs[idx][1] == 0:
            gs_ref[slot, :8] = carry_ref[:, cols]
        else:
            gs_ref[slot, :8] = gs_ref[(idx - 1) % FFN_SLOTS, FFN_ROW_SUB:]
        if subs[idx][1] == n_rb - 1:
            carry_ref[:, cols] = gs_ref[slot, FFN_ROW_SUB:]

    def epilogue(idx):
        rows, cols = window(idx)
        slot = idx % FFN_SLOTS
        g = gs_ref[slot, 8:]
        g1 = gs_ref[slot, 7:7 + FFN_ROW_SUB]
        g2 = gs_ref[slot, 6:6 + FFN_ROW_SUB]
        cw = cw_ref[:, cols]
        conv = cw[0:1] * g2 + cw[1:2] * g1 + cw[2:3] * g + cb_ref[:, cols]
        act = 0.5 * conv * (1.0 + lax.erf(conv * (2.0 ** -0.5)))
        o_ref[rows, cols] = (act * us_ref[slot]).astype(o_ref.dtype)

    ahead = FFN_SLOTS - 1
    for idx in range(ahead):
        project(idx)
    for idx in range(len(subs)):
        if idx + ahead < len(subs):
            project(idx + ahead)
        epilogue(idx)


def _ffn_up(xn, w_up, layer, conv_w, conv_b, s_len, tm=2048, tn=512):
    m, k = xn.shape
    nn = D_FF // tn
    assert s_len % tm == 0 and tm % FFN_ROW_SUB == 0 and tn % FFN_COL_SUB == 0
    return pl.pallas_call(
        functools.partial(_ffn_up_kernel, tm=tm, tn=tn, s_len=s_len),
        grid=(nn, m // tm),
        in_specs=[
            pl.BlockSpec((tm, k), lambda j, i: (i, 0)),
            pl.BlockSpec((None, k, tn), lambda j, i: (layer, 0, j)),
            pl.BlockSpec((None, k, tn), lambda j, i: (layer, 0, nn + j)),
            pl.BlockSpec((3, tn), lambda j, i: (0, j)),
            pl.BlockSpec((1, tn), lambda j, i: (0, j)),
        ],
        out_specs=pl.BlockSpec((tm, tn), lambda j, i: (i, j)),
        out_shape=jax.ShapeDtypeStruct((m, D_FF), BF16),
        scratch_shapes=[pltpu.VMEM((k, tn), BF16), pltpu.VMEM((k, tn), BF16),
                        pltpu.VMEM((FFN_SLOTS, FFN_ROW_SUB, FFN_COL_SUB), F32),
                        pltpu.VMEM((FFN_SLOTS, 8 + FFN_ROW_SUB, FFN_COL_SUB), F32),
                        pltpu.VMEM((8, tn), F32)],
        compiler_params=pltpu.CompilerParams(
            dimension_semantics=("arbitrary", "arbitrary")),
        name="ffn_up",
    )(xn, w_up, w_up, conv_w, conv_b.reshape(1, D_FF))


def _conv_glu(h, xn, w_up, layer, conv_w, conv_b, w_down, s_len):
    act = _ffn_up(xn, w_up, layer, conv_w, conv_b, s_len)
    return _matmul(act, w_down, layer, F32, residual=h, tn=1024, weight_buffers=1)


def kernel(x, attn_norm, gla_w_in, gla_w_a2, gla_b_a2, gla_head_norm, gla_w_out, kv_norm, w_kv,
           dsa_w_q, dsa_w_out, ffn_norm, ffn_w_up, ffn_conv_w, ffn_conv_b, ffn_w_down, final_norm):
    bsz, s_len, d = x.shape
    t = bsz * s_len
    h = x.reshape(t, d)

    n_main = 2 * GLA_KEY_DIM + 2 * GLA_VAL_DIM
    w_a1 = jnp.pad(gla_w_in[0, :, n_main:].astype(BF16), ((0, 0), (0, LANES - GATE_RANK)))
    w_a2 = jnp.pad(gla_w_a2[0].astype(BF16), ((0, LANES - GATE_RANK), (0, 0)))
    xn, a = _rmsnorm_gate_in(h, attn_norm[0], w_a1)
    proj = _matmul(xn, gla_w_in, 0, BF16, n_cols=n_main, tm=1024)
    o = _gla_recurrence(proj, a, w_a2, gla_b_a2[0], gla_head_norm[0], bsz, s_len)
    h, xn = _matmul_res_norm(o, gla_w_out, 0, h, ffn_norm[0])
    h = _conv_glu(h, xn, ffn_w_up, 0, ffn_conv_w[0], ffn_conv_b[0], ffn_w_down, s_len)

    perm_np = _perm_matrix()
    perm = jnp.asarray(perm_np, BF16)
    perm_t = jnp.asarray(perm_np.T, BF16)
    xkv, xq, xq_perm = _rmsnorm2_perm(h, kv_norm, attn_norm[1], perm)
    kv, kv_perm = _matmul_and_permuted(xkv, w_kv[None], 0, perm)

    q_scale = HEAD_DIM ** -0.5 * LOG2_E
    q0 = _matmul(xq, dsa_w_q, 0, BF16, n_cols=ATT_WIDTH, out_scale=q_scale, tm=1024)
    q12 = _matmul(xq_perm, dsa_w_q, 0, BF16, n_cols=2 * ATT_WIDTH, col_start=ATT_WIDTH,
                  out_scale=q_scale, tm=1024)
    o0, lse0 = _dsa_branch(q0, 0, kv, 0, bsz, s_len)
    o1, lse1 = _dsa_branch(q12, 0, kv_perm, 1, bsz, s_len)
    o2, lse2 = _dsa_branch(q12, 1, kv_perm, 2, bsz, s_len)
    o = _dsa_merge([o0, o1, o2], [lse0, lse1, lse2], perm_t)
    h, xn = _matmul_res_norm(o, dsa_w_out, 0, h, ffn_norm[1])
    h = _conv_glu(h, xn, ffn_w_up, 1, ffn_conv_w[1], ffn_conv_b[1], ffn_w_down, s_len)

    return _rmsnorm(h, final_norm, F32).reshape(bsz, s_len, d)
```

```python
import functools
import math

import numpy as np
import jax
import jax.numpy as jnp
from jax import lax
from jax.experimental import pallas as pl
from jax.experimental.pallas import tpu as pltpu

D_MODEL = 2048
GLA_HEADS = 4
GLA_KEY_DIM = 1024
GLA_VAL_DIM = 2048
GLA_DK = 256
GLA_DV = 512
GATE_RANK = 16
GATE_NORMALIZER = 16.0
GLA_CHUNK = 64
GLA_BLOCK = 256
GLA_HEADS_PER_STEP = 4
ATT_HEADS = 16
HEAD_DIM = 128
ATT_WIDTH = ATT_HEADS * HEAD_DIM
WINDOWS = (128, 512, 2048)
DILATIONS = (1, 4, 16)
ATT_BLOCK = 128
DSA_BLOCKS_PER_STEP = 4
QK_LOOKAHEAD = 2
PERM_TILE = 512
PERM_RUN = PERM_TILE // 16
D_FF = 5632
EPS = 1e-6
LOG2_E = math.log2(math.e)
LN_2 = math.log(2.0)
LANES = 128
FFN_ROW_SUB = 512
FFN_COL_SUB = 256
FFN_SLOTS = 2

F32 = jnp.float32
BF16 = jnp.bfloat16


def _alibi_slopes(n):
    def pow2_slopes(m):
        start = 2.0 ** (-8.0 / m)
        return [start ** (i + 1) for i in range(m)]
    assert math.log2(n).is_integer()
    return [float(v) for v in np.array(pow2_slopes(n), dtype=np.float32)]


def _rmsnorm_kernel(x_ref, g_ref, o_ref):
    x = x_ref[...]
    ms = jnp.mean(x * x, axis=-1, keepdims=True)
    o_ref[...] = ((x * lax.rsqrt(ms + EPS)) * g_ref[...]).astype(o_ref.dtype)


def _rmsnorm(x, g, out_dtype, tm=512):
    m, d = x.shape
    return pl.pallas_call(
        _rmsnorm_kernel,
        grid=(m // tm,),
        in_specs=[pl.BlockSpec((tm, d), lambda i: (i, 0)),
                  pl.BlockSpec((1, d), lambda i: (0, 0))],
        out_specs=pl.BlockSpec((tm, d), lambda i: (i, 0)),
        out_shape=jax.ShapeDtypeStruct((m, d), out_dtype),
        compiler_params=pltpu.CompilerParams(dimension_semantics=("arbitrary",)),
        name="rmsnorm",
    )(x, g.reshape(1, d))


def _cast_weights_once(w_ref, wb_ref):
    @pl.when(pl.program_id(1) == 0)
    def _():
        wb_ref[...] = w_ref[...].astype(BF16)


def _mm_kernel(x_ref, w_ref, *rest, out_scale):
    o_ref, wb_ref = rest[-2:]
    _cast_weights_once(w_ref, wb_ref)
    acc = jnp.dot(x_ref[...], wb_ref[...], preferred_element_type=F32)
    if len(rest) == 3:
        acc = rest[0][...] + acc
    if out_scale is not None:
        acc = acc * out_scale
    o_ref[...] = acc.astype(o_ref.dtype)


def _matmul(x, w, layer, out_dtype, n_cols=None, col_start=0, residual=None, out_scale=None,
            tm=512, tn=1024, weight_buffers=2):
    m, k = x.shape
    n = w.shape[2] if n_cols is None else n_cols
    assert w.shape[1] == k and m % tm == 0 and n % tn == 0 and col_start % tn == 0
    j0 = col_start // tn
    in_specs = [pl.BlockSpec((tm, k), lambda j, i: (i, 0)),
                pl.BlockSpec((None, k, tn), lambda j, i: (layer, 0, j0 + j),
                             pipeline_mode=pl.Buffered(weight_buffers))]
    args = [x, w]
    if residual is not None:
        in_specs.append(pl.BlockSpec((tm, tn), lambda j, i: (i, j)))
        args.append(residual)
    return pl.pallas_call(
        functools.partial(_mm_kernel, out_scale=out_scale),
        grid=(n // tn, m // tm),
        in_specs=in_specs,
        out_specs=pl.BlockSpec((tm, tn), lambda j, i: (i, j)),
        out_shape=jax.ShapeDtypeStruct((m, n), out_dtype),
        scratch_shapes=[pltpu.VMEM((k, tn), BF16)],
        compiler_params=pltpu.CompilerParams(
            dimension_semantics=("arbitrary", "arbitrary")),
        name="matmul",
    )(*args)


def _mm_perm_kernel(x_ref, w_ref, p_ref, o_ref, op_ref, wb_ref):
    _cast_weights_once(w_ref, wb_ref)
    out = jnp.dot(x_ref[...], wb_ref[...], preferred_element_type=F32).astype(BF16)
    o_ref[...] = out
    for t in range(o_ref.shape[0] // PERM_TILE):
        rows = slice(t * PERM_TILE, (t + 1) * PERM_TILE)
        op_ref[rows, :] = jnp.dot(p_ref[...], out[rows], preferred_element_type=F32).astype(BF16)


def _matmul_and_permuted(x, w, layer, perm, tm=1024, tn=1024):
    m, k = x.shape
    n = w.shape[2]
    assert w.shape[1] == k and m % tm == 0 and n % tn == 0 and tm % PERM_TILE == 0
    out_spec = pl.BlockSpec((tm, tn), lambda j, i: (i, j))
    out = jax.ShapeDtypeStruct((m, n), BF16)
    return pl.pallas_call(
        _mm_perm_kernel,
        grid=(n // tn, m // tm),
        in_specs=[pl.BlockSpec((tm, k), lambda j, i: (i, 0)),
                  pl.BlockSpec((None, k, tn), lambda j, i: (layer, 0, j)),
                  pl.BlockSpec((PERM_TILE, PERM_TILE), lambda j, i: (0, 0))],
        out_specs=[out_spec, out_spec],
        out_shape=[out, out],
        scratch_shapes=[pltpu.VMEM((k, tn), BF16)],
        compiler_params=pltpu.CompilerParams(
            dimension_semantics=("arbitrary", "arbitrary")),
        name="matmul_and_permuted",
    )(x, w, perm)


def _mm_res_norm_kernel(x_ref, w_ref, r_ref, g_ref, h_ref, xn_ref, wb_ref):
    @pl.when(pl.program_id(0) == 0)
    def _():
        wb_ref[...] = w_ref[...].astype(BF16)

    h = r_ref[...] + jnp.dot(x_ref[...], wb_ref[...], preferred_element_type=F32)
    h_ref[...] = h
    ms = jnp.mean(h * h, axis=-1, keepdims=True)
    xn_ref[...] = ((h * lax.rsqrt(ms + EPS)) * g_ref[...]).astype(xn_ref.dtype)


def _matmul_res_norm(x, w, layer, residual, norm_g, tm=512):
    m, k = x.shape
    n = w.shape[2]
    assert w.shape[1] == k and residual.shape == (m, n) and m % tm == 0
    row = pl.BlockSpec((tm, n), lambda i: (i, 0))
    return pl.pallas_call(
        _mm_res_norm_kernel,
        grid=(m // tm,),
        in_specs=[pl.BlockSpec((tm, k), lambda i: (i, 0)),
                  pl.BlockSpec((None, k, n), lambda i: (layer, 0, 0),
                               pipeline_mode=pl.Buffered(1)),
                  row,
                  pl.BlockSpec((1, n), lambda i: (0, 0))],
        out_specs=[row, row],
        out_shape=[jax.ShapeDtypeStruct((m, n), F32), jax.ShapeDtypeStruct((m, n), BF16)],
        scratch_shapes=[pltpu.VMEM((k, n), BF16)],
        compiler_params=pltpu.CompilerParams(dimension_semantics=("arbitrary",)),
        name="matmul_res_norm",
    )(x, w, residual, norm_g.reshape(1, n))


def _norm2_perm_kernel(x_ref, ga_ref, gb_ref, p_ref, a_ref, b_ref, bp_ref):
    x = x_ref[...]
    ms = jnp.mean(x * x, axis=-1, keepdims=True)
    y = x * lax.rsqrt(ms + EPS)
    a_ref[...] = (y * ga_ref[...]).astype(BF16)
    b = (y * gb_ref[...]).astype(BF16)
    b_ref[...] = b
    bp_ref[...] = jnp.dot(p_ref[...], b, preferred_element_type=F32).astype(BF16)


def _rmsnorm2_perm(x, g_a, g_b, perm):
    m, d = x.shape
    tm = PERM_TILE
    row = pl.BlockSpec((tm, d), lambda i: (i, 0))
    vec = pl.BlockSpec((1, d), lambda i: (0, 0))
    out = jax.ShapeDtypeStruct((m, d), BF16)
    return pl.pallas_call(
        _norm2_perm_kernel,
        grid=(m // tm,),
        in_specs=[row, vec, vec, pl.BlockSpec((tm, tm), lambda i: (0, 0))],
        out_specs=[row, row, row],
        out_shape=[out, out, out],
        compiler_params=pltpu.CompilerParams(dimension_semantics=("arbitrary",)),
        name="rmsnorm2_perm",
    )(x, g_a.reshape(1, d), g_b.reshape(1, d), perm)


def _norm_gate_in_kernel(x_ref, g_ref, w1_ref, xn_ref, a_ref):
    x = x_ref[...]
    ms = jnp.mean(x * x, axis=-1, keepdims=True)
    xn = ((x * lax.rsqrt(ms + EPS)) * g_ref[...]).astype(BF16)
    xn_ref[...] = xn
    a_ref[...] = jnp.dot(xn, w1_ref[...], preferred_element_type=F32).astype(BF16)


def _rmsnorm_gate_in(x, g, w_a1, tm=512):
    m, d = x.shape
    return pl.pallas_call(
        _norm_gate_in_kernel,
        grid=(m // tm,),
        in_specs=[pl.BlockSpec((tm, d), lambda i: (i, 0)),
                  pl.BlockSpec((1, d), lambda i: (0, 0)),
                  pl.BlockSpec((d, LANES), lambda i: (0, 0))],
        out_specs=[pl.BlockSpec((tm, d), lambda i: (i, 0)),
                   pl.BlockSpec((tm, LANES), lambda i: (i, 0))],
        out_shape=[jax.ShapeDtypeStruct((m, d), BF16), jax.ShapeDtypeStruct((m, LANES), BF16)],
        compiler_params=pltpu.CompilerParams(dimension_semantics=("arbitrary",)),
        name="rmsnorm_gate_in",
    )(x, g.reshape(1, d), w_a1)


def _split3(x):
    hi = x.astype(BF16)
    r1 = x - hi.astype(F32)
    mid = r1.astype(BF16)
    lo = (r1 - mid.astype(F32)).astype(BF16)
    return hi, mid, lo


def _dot01(mat01, x):
    hi, mid, lo = _split3(x)
    return (jnp.dot(mat01, hi, preferred_element_type=F32)
            + jnp.dot(mat01, mid, preferred_element_type=F32)
            + jnp.dot(mat01, lo, preferred_element_type=F32))


def _gla_kernel(q_ref, k_ref, v_ref, r_ref, a_ref, w2_ref, b2_ref, hn_ref, o_ref, st_ref):
    @pl.when(pl.program_id(2) == 0)
    def _():
        st_ref[...] = jnp.zeros_like(st_ref)

    nb = GLA_BLOCK
    heads = range(GLA_HEADS_PER_STEP)
    dk = lambda h: slice(h * GLA_DK, (h + 1) * GLA_DK)
    dv = lambda h: slice(h * GLA_DV, (h + 1) * GLA_DV)
    row = lax.broadcasted_iota(jnp.int32, (nb, nb), 0)
    col = lax.broadcasted_iota(jnp.int32, (nb, nb), 1)
    same_chunk = (row // GLA_CHUNK) == (col // GLA_CHUNK)
    causal = same_chunk & (col <= row)
    tri = jnp.where(causal, 1.0, 0.0).astype(BF16)
    nt = (((1,), (1,)), ((), ()))
    tn = (((0,), (0,)), ((), ()))

    a = a_ref[...]
    cum = []
    for h in heads:
        z = jnp.dot(a, w2_ref[:, dk(h)], preferred_element_type=F32) + b2_ref[:, dk(h)]
        log_sig = jnp.minimum(z, 0.0) - jnp.log(1.0 + jnp.exp(-jnp.abs(z)))
        log_alpha = log_sig * (1.0 / GATE_NORMALIZER)
        cum.append(_dot01(tri, log_alpha))
    k = [k_ref[:, dk(h)].astype(F32) for h in heads]
    q_dec = [((q_ref[:, dk(h)].astype(F32) * (GLA_DK ** -0.5)) * jnp.exp(cum[h])).astype(BF16)
             for h in heads]
    k_inv = [(k[h] * jnp.exp(-cum[h])).astype(BF16) for h in heads]
    scores = [lax.dot_general(q_dec[h], k_inv[h], nt, preferred_element_type=F32)
              for h in heads]
    scores = [jnp.where(causal, s, 0.0).astype(BF16) for s in scores]
    o_intra = [jnp.dot(scores[h], v_ref[:, dv(h)], preferred_element_type=F32)
               for h in heads]

    o_parts = [[] for _ in heads]
    for c in range(nb // GLA_CHUNK):
        rows = slice(c * GLA_CHUNK, (c + 1) * GLA_CHUNK)
        for h in heads:
            last_c = cum[h][(c + 1) * GLA_CHUNK - 1:(c + 1) * GLA_CHUNK]
            k_end = (k[h][rows] * jnp.exp(last_c - cum[h][rows])).astype(BF16)
            st = st_ref[h]
            o_inter = lax.dot_general(q_dec[h][rows], st.astype(BF16), nt,
                                      preferred_element_type=F32)
            upd = lax.dot_general(v_ref[rows, dv(h)], k_end, tn,
                                  preferred_element_type=F32)
            st_ref[h] = st * jnp.exp(last_c) + upd
            o_parts[h].append(o_intra[h][rows] + o_inter)

    for h in heads:
        o = jnp.concatenate(o_parts[h], axis=0)
        ms = jnp.mean(o * o, axis=-1, keepdims=True)
        o = (o * lax.rsqrt(ms + EPS)) * hn_ref[...]
        r = r_ref[:, dv(h)].astype(F32)
        gate = r * (1.0 / (1.0 + jnp.exp(-r)))
        o_ref[:, dv(h)] = (o * gate).astype(o_ref.dtype)


def _gla_recurrence(proj, a, w_a2, b_a2, head_norm, bsz, s_len):
    t = bsz * s_len
    nblk = s_len // GLA_BLOCK
    hps = GLA_HEADS_PER_STEP
    wk, wv = hps * GLA_DK, hps * GLA_DV
    k_off = GLA_KEY_DIM // wk
    v_off = 2 * GLA_KEY_DIM // wv
    r_off = (2 * GLA_KEY_DIM + GLA_VAL_DIM) // wv
    row = lambda b, g, i: b * nblk + i
    return pl.pallas_call(
        _gla_kernel,
        grid=(bsz, GLA_HEADS // hps, nblk),
        in_specs=[
            pl.BlockSpec((GLA_BLOCK, wk), lambda b, g, i: (row(b, g, i), g)),
            pl.BlockSpec((GLA_BLOCK, wk), lambda b, g, i: (row(b, g, i), k_off + g)),
            pl.BlockSpec((GLA_BLOCK, wv), lambda b, g, i: (row(b, g, i), v_off + g)),
            pl.BlockSpec((GLA_BLOCK, wv), lambda b, g, i: (row(b, g, i), r_off + g)),
            pl.BlockSpec((GLA_BLOCK, LANES), lambda b, g, i: (row(b, g, i), 0)),
            pl.BlockSpec((LANES, wk), lambda b, g, i: (0, g)),
            pl.BlockSpec((1, wk), lambda b, g, i: (0, g)),
            pl.BlockSpec((1, GLA_DV), lambda b, g, i: (0, 0)),
        ],
        out_specs=pl.BlockSpec((GLA_BLOCK, wv), lambda b, g, i: (row(b, g, i), g)),
        out_shape=jax.ShapeDtypeStruct((t, GLA_VAL_DIM), BF16),
        scratch_shapes=[pltpu.VMEM((hps, GLA_DV, GLA_DK), F32)],
        compiler_params=pltpu.CompilerParams(
            dimension_semantics=("arbitrary", "arbitrary", "arbitrary")),
        name="gla_recurrence",
    )(proj, proj, proj, proj, a, w_a2, b_a2.reshape(1, GLA_KEY_DIM), head_norm.reshape(1, GLA_DV))


def _perm_matrix():
    p = np.zeros((PERM_TILE, PERM_TILE), np.float32)
    for i in range(PERM_RUN):
        for a in range(4):
            for r4 in range(4):
                p[(4 * r4 + a) * PERM_RUN + i, 16 * i + 4 * a + r4] = 1.0
    return p


def _dsa_lead(layout, s):
    if layout == "rows":
        return (slice(s * ATT_BLOCK, (s + 1) * ATT_BLOCK),)
    if layout == "tile":
        return (s, slice(None), slice(None))
    assert layout == "tiles4"
    return (slice(4 * s, 4 * s + 4), slice(None))


def _dsa_block_shape(layout, width):
    return (ATT_BLOCK, width) if layout == "rows" else (4, PERM_RUN, width)


def _dsa_kernel(q_ref, kc_ref, vc_ref, o_ref, lse_ref, kp_ref, vp_ref, bias_ref, *, dil, layout,
                qb, slopes):
    n = pl.program_id(2)
    nk = 2 * ATT_BLOCK
    ck = lax.broadcasted_iota(jnp.int32, (ATT_BLOCK, nk), 1)
    is_cur = ck >= ATT_BLOCK

    @pl.when(n == 0)
    def _():
        kp_ref[...] = jnp.zeros_like(kp_ref)
        vp_ref[...] = jnp.zeros_like(vp_ref)

    @pl.when((pl.program_id(0) == 0) & (pl.program_id(1) == 0) & (n == 0))
    def _():
        rq = lax.broadcasted_iota(jnp.int32, (ATT_BLOCK, nk), 0)
        ck_in = ck & (ATT_BLOCK - 1)
        if layout == "tile":
            sub_q = 4 * (rq & (PERM_RUN - 1)) + (rq >> 5)
            sub_k = 4 * (ck_in & (PERM_RUN - 1)) + (ck_in >> 5)
        else:
            sub_q, sub_k = rq, ck_in
        j = sub_q - sub_k + jnp.where(is_cur, 0, ATT_BLOCK)
        valid = (j >= 0) & (j <= ATT_BLOCK)
        neg_dist = jnp.where(valid, -(j * dil).astype(F32), -jnp.inf)
        for h in range(ATT_HEADS):
            bias_ref[h] = (slopes[h] * LOG2_E) * neg_dist

    has_prev = is_cur | (n > 0)

    lane = lax.broadcasted_iota(jnp.int32, (ATT_BLOCK, LANES), 1)
    ones = jnp.ones((nk, HEAD_DIM), BF16)
    nt = (((1,), (1,)), ((), ()))

    def head(ref, s, h):
        x = ref[_dsa_lead(layout, s) + (slice(h * HEAD_DIM, (h + 1) * HEAD_DIM),)]
        return x.reshape(ATT_BLOCK, HEAD_DIM)

    def prev_head(ref, carry_ref, s, h):
        if s == 0:
            return carry_ref[:, h * HEAD_DIM:(h + 1) * HEAD_DIM]
        return head(ref, s - 1, h)

    for s in range(qb):
        lead = _dsa_lead(layout, s)

        def scores(h, s=s):
            kcat = jnp.concatenate([prev_head(kc_ref, kp_ref, s, h), head(kc_ref, s, h)], axis=0)
            return lax.dot_general(head(q_ref, s, h), kcat, nt, preferred_element_type=F32)

        pending = [scores(h) for h in range(QK_LOOKAHEAD)]
        lse_tile = jnp.zeros((ATT_BLOCK, LANES), F32)
        for h in range(ATT_HEADS):
            sc = pending.pop(0)
            if h + QK_LOOKAHEAD < ATT_HEADS:
                pending.append(scores(h + QK_LOOKAHEAD))
            bias = bias_ref[h]
            if s == 0:
                bias = jnp.where(has_prev, bias, -jnp.inf)
            z = sc + bias
            m = jnp.max(z, axis=-1, keepdims=True)
            p = jnp.exp2(z - m).astype(BF16)
            vcat = jnp.concatenate([prev_head(vc_ref, vp_ref, s, h), head(vc_ref, s, h)], axis=0)
            acc = jnp.dot(p, jnp.concatenate([vcat, ones], axis=1),
                          preferred_element_type=F32)
            l = acc[:, HEAD_DIM:]
            o = (acc[:, :HEAD_DIM] / l).astype(o_ref.dtype)
            o_ref[lead + (slice(h * HEAD_DIM, (h + 1) * HEAD_DIM),)] = o.reshape(
                _dsa_block_shape(layout, HEAD_DIM))
            lse_tile = jnp.where(lane == h, m * LN_2 + jnp.log(l), lse_tile)
        lse_ref[lead + (slice(None),)] = lse_tile.reshape(_dsa_block_shape(layout, LANES))

    last = _dsa_lead(layout, qb - 1) + (slice(None),)
    kp_ref[...] = kc_ref[last].reshape(kp_ref.shape)
    vp_ref[...] = vc_ref[last].reshape(vp_ref.shape)


def _dsa_branch(q, q_col, kv, g, bsz, s_len):
    d = DILATIONS[g]
    assert WINDOWS[g] // d == ATT_BLOCK
    t = bsz * s_len
    ntile = s_len // PERM_TILE
    qb = min(DSA_BLOCKS_PER_STEP, s_len // d // ATT_BLOCK)
    if g == 0:
        layout = "rows"
        view = lambda x: x.reshape(bsz, s_len, x.shape[-1])
        blk = lambda w: (None, qb * ATT_BLOCK, w)
        grid = (bsz, 1, s_len // (qb * ATT_BLOCK))
        at = lambda c: (lambda b, r, n: (b, n, c))
    elif g == 1:
        layout = "tile"
        view = lambda x: x.reshape(bsz, ntile, 4, 4, PERM_RUN, x.shape[-1])
        blk = lambda w: (None, qb, None, 4, PERM_RUN, w)
        grid = (bsz, 4, ntile // qb)
        at = lambda c: (lambda b, r, n: (b, n, r, 0, 0, c))
    else:
        layout = "tiles4"
        view = lambda x: x.reshape(bsz, ntile, 16, PERM_RUN, x.shape[-1])
        blk = lambda w: (None, 4 * qb, None, PERM_RUN, w)
        grid = (bsz, 16, ntile // (4 * qb))
        at = lambda c: (lambda b, r, n: (b, n, r, 0, c))
    qv, kvv = view(q), view(kv)
    o, lse = pl.pallas_call(
        functools.partial(_dsa_kernel, dil=d, layout=layout, qb=qb,
                          slopes=_alibi_slopes(ATT_HEADS)),
        grid=grid,
        in_specs=[
            pl.BlockSpec(blk(ATT_WIDTH), at(q_col)),
            pl.BlockSpec(blk(ATT_WIDTH), at(0)),
            pl.BlockSpec(blk(ATT_WIDTH), at(1)),
        ],
        out_specs=[pl.BlockSpec(blk(ATT_WIDTH), at(0)),
                   pl.BlockSpec(blk(LANES), at(0))],
        out_shape=[jax.ShapeDtypeStruct(qv.shape[:-1] + (ATT_WIDTH,), BF16),
                   jax.ShapeDtypeStruct(qv.shape[:-1] + (LANES,), F32)],
        scratch_shapes=[pltpu.VMEM((ATT_BLOCK, ATT_WIDTH), BF16),
                        pltpu.VMEM((ATT_BLOCK, ATT_WIDTH), BF16),
                        pltpu.VMEM((ATT_HEADS, ATT_BLOCK, 2 * ATT_BLOCK), F32)],
        compiler_params=pltpu.CompilerParams(
            dimension_semantics=("arbitrary", "arbitrary", "arbitrary")),
        name=f"dsa_branch{g}",
    )(qv, kvv, kvv)
    return o.reshape(t, ATT_WIDTH), lse.reshape(t, LANES)


def _merge_kernel(pt_ref, o0_ref, o1_ref, o2_ref, l0_ref, l1_ref, l2_ref, o_ref):
    pt = pt_ref[...]
    l0 = l0_ref[...]
    l12 = _dot01(pt, jnp.concatenate([l1_ref[...], l2_ref[...]], axis=1))
    l1, l2 = l12[:, :LANES], l12[:, LANES:]
    m = jnp.maximum(jnp.maximum(l0, l1), l2)
    e0, e1, e2 = jnp.exp(l0 - m), jnp.exp(l1 - m), jnp.exp(l2 - m)
    den = e0 + e1 + e2
    w1, w2 = e1 / den, e2 / den
    for hp in range(ATT_HEADS // 2):
        pair = slice(2 * hp * HEAD_DIM, (2 * hp + 2) * HEAD_DIM)
        o1 = jnp.dot(pt, o1_ref[:, pair], preferred_element_type=F32)
        o2 = jnp.dot(pt, o2_ref[:, pair], preferred_element_type=F32)
        for k in range(2):
            h = 2 * hp + k
            sl = slice(h * HEAD_DIM, (h + 1) * HEAD_DIM)
            in_pair = slice(k * HEAD_DIM, (k + 1) * HEAD_DIM)
            o0 = o0_ref[:, sl].astype(F32)
            o = (o0 + w1[:, h:h + 1] * (o1[:, in_pair] - o0)
                 + w2[:, h:h + 1] * (o2[:, in_pair] - o0))
            o_ref[:, sl] = o.astype(o_ref.dtype)


def _dsa_merge(outs, lses, perm_t):
    t, hd = outs[0].shape
    tm = PERM_TILE
    o_spec = pl.BlockSpec((tm, hd), lambda i: (i, 0))
    l_spec = pl.BlockSpec((tm, LANES), lambda i: (i, 0))
    return pl.pallas_call(
        _merge_kernel,
        grid=(t // tm,),
        in_specs=[pl.BlockSpec((tm, tm), lambda i: (0, 0)),
                  o_spec, o_spec, o_spec, l_spec, l_spec, l_spec],
        out_specs=o_spec,
        out_shape=jax.ShapeDtypeStruct((t, hd), BF16),
        compiler_params=pltpu.CompilerParams(dimension_semantics=("arbitrary",)),
        name="dsa_merge",
    )(perm_t, *outs, *lses)


def _ffn_up_kernel(x_ref, wu_ref, wg_ref, cw_ref, cb_ref, o_ref, wub_ref, wgb_ref,
                   us_ref, gs_ref, carry_ref, *, tm, tn, s_len):
    @pl.when(pl.program_id(1) == 0)
    def _():
        wub_ref[...] = wu_ref[...].astype(BF16)
        wgb_ref[...] = wg_ref[...].astype(BF16)

    @pl.when((pl.program_id(1) * tm) % s_len == 0)
    def _():
        carry_ref[...] = jnp.zeros_like(carry_ref)

    n_rb = tm // FFN_ROW_SUB
    subs = [(cb, rb) for cb in range(tn // FFN_COL_SUB) for rb in range(n_rb)]

    def window(idx):
        cb, rb = subs[idx]
        return (slice(rb * FFN_ROW_SUB, (rb + 1) * FFN_ROW_SUB),
                slice(cb * FFN_COL_SUB, (cb + 1) * FFN_COL_SUB))

    def project(idx):
        rows, cols = window(idx)
        slot = idx % FFN_SLOTS
        xr = x_ref[rows, :]
        us_ref[slot] = jnp.dot(xr, wub_ref[:, cols], preferred_element_type=F32)
        gs_ref[slot, 8:] = jnp.dot(xr, wgb_ref[:, cols], preferred_element_type=F32)
        if subs[idx][1] == 0:
            gs_ref[slot, :8] = carry_ref[:, cols]
        else:
            gs_ref[slot, :8] = gs_ref[(idx - 1) % FFN_SLOTS, FFN_ROW_SUB:]
        if subs[idx][1] == n_rb - 1:
            carry_ref[:, cols] = gs_ref[slot, FFN_ROW_SUB:]

    def epilogue(idx):
        rows, cols = window(idx)
        slot = idx % FFN_SLOTS
        g = gs_ref[slot, 8:]
        g1 = gs_ref[slot, 7:7 + FFN_ROW_SUB]
        g2 = gs_ref[slot, 6:6 + FFN_ROW_SUB]
        cw = cw_ref[:, cols]
        conv = cw[0:1] * g2 + cw[1:2] * g1 + cw[2:3] * g + cb_ref[:, cols]
        act = 0.5 * conv * (1.0 + lax.erf(conv * (2.0 ** -0.5)))
        o_ref[rows, cols] = (act * us_ref[slot]).astype(o_ref.dtype)

    ahead = FFN_SLOTS - 1
    for idx in range(ahead):
        project(idx)
    for idx in range(len(subs)):
        if idx + ahead < len(subs):
            project(idx + ahead)
        epilogue(idx)


def _ffn_up(xn, w_up, layer, conv_w, conv_b, s_len, tm=2048, tn=512):
    m, k = xn.shape
    nn = D_FF // tn
    assert s_len % tm == 0 and tm % FFN_ROW_SUB == 0 and tn % FFN_COL_SUB == 0
    return pl.pallas_call(
        functools.partial(_ffn_up_kernel, tm=tm, tn=tn, s_len=s_len),
        grid=(nn, m // tm),
        in_specs=[
            pl.BlockSpec((tm, k), lambda j, i: (i, 0)),
            pl.BlockSpec((None, k, tn), lambda j, i: (layer, 0, j)),
            pl.BlockSpec((None, k, tn), lambda j, i: (layer, 0, nn + j)),
            pl.BlockSpec((3, tn), lambda j, i: (0, j)),
            pl.BlockSpec((1, tn), lambda j, i: (0, j)),
        ],
        out_specs=pl.BlockSpec((tm, tn), lambda j, i: (i, j)),
        out_shape=jax.ShapeDtypeStruct((m, D_FF), BF16),
        scratch_shapes=[pltpu.VMEM((k, tn), BF16), pltpu.VMEM((k, tn), BF16),
                        pltpu.VMEM((FFN_SLOTS, FFN_ROW_SUB, FFN_COL_SUB), F32),
                        pltpu.VMEM((FFN_SLOTS, 8 + FFN_ROW_SUB, FFN_COL_SUB), F32),
                        pltpu.VMEM((8, tn), F32)],
        compiler_params=pltpu.CompilerParams(
            dimension_semantics=("arbitrary", "arbitrary")),
        name="ffn_up",
    )(xn, w_up, w_up, conv_w, conv_b.reshape(1, D_FF))


def _conv_glu(h, xn, w_up, layer, conv_w, conv_b, w_down, s_len):
    act = _ffn_up(xn, w_up, layer, conv_w, conv_b, s_len)
    return _matmul(act, w_down, layer, F32, residual=h, tn=1024, weight_buffers=1)


def kernel(x, attn_norm, gla_w_in, gla_w_a2, gla_b_a2, gla_head_norm, gla_w_out, kv_norm, w_kv,
           dsa_w_q, dsa_w_out, ffn_norm, ffn_w_up, ffn_conv_w, ffn_conv_b, ffn_w_down, final_norm):
    bsz, s_len, d = x.shape
    t = bsz * s_len
    h = x.reshape(t, d)

    n_main = 2 * GLA_KEY_DIM + 2 * GLA_VAL_DIM
    w_a1 = jnp.pad(gla_w_in[0, :, n_main:].astype(BF16), ((0, 0), (0, LANES - GATE_RANK)))
    w_a2 = jnp.pad(gla_w_a2[0].astype(BF16), ((0, LANES - GATE_RANK), (0, 0)))
    xn, a = _rmsnorm_gate_in(h, attn_norm[0], w_a1)
    proj = _matmul(xn, gla_w_in, 0, BF16, n_cols=n_main, tm=1024)
    o = _gla_recurrence(proj, a, w_a2, gla_b_a2[0], gla_head_norm[0], bsz, s_len)
    h, xn = _matmul_res_norm(o, gla_w_out, 0, h, ffn_norm[0])
    h = _conv_glu(h, xn, ffn_w_up, 0, ffn_conv_w[0], ffn_conv_b[0], ffn_w_down, s_len)

    perm_np = _perm_matrix()
    perm = jnp.asarray(perm_np, BF16)
    perm_t = jnp.asarray(perm_np.T, BF16)
    xkv, xq, xq_perm = _rmsnorm2_perm(h, kv_norm, attn_norm[1], perm)
    kv, kv_perm = _matmul_and_permuted(xkv, w_kv[None], 0, perm)

    q_scale = HEAD_DIM ** -0.5 * LOG2_E
    q0 = _matmul(xq, dsa_w_q, 0, BF16, n_cols=ATT_WIDTH, out_scale=q_scale, tm=1024)
    q12 = _matmul(xq_perm, dsa_w_q, 0, BF16, n_cols=2 * ATT_WIDTH, col_start=ATT_WIDTH,
                  out_scale=q_scale, tm=1024)
    o0, lse0 = _dsa_branch(q0, 0, kv, 0, bsz, s_len)
    o1, lse1 = _dsa_branch(q12, 0, kv_perm, 1, bsz, s_len)
    o2, lse2 = _dsa_branch(q12, 1, kv_perm, 2, bsz, s_len)
    o = _dsa_merge([o0, o1, o2], [lse0, lse1, lse2], perm_t)
    h, xn = _matmul_res_norm(o, dsa_w_out, 0, h, ffn_norm[1])
    h = _conv_glu(h, xn, ffn_w_up, 1, ffn_conv_w[1], ffn_conv_b[1], ffn_w_down, s_len)

    return _rmsnorm(h, final_norm, F32).reshape(bsz, s_len, d)
```

```python
import functools
import math

import numpy as np
import jax
import jax.numpy as jnp
from jax import lax
from jax.experimental import pallas as pl
from jax.experimental.pallas import tpu as pltpu

D_MODEL = 2048
GLA_HEADS = 4
GLA_KEY_DIM = 1024
GLA_VAL_DIM = 2048
GLA_DK = 256
GLA_DV = 512
GATE_RANK = 16
GATE_NORMALIZER = 16.0
GLA_CHUNK = 64
GLA_BLOCK = 256
GLA_HEADS_PER_STEP = 4
ATT_HEADS = 16
HEAD_DIM = 128
ATT_WIDTH = ATT_HEADS * HEAD_DIM
WINDOWS = (128, 512, 2048)
DILATIONS = (1, 4, 16)
ATT_BLOCK = 128
DSA_BLOCKS_PER_STEP = 4
QK_LOOKAHEAD = 2
PERM_TILE = 512
PERM_RUN = PERM_TILE // 16
D_FF = 5632
EPS = 1e-6
LOG2_E = math.log2(math.e)
LN_2 = math.log(2.0)
LANES = 128
FFN_ROW_SUB = 512
FFN_COL_SUB = 256
FFN_SLOTS = 2

F32 = jnp.float32
BF16 = jnp.bfloat16


def _alibi_slopes(n):
    def pow2_slopes(m):
        start = 2.0 ** (-8.0 / m)
        return [start ** (i + 1) for i in range(m)]
    assert math.log2(n).is_integer()
    return [float(v) for v in np.array(pow2_slopes(n), dtype=np.float32)]


def _rmsnorm_kernel(x_ref, g_ref, o_ref):
    x = x_ref[...]
    ms = jnp.mean(x * x, axis=-1, keepdims=True)
    o_ref[...] = ((x * lax.rsqrt(ms + EPS)) * g_ref[...]).astype(o_ref.dtype)


def _rmsnorm(x, g, out_dtype, tm=512):
    m, d = x.shape
    return pl.pallas_call(
        _rmsnorm_kernel,
        grid=(m // tm,),
        in_specs=[pl.BlockSpec((tm, d), lambda i: (i, 0)),
                  pl.BlockSpec((1, d), lambda i: (0, 0))],
        out_specs=pl.BlockSpec((tm, d), lambda i: (i, 0)),
        out_shape=jax.ShapeDtypeStruct((m, d), out_dtype),
        compiler_params=pltpu.CompilerParams(dimension_semantics=("arbitrary",)),
        name="rmsnorm",
    )(x, g.reshape(1, d))


def _cast_weights_once(w_ref, wb_ref):
    @pl.when(pl.program_id(1) == 0)
    def _():
        wb_ref[...] = w_ref[...].astype(BF16)


def _mm_kernel(x_ref, w_ref, *rest, out_scale, has_residual, cast_weights):
    o_ref = rest[1 if has_residual else 0]
    if cast_weights:
        _cast_weights_once(w_ref, rest[-1])
        w_ref = rest[-1]
    acc = jnp.dot(x_ref[...], w_ref[...], preferred_element_type=F32)
    if has_residual:
        acc = rest[0][...] + acc
    if out_scale is not None:
        acc = acc * out_scale
    o_ref[...] = acc.astype(o_ref.dtype)


def _matmul(x, w, layer, out_dtype, n_cols=None, col_start=0, residual=None, out_scale=None,
            tm=512, tn=1024, weight_buffers=2):
    m, k = x.shape
    cast_weights = w.dtype != BF16
    n = w.shape[2] if n_cols is None else n_cols
    assert w.shape[1] == k and m % tm == 0 and n % tn == 0 and col_start % tn == 0
    j0 = col_start // tn
    in_specs = [pl.BlockSpec((tm, k), lambda j, i: (i, 0)),
                pl.BlockSpec((None, k, tn), lambda j, i: (layer, 0, j0 + j),
                             pipeline_mode=pl.Buffered(weight_buffers))]
    args = [x, w]
    if residual is not None:
        in_specs.append(pl.BlockSpec((tm, tn), lambda j, i: (i, j)))
        args.append(residual)
    return pl.pallas_call(
        functools.partial(_mm_kernel, out_scale=out_scale, has_residual=residual is not None,
                          cast_weights=cast_weights),
        grid=(n // tn, m // tm),
        in_specs=in_specs,
        out_specs=pl.BlockSpec((tm, tn), lambda j, i: (i, j)),
        out_shape=jax.ShapeDtypeStruct((m, n), out_dtype),
        scratch_shapes=[pltpu.VMEM((k, tn), BF16)] if cast_weights else [],
        compiler_params=pltpu.CompilerParams(
            dimension_semantics=("arbitrary", "arbitrary")),
        name="matmul",
    )(*args)


def _mm_perm_kernel(x_ref, w_ref, p_ref, o_ref, op_ref, wb_ref):
    _cast_weights_once(w_ref, wb_ref)
    out = jnp.dot(x_ref[...], wb_ref[...], preferred_element_type=F32).astype(BF16)
    o_ref[...] = out
    for t in range(o_ref.shape[0] // PERM_TILE):
        rows = slice(t * PERM_TILE, (t + 1) * PERM_TILE)
        op_ref[rows, :] = jnp.dot(p_ref[...], out[rows], preferred_element_type=F32).astype(BF16)


def _matmul_and_permuted(x, w, layer, perm, tm=1024, tn=1024):
    m, k = x.shape
    n = w.shape[2]
    assert w.shape[1] == k and m % tm == 0 and n % tn == 0 and tm % PERM_TILE == 0
    out_spec = pl.BlockSpec((tm, tn), lambda j, i: (i, j))
    out = jax.ShapeDtypeStruct((m, n), BF16)
    return pl.pallas_call(
        _mm_perm_kernel,
        grid=(n // tn, m // tm),
        in_specs=[pl.BlockSpec((tm, k), lambda j, i: (i, 0)),
                  pl.BlockSpec((None, k, tn), lambda j, i: (layer, 0, j)),
                  pl.BlockSpec((PERM_TILE, PERM_TILE), lambda j, i: (0, 0))],
        out_specs=[out_spec, out_spec],
        out_shape=[out, out],
        scratch_shapes=[pltpu.VMEM((k, tn), BF16)],
        compiler_params=pltpu.CompilerParams(
            dimension_semantics=("arbitrary", "arbitrary")),
        name="matmul_and_permuted",
    )(x, w, perm)


def _mm_res_norm_kernel(x_ref, w_ref, r_ref, g_ref, h_ref, xn_ref, wb_ref):
    @pl.when(pl.program_id(0) == 0)
    def _():
        wb_ref[...] = w_ref[...].astype(BF16)

    h = r_ref[...] + jnp.dot(x_ref[...], wb_ref[...], preferred_element_type=F32)
    h_ref[...] = h
    ms = jnp.mean(h * h, axis=-1, keepdims=True)
    xn_ref[...] = ((h * lax.rsqrt(ms + EPS)) * g_ref[...]).astype(xn_ref.dtype)


def _matmul_res_norm(x, w, layer, residual, norm_g, tm=512):
    m, k = x.shape
    n = w.shape[2]
    assert w.shape[1] == k and residual.shape == (m, n) and m % tm == 0
    row = pl.BlockSpec((tm, n), lambda i: (i, 0))
    return pl.pallas_call(
        _mm_res_norm_kernel,
        grid=(m // tm,),
        in_specs=[pl.BlockSpec((tm, k), lambda i: (i, 0)),
                  pl.BlockSpec((None, k, n), lambda i: (layer, 0, 0),
                               pipeline_mode=pl.Buffered(1)),
                  row,
                  pl.BlockSpec((1, n), lambda i: (0, 0))],
        out_specs=[row, row],
        out_shape=[jax.ShapeDtypeStruct((m, n), F32), jax.ShapeDtypeStruct((m, n), BF16)],
        scratch_shapes=[pltpu.VMEM((k, n), BF16)],
        compiler_params=pltpu.CompilerParams(dimension_semantics=("arbitrary",)),
        name="matmul_res_norm",
    )(x, w, residual, norm_g.reshape(1, n))


def _norm2_perm_kernel(x_ref, ga_ref, gb_ref, p_ref, a_ref, b_ref, bp_ref):
    x = x_ref[...]
    ms = jnp.mean(x * x, axis=-1, keepdims=True)
    y = x * lax.rsqrt(ms + EPS)
    a_ref[...] = (y * ga_ref[...]).astype(BF16)
    b = (y * gb_ref[...]).astype(BF16)
    b_ref[...] = b
    bp_ref[...] = jnp.dot(p_ref[...], b, preferred_element_type=F32).astype(BF16)


def _rmsnorm2_perm(x, g_a, g_b, perm):
    m, d = x.shape
    tm = PERM_TILE
    row = pl.BlockSpec((tm, d), lambda i: (i, 0))
    vec = pl.BlockSpec((1, d), lambda i: (0, 0))
    out = jax.ShapeDtypeStruct((m, d), BF16)
    return pl.pallas_call(
        _norm2_perm_kernel,
        grid=(m // tm,),
        in_specs=[row, vec, vec, pl.BlockSpec((tm, tm), lambda i: (0, 0))],
        out_specs=[row, row, row],
        out_shape=[out, out, out],
        compiler_params=pltpu.CompilerParams(dimension_semantics=("arbitrary",)),
        name="rmsnorm2_perm",
    )(x, g_a.reshape(1, d), g_b.reshape(1, d), perm)


def _norm_gate_in_kernel(x_ref, g_ref, w1_ref, xn_ref, a_ref):
    x = x_ref[...]
    ms = jnp.mean(x * x, axis=-1, keepdims=True)
    xn = ((x * lax.rsqrt(ms + EPS)) * g_ref[...]).astype(BF16)
    xn_ref[...] = xn
    a_ref[...] = jnp.dot(xn, w1_ref[...], preferred_element_type=F32).astype(BF16)


def _rmsnorm_gate_in(x, g, w_a1, tm=512):
    m, d = x.shape
    return pl.pallas_call(
        _norm_gate_in_kernel,
        grid=(m // tm,),
        in_specs=[pl.BlockSpec((tm, d), lambda i: (i, 0)),
                  pl.BlockSpec((1, d), lambda i: (0, 0)),
                  pl.BlockSpec((d, LANES), lambda i: (0, 0))],
        out_specs=[pl.BlockSpec((tm, d), lambda i: (i, 0)),
                   pl.BlockSpec((tm, LANES), lambda i: (i, 0))],
        out_shape=[jax.ShapeDtypeStruct((m, d), BF16), jax.ShapeDtypeStruct((m, LANES), BF16)],
        compiler_params=pltpu.CompilerParams(dimension_semantics=("arbitrary",)),
        name="rmsnorm_gate_in",
    )(x, g.reshape(1, d), w_a1)


def _split3(x):
    hi = x.astype(BF16)
    r1 = x - hi.astype(F32)
    mid = r1.astype(BF16)
    lo = (r1 - mid.astype(F32)).astype(BF16)
    return hi, mid, lo


def _dot01(mat01, x):
    hi, mid, lo = _split3(x)
    return (jnp.dot(mat01, hi, preferred_element_type=F32)
            + jnp.dot(mat01, mid, preferred_element_type=F32)
            + jnp.dot(mat01, lo, preferred_element_type=F32))


def _gla_kernel(q_ref, k_ref, v_ref, r_ref, a_ref, w2_ref, b2_ref, hn_ref, o_ref, st_ref):
    @pl.when(pl.program_id(2) == 0)
    def _():
        st_ref[...] = jnp.zeros_like(st_ref)

    nb = GLA_BLOCK
    heads = range(GLA_HEADS_PER_STEP)
    dk = lambda h: slice(h * GLA_DK, (h + 1) * GLA_DK)
    dv = lambda h: slice(h * GLA_DV, (h + 1) * GLA_DV)
    row = lax.broadcasted_iota(jnp.int32, (nb, nb), 0)
    col = lax.broadcasted_iota(jnp.int32, (nb, nb), 1)
    same_chunk = (row // GLA_CHUNK) == (col // GLA_CHUNK)
    causal = same_chunk & (col <= row)
    tri = jnp.where(causal, 1.0, 0.0).astype(BF16)
    nt = (((1,), (1,)), ((), ()))
    tn = (((0,), (0,)), ((), ()))

    a = a_ref[...]
    cum = []
    for h in heads:
        z = jnp.dot(a, w2_ref[:, dk(h)], preferred_element_type=F32) + b2_ref[:, dk(h)]
        log_sig = jnp.minimum(z, 0.0) - jnp.log(1.0 + jnp.exp(-jnp.abs(z)))
        log2_alpha = log_sig * (LOG2_E / GATE_NORMALIZER)
        cum.append(_dot01(tri, log2_alpha))
    k = [k_ref[:, dk(h)].astype(F32) for h in heads]
    q_dec = [((q_ref[:, dk(h)].astype(F32) * (GLA_DK ** -0.5)) * jnp.exp2(cum[h])).astype(BF16)
             for h in heads]
    k_inv = [(k[h] * jnp.exp2(-cum[h])).astype(BF16) for h in heads]
    scores = [lax.dot_general(q_dec[h], k_inv[h], nt, preferred_element_type=F32)
              for h in heads]
    scores = [jnp.where(causal, s, 0.0).astype(BF16) for s in scores]
    o_intra = [jnp.dot(scores[h], v_ref[:, dv(h)], preferred_element_type=F32)
               for h in heads]

    o_parts = [[] for _ in heads]
    for c in range(nb // GLA_CHUNK):
        rows = slice(c * GLA_CHUNK, (c + 1) * GLA_CHUNK)
        for h in heads:
            last_c = cum[h][(c + 1) * GLA_CHUNK - 1:(c + 1) * GLA_CHUNK]
            k_end = (k[h][rows] * jnp.exp2(last_c - cum[h][rows])).astype(BF16)
            st = st_ref[h]
            o_inter = lax.dot_general(q_dec[h][rows], st.astype(BF16), nt,
                                      preferred_element_type=F32)
            upd = lax.dot_general(v_ref[rows, dv(h)], k_end, tn,
                                  preferred_element_type=F32)
            st_ref[h] = st * jnp.exp2(last_c) + upd
            o_parts[h].append(o_intra[h][rows] + o_inter)

    for h in heads:
        o = jnp.concatenate(o_parts[h], axis=0)
        ms = jnp.mean(o * o, axis=-1, keepdims=True)
        o = (o * lax.rsqrt(ms + EPS)) * hn_ref[...]
        r = r_ref[:, dv(h)].astype(F32)
        gate = r * (1.0 / (1.0 + jnp.exp(-r)))
        o_ref[:, dv(h)] = (o * gate).astype(o_ref.dtype)


def _gla_recurrence(proj, a, w_a2, b_a2, head_norm, bsz, s_len):
    t = bsz * s_len
    nblk = s_len // GLA_BLOCK
    hps = GLA_HEADS_PER_STEP
    wk, wv = hps * GLA_DK, hps * GLA_DV
    k_off = GLA_KEY_DIM // wk
    v_off = 2 * GLA_KEY_DIM // wv
    r_off = (2 * GLA_KEY_DIM + GLA_VAL_DIM) // wv
    row = lambda b, g, i: b * nblk + i
    return pl.pallas_call(
        _gla_kernel,
        grid=(bsz, GLA_HEADS // hps, nblk),
        in_specs=[
            pl.BlockSpec((GLA_BLOCK, wk), lambda b, g, i: (row(b, g, i), g)),
            pl.BlockSpec((GLA_BLOCK, wk), lambda b, g, i: (row(b, g, i), k_off + g)),
            pl.BlockSpec((GLA_BLOCK, wv), lambda b, g, i: (row(b, g, i), v_off + g)),
            pl.BlockSpec((GLA_BLOCK, wv), lambda b, g, i: (row(b, g, i), r_off + g)),
            pl.BlockSpec((GLA_BLOCK, LANES), lambda b, g, i: (row(b, g, i), 0)),
            pl.BlockSpec((LANES, wk), lambda b, g, i: (0, g)),
            pl.BlockSpec((1, wk), lambda b, g, i: (0, g)),
            pl.BlockSpec((1, GLA_DV), lambda b, g, i: (0, 0)),
        ],
        out_specs=pl.BlockSpec((GLA_BLOCK, wv), lambda b, g, i: (row(b, g, i), g)),
        out_shape=jax.ShapeDtypeStruct((t, GLA_VAL_DIM), BF16),
        scratch_shapes=[pltpu.VMEM((hps, GLA_DV, GLA_DK), F32)],
        compiler_params=pltpu.CompilerParams(
            dimension_semantics=("arbitrary", "arbitrary", "arbitrary")),
        name="gla_recurrence",
    )(proj, proj, proj, proj, a, w_a2, b_a2.reshape(1, GLA_KEY_DIM), head_norm.reshape(1, GLA_DV))


def _perm_matrix():
    p = np.zeros((PERM_TILE, PERM_TILE), np.float32)
    for i in range(PERM_RUN):
        for a in range(4):
            for r4 in range(4):
                p[(4 * r4 + a) * PERM_RUN + i, 16 * i + 4 * a + r4] = 1.0
    return p


def _dsa_lead(layout, s):
    if layout == "rows":
        return (slice(s * ATT_BLOCK, (s + 1) * ATT_BLOCK),)
    if layout == "tile":
        return (s, slice(None), slice(None))
    assert layout == "tiles4"
    return (slice(4 * s, 4 * s + 4), slice(None))


def _dsa_block_shape(layout, width):
    return (ATT_BLOCK, width) if layout == "rows" else (4, PERM_RUN, width)


def _dsa_kernel(q_ref, kc_ref, vc_ref, o_ref, lse_ref, kp_ref, vp_ref, bias_ref, *, dil, layout,
                qb, slopes):
    n = pl.program_id(2)
    nk = 2 * ATT_BLOCK
    ck = lax.broadcasted_iota(jnp.int32, (ATT_BLOCK, nk), 1)
    is_cur = ck >= ATT_BLOCK

    @pl.when(n == 0)
    def _():
        kp_ref[...] = jnp.zeros_like(kp_ref)
        vp_ref[...] = jnp.zeros_like(vp_ref)

    @pl.when((pl.program_id(0) == 0) & (pl.program_id(1) == 0) & (n == 0))
    def _():
        rq = lax.broadcasted_iota(jnp.int32, (ATT_BLOCK, nk), 0)
        ck_in = ck & (ATT_BLOCK - 1)
        if layout == "tile":
            sub_q = 4 * (rq & (PERM_RUN - 1)) + (rq >> 5)
            sub_k = 4 * (ck_in & (PERM_RUN - 1)) + (ck_in >> 5)
        else:
            sub_q, sub_k = rq, ck_in
        j = sub_q - sub_k + jnp.where(is_cur, 0, ATT_BLOCK)
        valid = (j >= 0) & (j <= ATT_BLOCK)
        neg_dist = jnp.where(valid, -(j * dil).astype(F32), -jnp.inf)
        for h in range(ATT_HEADS):
            bias_ref[h] = (slopes[h] * LOG2_E) * neg_dist

    has_prev = is_cur | (n > 0)

    lane = lax.broadcasted_iota(jnp.int32, (ATT_BLOCK, LANES), 1)
    ones = jnp.ones((nk, HEAD_DIM), BF16)
    nt = (((1,), (1,)), ((), ()))

    def head(ref, s, h):
        x = ref[_dsa_lead(layout, s) + (slice(h * HEAD_DIM, (h + 1) * HEAD_DIM),)]
        return x.reshape(ATT_BLOCK, HEAD_DIM)

    def prev_head(ref, carry_ref, s, h):
        if s == 0:
            return carry_ref[:, h * HEAD_DIM:(h + 1) * HEAD_DIM]
        return head(ref, s - 1, h)

    for s in range(qb):
        lead = _dsa_lead(layout, s)

        def scores(h, s=s):
            kcat = jnp.concatenate([prev_head(kc_ref, kp_ref, s, h), head(kc_ref, s, h)], axis=0)
            return lax.dot_general(head(q_ref, s, h), kcat, nt, preferred_element_type=F32)

        pending = [scores(h) for h in range(QK_LOOKAHEAD)]
        lse_tile = jnp.zeros((ATT_BLOCK, LANES), F32)
        for h in range(ATT_HEADS):
            sc = pending.pop(0)
            if h + QK_LOOKAHEAD < ATT_HEADS:
                pending.append(scores(h + QK_LOOKAHEAD))
            bias = bias_ref[h]
            if s == 0:
                bias = jnp.where(has_prev, bias, -jnp.inf)
            z = sc + bias
            m = jnp.max(z, axis=-1, keepdims=True)
            p = jnp.exp2(z - m).astype(BF16)
            vcat = jnp.concatenate([prev_head(vc_ref, vp_ref, s, h), head(vc_ref, s, h)], axis=0)
            acc = jnp.dot(p, jnp.concatenate([vcat, ones], axis=1),
                          preferred_element_type=F32)
            l = acc[:, HEAD_DIM:]
            o = (acc[:, :HEAD_DIM] / l).astype(o_ref.dtype)
            o_ref[lead + (slice(h * HEAD_DIM, (h + 1) * HEAD_DIM),)] = o.reshape(
                _dsa_block_shape(layout, HEAD_DIM))
            lse_tile = jnp.where(lane == h, m * LN_2 + jnp.log(l), lse_tile)
        lse_ref[lead + (slice(None),)] = lse_tile.reshape(_dsa_block_shape(layout, LANES))

    last = _dsa_lead(layout, qb - 1) + (slice(None),)
    kp_ref[...] = kc_ref[last].reshape(kp_ref.shape)
    vp_ref[...] = vc_ref[last].reshape(vp_ref.shape)


def _dsa_branch(q, q_col, kv, g, bsz, s_len):
    d = DILATIONS[g]
    assert WINDOWS[g] // d == ATT_BLOCK
    t = bsz * s_len
    ntile = s_len // PERM_TILE
    qb = min(DSA_BLOCKS_PER_STEP, s_len // d // ATT_BLOCK)
    if g == 0:
        layout = "rows"
        view = lambda x: x.reshape(bsz, s_len, x.shape[-1])
        blk = lambda w: (None, qb * ATT_BLOCK, w)
        grid = (bsz, 1, s_len // (qb * ATT_BLOCK))
        at = lambda c: (lambda b, r, n: (b, n, c))
    elif g == 1:
        layout = "tile"
        view = lambda x: x.reshape(bsz, ntile, 4, 4, PERM_RUN, x.shape[-1])
        blk = lambda w: (None, qb, None, 4, PERM_RUN, w)
        grid = (bsz, 4, ntile // qb)
        at = lambda c: (lambda b, r, n: (b, n, r, 0, 0, c))
    else:
        layout = "tiles4"
        view = lambda x: x.reshape(bsz, ntile, 16, PERM_RUN, x.shape[-1])
        blk = lambda w: (None, 4 * qb, None, PERM_RUN, w)
        grid = (bsz, 16, ntile // (4 * qb))
        at = lambda c: (lambda b, r, n: (b, n, r, 0, c))
    qv, kvv = view(q), view(kv)
    o, lse = pl.pallas_call(
        functools.partial(_dsa_kernel, dil=d, layout=layout, qb=qb,
                          slopes=_alibi_slopes(ATT_HEADS)),
        grid=grid,
        in_specs=[
            pl.BlockSpec(blk(ATT_WIDTH), at(q_col)),
            pl.BlockSpec(blk(ATT_WIDTH), at(0)),
            pl.BlockSpec(blk(ATT_WIDTH), at(1)),
        ],
        out_specs=[pl.BlockSpec(blk(ATT_WIDTH), at(0)),
                   pl.BlockSpec(blk(LANES), at(0))],
        out_shape=[jax.ShapeDtypeStruct(qv.shape[:-1] + (ATT_WIDTH,), BF16),
                   jax.ShapeDtypeStruct(qv.shape[:-1] + (LANES,), F32)],
        scratch_shapes=[pltpu.VMEM((ATT_BLOCK, ATT_WIDTH), BF16),
                        pltpu.VMEM((ATT_BLOCK, ATT_WIDTH), BF16),
                        pltpu.VMEM((ATT_HEADS, ATT_BLOCK, 2 * ATT_BLOCK), F32)],
        compiler_params=pltpu.CompilerParams(
            dimension_semantics=("arbitrary", "arbitrary", "arbitrary")),
        name=f"dsa_branch{g}",
    )(qv, kvv, kvv)
    return o.reshape(t, ATT_WIDTH), lse.reshape(t, LANES)


def _merge_kernel(pt_ref, o0_ref, o1_ref, o2_ref, l0_ref, l1_ref, l2_ref, o_ref):
    pt = pt_ref[...]
    l0 = l0_ref[...]
    l12 = _dot01(pt, jnp.concatenate([l1_ref[...], l2_ref[...]], axis=1))
    l1, l2 = l12[:, :LANES], l12[:, LANES:]
    m = jnp.maximum(jnp.maximum(l0, l1), l2)
    e0, e1, e2 = jnp.exp(l0 - m), jnp.exp(l1 - m), jnp.exp(l2 - m)
    den = e0 + e1 + e2
    w1, w2 = e1 / den, e2 / den
    for hp in range(ATT_HEADS // 2):
        pair = slice(2 * hp * HEAD_DIM, (2 * hp + 2) * HEAD_DIM)
        o1 = jnp.dot(pt, o1_ref[:, pair], preferred_element_type=F32)
        o2 = jnp.dot(pt, o2_ref[:, pair], preferred_element_type=F32)
        for k in range(2):
            h = 2 * hp + k
            sl = slice(h * HEAD_DIM, (h + 1) * HEAD_DIM)
            in_pair = slice(k * HEAD_DIM, (k + 1) * HEAD_DIM)
            o0 = o0_ref[:, sl].astype(F32)
            o = (o0 + w1[:, h:h + 1] * (o1[:, in_pair] - o0)
                 + w2[:, h:h + 1] * (o2[:, in_pair] - o0))
            o_ref[:, sl] = o.astype(o_ref.dtype)


def _dsa_merge(outs, lses, perm_t):
    t, hd = outs[0].shape
    tm = PERM_TILE
    o_spec = pl.BlockSpec((tm, hd), lambda i: (i, 0))
    l_spec = pl.BlockSpec((tm, LANES), lambda i: (i, 0))
    return pl.pallas_call(
        _merge_kernel,
        grid=(t // tm,),
        in_specs=[pl.BlockSpec((tm, tm), lambda i: (0, 0)),
                  o_spec, o_spec, o_spec, l_spec, l_spec, l_spec],
        out_specs=o_spec,
        out_shape=jax.ShapeDtypeStruct((t, hd), BF16),
        compiler_params=pltpu.CompilerParams(dimension_semantics=("arbitrary",)),
        name="dsa_merge",
    )(perm_t, *outs, *lses)


def _ffn_up_kernel(x_ref, wu_ref, wg_ref, cw_ref, cb_ref, o_ref, wub_ref, wgb_ref,
                   us_ref, gs_ref, carry_ref, *, tm, tn, s_len):
    @pl.when(pl.program_id(1) == 0)
    def _():
        wub_ref[...] = wu_ref[...].astype(BF16)
        wgb_ref[...] = wg_ref[...].astype(BF16)

    @pl.when((pl.program_id(1) * tm) % s_len == 0)
    def _():
        carry_ref[...] = jnp.zeros_like(carry_ref)

    n_rb = tm // FFN_ROW_SUB
    subs = [(cb, rb) for cb in range(tn // FFN_COL_SUB) for rb in range(n_rb)]

    def window(idx):
        cb, rb = subs[idx]
        return (slice(rb * FFN_ROW_SUB, (rb + 1) * FFN_ROW_SUB),
                slice(cb * FFN_COL_SUB, (cb + 1) * FFN_COL_SUB))

    def project(idx):
        rows, cols = window(idx)
        slot = idx % FFN_SLOTS
        xr = x_ref[rows, :]
        us_ref[slot] = jnp.dot(xr, wub_ref[:, cols], preferred_element_type=F32)
        gs_ref[slot, 8:] = jnp.dot(xr, wgb_ref[:, cols], preferred_element_type=F32)
        if subs[idx][1] == 0:
            gs_ref[slot, :8] = carry_ref[:, cols]
        else:
            gs_ref[slot, :8] = gs_ref[(idx - 1) % FFN_SLOTS, FFN_ROW_SUB:]
        if subs[idx][1] == n_rb - 1:
            carry_ref[:, cols] = gs_ref[slot, FFN_ROW_SUB:]

    def epilogue(idx):
        rows, cols = window(idx)
        slot = idx % FFN_SLOTS
        g = gs_ref[slot, 8:]
        g1 = gs_ref[slot, 7:7 + FFN_ROW_SUB]
        g2 = gs_ref[slot, 6:6 + FFN_ROW_SUB]
        cw = 0.5 * cw_ref[:, cols]
        half = cw[0:1] * g2 + cw[1:2] * g1 + cw[2:3] * g + 0.5 * cb_ref[:, cols]
        act = half * (1.0 + lax.erf(half * (2.0 ** 0.5)))
        o_ref[rows, cols] = (act * us_ref[slot]).astype(o_ref.dtype)

    ahead = FFN_SLOTS - 1
    for idx in range(ahead):
        project(idx)
    for idx in range(len(subs)):
        if idx + ahead < len(subs):
            project(idx + ahead)
        epilogue(idx)


def _ffn_up(xn, w_up, layer, conv_w, conv_b, s_len, tm=2048, tn=512):
    m, k = xn.shape
    nn = D_FF // tn
    assert s_len % tm == 0 and tm % FFN_ROW_SUB == 0 and tn % FFN_COL_SUB == 0
    return pl.pallas_call(
        functools.partial(_ffn_up_kernel, tm=tm, tn=tn, s_len=s_len),
        grid=(nn, m // tm),
        in_specs=[
            pl.BlockSpec((tm, k), lambda j, i: (i, 0)),
            pl.BlockSpec((None, k, tn), lambda j, i: (layer, 0, j)),
            pl.BlockSpec((None, k, tn), lambda j, i: (layer, 0, nn + j)),
            pl.BlockSpec((3, tn), lambda j, i: (0, j)),
            pl.BlockSpec((1, tn), lambda j, i: (0, j)),
        ],
        out_specs=pl.BlockSpec((tm, tn), lambda j, i: (i, j)),
        out_shape=jax.ShapeDtypeStruct((m, D_FF), BF16),
        scratch_shapes=[pltpu.VMEM((k, tn), BF16), pltpu.VMEM((k, tn), BF16),
                        pltpu.VMEM((FFN_SLOTS, FFN_ROW_SUB, FFN_COL_SUB), F32),
                        pltpu.VMEM((FFN_SLOTS, 8 + FFN_ROW_SUB, FFN_COL_SUB), F32),
                        pltpu.VMEM((8, tn), F32)],
        compiler_params=pltpu.CompilerParams(
            dimension_semantics=("arbitrary", "arbitrary")),
        name="ffn_up",
    )(xn, w_up, w_up, conv_w, conv_b.reshape(1, D_FF))


def _conv_glu(h, xn, w_up, layer, conv_w, conv_b, w_down, s_len):
    act = _ffn_up(xn, w_up, layer, conv_w, conv_b, s_len)
    return _matmul(act, w_down, layer, F32, residual=h, tn=1024, weight_buffers=1)


def kernel(x, attn_norm, gla_w_in, gla_w_a2, gla_b_a2, gla_head_norm, gla_w_out, kv_norm, w_kv,
           dsa_w_q, dsa_w_out, ffn_norm, ffn_w_up, ffn_conv_w, ffn_conv_b, ffn_w_down, final_norm):
    bsz, s_len, d = x.shape
    t = bsz * s_len
    h = x.reshape(t, d)

    n_main = 2 * GLA_KEY_DIM + 2 * GLA_VAL_DIM
    w_in = gla_w_in.astype(BF16)
    w_a1 = jnp.pad(w_in[0, :, n_main:], ((0, 0), (0, LANES - GATE_RANK)))
    w_a2 = jnp.pad(gla_w_a2[0].astype(BF16), ((0, LANES - GATE_RANK), (0, 0)))
    xn, a = _rmsnorm_gate_in(h, attn_norm[0], w_a1)
    proj = _matmul(xn, w_in, 0, BF16, n_cols=n_main, tm=1024)
    o = _gla_recurrence(proj, a, w_a2, gla_b_a2[0], gla_head_norm[0], bsz, s_len)
    h, xn = _matmul_res_norm(o, gla_w_out, 0, h, ffn_norm[0])
    h = _conv_glu(h, xn, ffn_w_up, 0, ffn_conv_w[0], ffn_conv_b[0], ffn_w_down, s_len)

    perm_np = _perm_matrix()
    perm = jnp.asarray(perm_np, BF16)
    perm_t = jnp.asarray(perm_np.T, BF16)
    xkv, xq, xq_perm = _rmsnorm2_perm(h, kv_norm, attn_norm[1], perm)
    kv, kv_perm = _matmul_and_permuted(xkv, w_kv[None], 0, perm)

    q_scale = HEAD_DIM ** -0.5 * LOG2_E
    q0 = _matmul(xq, dsa_w_q, 0, BF16, n_cols=ATT_WIDTH, out_scale=q_scale, tm=1024)
    q12 = _matmul(xq_perm, dsa_w_q, 0, BF16, n_cols=2 * ATT_WIDTH, col_start=ATT_WIDTH,
                  out_scale=q_scale, tm=1024)
    o0, lse0 = _dsa_branch(q0, 0, kv, 0, bsz, s_len)
    o1, lse1 = _dsa_branch(q12, 0, kv_perm, 1, bsz, s_len)
    o2, lse2 = _dsa_branch(q12, 1, kv_perm, 2, bsz, s_len)
    o = _dsa_merge([o0, o1, o2], [lse0, lse1, lse2], perm_t)
    h, xn = _matmul_res_norm(o, dsa_w_out, 0, h, ffn_norm[1])
    h = _conv_glu(h, xn, ffn_w_up, 1, ffn_conv_w[1], ffn_conv_b[1], ffn_w_down, s_len)

    return _rmsnorm(h, final_norm, F32).reshape(bsz, s_len, d)
```

```python
import functools
import math

import numpy as np
import jax
import jax.numpy as jnp
from jax import lax
from jax.experimental import pallas as pl
from jax.experimental.pallas import tpu as pltpu

D_MODEL = 2048
GLA_HEADS = 4
GLA_KEY_DIM = 1024
GLA_VAL_DIM = 2048
GLA_DK = 256
GLA_DV = 512
GATE_RANK = 16
GATE_NORMALIZER = 16.0
GLA_CHUNK = 64
GLA_BLOCK = 256
GLA_HEADS_PER_STEP = 4
ATT_HEADS = 16
HEAD_DIM = 128
ATT_WIDTH = ATT_HEADS * HEAD_DIM
WINDOWS = (128, 512, 2048)
DILATIONS = (1, 4, 16)
ATT_BLOCK = 128
DSA_BLOCKS_PER_STEP = 4
QK_LOOKAHEAD = 2
PERM_TILE = 512
PERM_RUN = PERM_TILE // 16
PERM_HALF = PERM_TILE // 2
HALF_RUN = PERM_RUN // 2
D_FF = 5632
EPS = 1e-6
LOG2_E = math.log2(math.e)
LN_2 = math.log(2.0)
LANES = 128
FFN_ROW_SUB = 512
FFN_COL_SUB = 256
FFN_SLOTS = 2

F32 = jnp.float32
BF16 = jnp.bfloat16


def _alibi_slopes(n):
    def pow2_slopes(m):
        start = 2.0 ** (-8.0 / m)
        return [start ** (i + 1) for i in range(m)]
    assert math.log2(n).is_integer()
    return [float(v) for v in np.array(pow2_slopes(n), dtype=np.float32)]


def _rmsnorm_kernel(x_ref, g_ref, o_ref):
    x = x_ref[...]
    ms = jnp.mean(x * x, axis=-1, keepdims=True)
    o_ref[...] = ((x * lax.rsqrt(ms + EPS)) * g_ref[...]).astype(o_ref.dtype)


def _rmsnorm(x, g, out_dtype, tm=512):
    m, d = x.shape
    return pl.pallas_call(
        _rmsnorm_kernel,
        grid=(m // tm,),
        in_specs=[pl.BlockSpec((tm, d), lambda i: (i, 0)),
                  pl.BlockSpec((1, d), lambda i: (0, 0))],
        out_specs=pl.BlockSpec((tm, d), lambda i: (i, 0)),
        out_shape=jax.ShapeDtypeStruct((m, d), out_dtype),
        compiler_params=pltpu.CompilerParams(dimension_semantics=("arbitrary",)),
        name="rmsnorm",
    )(x, g.reshape(1, d))


def _cast_weights_once(w_ref, wb_ref):
    @pl.when(pl.program_id(1) == 0)
    def _():
        wb_ref[...] = w_ref[...].astype(BF16)


def _mm_kernel(x_ref, w_ref, *rest, out_scale, has_residual, cast_weights):
    o_ref = rest[1 if has_residual else 0]
    if cast_weights:
        _cast_weights_once(w_ref, rest[-1])
        w_ref = rest[-1]
    acc = jnp.dot(x_ref[...], w_ref[...], preferred_element_type=F32)
    if has_residual:
        acc = rest[0][...] + acc
    if out_scale is not None:
        acc = acc * out_scale
    o_ref[...] = acc.astype(o_ref.dtype)


def _matmul(x, w, layer, out_dtype, n_cols=None, col_start=0, residual=None, out_scale=None,
            tm=512, tn=1024, weight_buffers=2):
    m, k = x.shape
    cast_weights = w.dtype != BF16
    n = w.shape[2] if n_cols is None else n_cols
    assert w.shape[1] == k and m % tm == 0 and n % tn == 0 and col_start % tn == 0
    j0 = col_start // tn
    in_specs = [pl.BlockSpec((tm, k), lambda j, i: (i, 0)),
                pl.BlockSpec((None, k, tn), lambda j, i: (layer, 0, j0 + j),
                             pipeline_mode=pl.Buffered(weight_buffers))]
    args = [x, w]
    if residual is not None:
        in_specs.append(pl.BlockSpec((tm, tn), lambda j, i: (i, j)))
        args.append(residual)
    return pl.pallas_call(
        functools.partial(_mm_kernel, out_scale=out_scale, has_residual=residual is not None,
                          cast_weights=cast_weights),
        grid=(n // tn, m // tm),
        in_specs=in_specs,
        out_specs=pl.BlockSpec((tm, tn), lambda j, i: (i, j)),
        out_shape=jax.ShapeDtypeStruct((m, n), out_dtype),
        scratch_shapes=[pltpu.VMEM((k, tn), BF16)] if cast_weights else [],
        compiler_params=pltpu.CompilerParams(
            dimension_semantics=("arbitrary", "arbitrary")),
        name="matmul",
    )(*args)


def _mm_perm_kernel(x_ref, w_ref, p_ref, o_ref, op_ref, wb_ref):
    _cast_weights_once(w_ref, wb_ref)
    out = jnp.dot(x_ref[...], wb_ref[...], preferred_element_type=F32).astype(BF16)
    o_ref[...] = out
    for t in range(o_ref.shape[0] // PERM_TILE):
        _store_permuted(p_ref[...], out[t * PERM_TILE:(t + 1) * PERM_TILE], op_ref, t * PERM_TILE)


def _matmul_and_permuted(x, w, layer, perm, tm=1024, tn=1024):
    m, k = x.shape
    n = w.shape[2]
    assert w.shape[1] == k and m % tm == 0 and n % tn == 0 and tm % PERM_TILE == 0
    out_spec = pl.BlockSpec((tm, tn), lambda j, i: (i, j))
    out = jax.ShapeDtypeStruct((m, n), BF16)
    return pl.pallas_call(
        _mm_perm_kernel,
        grid=(n // tn, m // tm),
        in_specs=[pl.BlockSpec((tm, k), lambda j, i: (i, 0)),
                  pl.BlockSpec((None, k, tn), lambda j, i: (layer, 0, j)),
                  pl.BlockSpec((PERM_HALF, PERM_HALF), lambda j, i: (0, 0))],
        out_specs=[out_spec, out_spec],
        out_shape=[out, out],
        scratch_shapes=[pltpu.VMEM((k, tn), BF16)],
        compiler_params=pltpu.CompilerParams(
            dimension_semantics=("arbitrary", "arbitrary")),
        name="matmul_and_permuted",
    )(x, w, perm)


def _mm_res_norm_kernel(x_ref, w_ref, r_ref, g_ref, h_ref, xn_ref, wb_ref):
    @pl.when(pl.program_id(0) == 0)
    def _():
        wb_ref[...] = w_ref[...].astype(BF16)

    h = r_ref[...] + jnp.dot(x_ref[...], wb_ref[...], preferred_element_type=F32)
    h_ref[...] = h
    ms = jnp.mean(h * h, axis=-1, keepdims=True)
    xn_ref[...] = ((h * lax.rsqrt(ms + EPS)) * g_ref[...]).astype(xn_ref.dtype)


def _matmul_res_norm(x, w, layer, residual, norm_g, tm=512):
    m, k = x.shape
    n = w.shape[2]
    assert w.shape[1] == k and residual.shape == (m, n) and m % tm == 0
    row = pl.BlockSpec((tm, n), lambda i: (i, 0))
    return pl.pallas_call(
        _mm_res_norm_kernel,
        grid=(m // tm,),
        in_specs=[pl.BlockSpec((tm, k), lambda i: (i, 0)),
                  pl.BlockSpec((None, k, n), lambda i: (layer, 0, 0),
                               pipeline_mode=pl.Buffered(1)),
                  row,
                  pl.BlockSpec((1, n), lambda i: (0, 0))],
        out_specs=[row, row],
        out_shape=[jax.ShapeDtypeStruct((m, n), F32), jax.ShapeDtypeStruct((m, n), BF16)],
        scratch_shapes=[pltpu.VMEM((k, n), BF16)],
        compiler_params=pltpu.CompilerParams(dimension_semantics=("arbitrary",)),
        name="matmul_res_norm",
    )(x, w, residual, norm_g.reshape(1, n))


def _norm2_perm_kernel(x_ref, ga_ref, gb_ref, p_ref, a_ref, b_ref, bp_ref):
    x = x_ref[...]
    ms = jnp.mean(x * x, axis=-1, keepdims=True)
    y = x * lax.rsqrt(ms + EPS)
    a_ref[...] = (y * ga_ref[...]).astype(BF16)
    b = (y * gb_ref[...]).astype(BF16)
    b_ref[...] = b
    _store_permuted(p_ref[...], b, bp_ref, 0)


def _rmsnorm2_perm(x, g_a, g_b, perm):
    m, d = x.shape
    tm = PERM_TILE
    row = pl.BlockSpec((tm, d), lambda i: (i, 0))
    vec = pl.BlockSpec((1, d), lambda i: (0, 0))
    out = jax.ShapeDtypeStruct((m, d), BF16)
    return pl.pallas_call(
        _norm2_perm_kernel,
        grid=(m // tm,),
        in_specs=[row, vec, vec, pl.BlockSpec((PERM_HALF, PERM_HALF), lambda i: (0, 0))],
        out_specs=[row, row, row],
        out_shape=[out, out, out],
        compiler_params=pltpu.CompilerParams(dimension_semantics=("arbitrary",)),
        name="rmsnorm2_perm",
    )(x, g_a.reshape(1, d), g_b.reshape(1, d), perm)


def _norm_gate_in_kernel(x_ref, g_ref, w1_ref, xn_ref, a_ref):
    x = x_ref[...]
    ms = jnp.mean(x * x, axis=-1, keepdims=True)
    xn = ((x * lax.rsqrt(ms + EPS)) * g_ref[...]).astype(BF16)
    xn_ref[...] = xn
    a_ref[...] = jnp.dot(xn, w1_ref[...], preferred_element_type=F32).astype(BF16)


def _rmsnorm_gate_in(x, g, w_a1, tm=512):
    m, d = x.shape
    return pl.pallas_call(
        _norm_gate_in_kernel,
        grid=(m // tm,),
        in_specs=[pl.BlockSpec((tm, d), lambda i: (i, 0)),
                  pl.BlockSpec((1, d), lambda i: (0, 0)),
                  pl.BlockSpec((d, LANES), lambda i: (0, 0))],
        out_specs=[pl.BlockSpec((tm, d), lambda i: (i, 0)),
                   pl.BlockSpec((tm, LANES), lambda i: (i, 0))],
        out_shape=[jax.ShapeDtypeStruct((m, d), BF16), jax.ShapeDtypeStruct((m, LANES), BF16)],
        compiler_params=pltpu.CompilerParams(dimension_semantics=("arbitrary",)),
        name="rmsnorm_gate_in",
    )(x, g.reshape(1, d), w_a1)


def _split3(x):
    hi = x.astype(BF16)
    r1 = x - hi.astype(F32)
    mid = r1.astype(BF16)
    lo = (r1 - mid.astype(F32)).astype(BF16)
    return hi, mid, lo


def _dot01(mat01, x):
    hi, mid, lo = _split3(x)
    return (jnp.dot(mat01, hi, preferred_element_type=F32)
            + jnp.dot(mat01, mid, preferred_element_type=F32)
            + jnp.dot(mat01, lo, preferred_element_type=F32))


def _gla_kernel(q_ref, k_ref, v_ref, r_ref, a_ref, w2_ref, b2_ref, hn_ref, o_ref, st_ref):
    @pl.when(pl.program_id(2) == 0)
    def _():
        st_ref[...] = jnp.zeros_like(st_ref)

    nb = GLA_BLOCK
    heads = range(GLA_HEADS_PER_STEP)
    dk = lambda h: slice(h * GLA_DK, (h + 1) * GLA_DK)
    dv = lambda h: slice(h * GLA_DV, (h + 1) * GLA_DV)
    row = lax.broadcasted_iota(jnp.int32, (nb, nb), 0)
    col = lax.broadcasted_iota(jnp.int32, (nb, nb), 1)
    same_chunk = (row // GLA_CHUNK) == (col // GLA_CHUNK)
    causal = same_chunk & (col <= row)
    tri = jnp.where(causal, 1.0, 0.0).astype(BF16)
    nt = (((1,), (1,)), ((), ()))
    tn = (((0,), (0,)), ((), ()))

    a = a_ref[...]
    cum = []
    for h in heads:
        z = jnp.dot(a, w2_ref[:, dk(h)], preferred_element_type=F32) + b2_ref[:, dk(h)]
        log_sig = jnp.minimum(z, 0.0) - jnp.log(1.0 + jnp.exp(-jnp.abs(z)))
        log2_alpha = log_sig * (LOG2_E / GATE_NORMALIZER)
        cum.append(_dot01(tri, log2_alpha))
    k = [k_ref[:, dk(h)].astype(F32) for h in heads]
    q_dec = [((q_ref[:, dk(h)].astype(F32) * (GLA_DK ** -0.5)) * jnp.exp2(cum[h])).astype(BF16)
             for h in heads]
    k_inv = [(k[h] * jnp.exp2(-cum[h])).astype(BF16) for h in heads]
    scores = [lax.dot_general(q_dec[h], k_inv[h], nt, preferred_element_type=F32)
              for h in heads]
    scores = [jnp.where(causal, s, 0.0).astype(BF16) for s in scores]
    o_intra = [jnp.dot(scores[h], v_ref[:, dv(h)], preferred_element_type=F32)
               for h in heads]

    o_parts = [[] for _ in heads]
    for c in range(nb // GLA_CHUNK):
        rows = slice(c * GLA_CHUNK, (c + 1) * GLA_CHUNK)
        for h in heads:
            last_c = cum[h][(c + 1) * GLA_CHUNK - 1:(c + 1) * GLA_CHUNK]
            k_end = (k[h][rows] * jnp.exp2(last_c - cum[h][rows])).astype(BF16)
            st = st_ref[h]
            o_inter = lax.dot_general(q_dec[h][rows], st.astype(BF16), nt,
                                      preferred_element_type=F32)
            upd = lax.dot_general(v_ref[rows, dv(h)], k_end, tn,
                                  preferred_element_type=F32)
            st_ref[h] = st * jnp.exp2(last_c) + upd
            o_parts[h].append(o_intra[h][rows] + o_inter)

    for h in heads:
        o = jnp.concatenate(o_parts[h], axis=0)
        ms = jnp.mean(o * o, axis=-1, keepdims=True)
        o = (o * lax.rsqrt(ms + EPS)) * hn_ref[...]
        r = r_ref[:, dv(h)].astype(F32)
        gate = r * (1.0 / (1.0 + jnp.exp(-r)))
        o_ref[:, dv(h)] = (o * gate).astype(o_ref.dtype)


def _gla_recurrence(proj, a, w_a2, b_a2, head_norm, bsz, s_len):
    t = bsz * s_len
    nblk = s_len // GLA_BLOCK
    hps = GLA_HEADS_PER_STEP
    wk, wv = hps * GLA_DK, hps * GLA_DV
    k_off = GLA_KEY_DIM // wk
    v_off = 2 * GLA_KEY_DIM // wv
    r_off = (2 * GLA_KEY_DIM + GLA_VAL_DIM) // wv
    row = lambda b, g, i: b * nblk + i
    return pl.pallas_call(
        _gla_kernel,
        grid=(bsz, GLA_HEADS // hps, nblk),
        in_specs=[
            pl.BlockSpec((GLA_BLOCK, wk), lambda b, g, i: (row(b, g, i), g)),
            pl.BlockSpec((GLA_BLOCK, wk), lambda b, g, i: (row(b, g, i), k_off + g)),
            pl.BlockSpec((GLA_BLOCK, wv), lambda b, g, i: (row(b, g, i), v_off + g)),
            pl.BlockSpec((GLA_BLOCK, wv), lambda b, g, i: (row(b, g, i), r_off + g)),
            pl.BlockSpec((GLA_BLOCK, LANES), lambda b, g, i: (row(b, g, i), 0)),
            pl.BlockSpec((LANES, wk), lambda b, g, i: (0, g)),
            pl.BlockSpec((1, wk), lambda b, g, i: (0, g)),
            pl.BlockSpec((1, GLA_DV), lambda b, g, i: (0, 0)),
        ],
        out_specs=pl.BlockSpec((GLA_BLOCK, wv), lambda b, g, i: (row(b, g, i), g)),
        out_shape=jax.ShapeDtypeStruct((t, GLA_VAL_DIM), BF16),
        scratch_shapes=[pltpu.VMEM((hps, GLA_DV, GLA_DK), F32)],
        compiler_params=pltpu.CompilerParams(
            dimension_semantics=("arbitrary", "arbitrary", "arbitrary")),
        name="gla_recurrence",
    )(proj, proj, proj, proj, a, w_a2, b_a2.reshape(1, GLA_KEY_DIM), head_norm.reshape(1, GLA_DV))


def _half_perm_matrix():
    q = np.zeros((PERM_HALF, PERM_HALF), np.float32)
    for i in range(HALF_RUN):
        for a in range(4):
            for r4 in range(4):
                q[(4 * r4 + a) * HALF_RUN + i, 16 * i + 4 * a + r4] = 1.0
    return q


def _store_permuted(q, x, out_ref, base):
    for hf in range(2):
        y = jnp.dot(q, x[hf * PERM_HALF:(hf + 1) * PERM_HALF],
                    preferred_element_type=F32).astype(out_ref.dtype)
        for grp in range(16):
            dst = base + grp * PERM_RUN + hf * HALF_RUN
            out_ref[dst:dst + HALF_RUN, :] = y[grp * HALF_RUN:(grp + 1) * HALF_RUN]


def _gather_half(ref, hf, cols):
    return jnp.concatenate(
        [ref[grp * PERM_RUN + hf * HALF_RUN:grp * PERM_RUN + (hf + 1) * HALF_RUN, cols]
         for grp in range(16)], axis=0)


def _dsa_lead(layout, s):
    if layout == "rows":
        return (slice(s * ATT_BLOCK, (s + 1) * ATT_BLOCK),)
    if layout == "tile":
        return (s, slice(None), slice(None))
    assert layout == "tiles4"
    return (slice(4 * s, 4 * s + 4), slice(None))


def _dsa_block_shape(layout, width):
    return (ATT_BLOCK, width) if layout == "rows" else (4, PERM_RUN, width)


def _dsa_kernel(q_ref, kc_ref, vc_ref, o_ref, lse_ref, kp_ref, vp_ref, bias_ref, *, dil, layout,
                qb, slopes):
    n = pl.program_id(2)
    nk = 2 * ATT_BLOCK
    ck = lax.broadcasted_iota(jnp.int32, (ATT_BLOCK, nk), 1)
    is_cur = ck >= ATT_BLOCK

    @pl.when(n == 0)
    def _():
        kp_ref[...] = jnp.zeros_like(kp_ref)
        vp_ref[...] = jnp.zeros_like(vp_ref)

    @pl.when((pl.program_id(0) == 0) & (pl.program_id(1) == 0) & (n == 0))
    def _():
        rq = lax.broadcasted_iota(jnp.int32, (ATT_BLOCK, nk), 0)
        ck_in = ck & (ATT_BLOCK - 1)
        if layout == "tile":
            sub_q = 4 * (rq & (PERM_RUN - 1)) + (rq >> 5)
            sub_k = 4 * (ck_in & (PERM_RUN - 1)) + (ck_in >> 5)
        else:
            sub_q, sub_k = rq, ck_in
        j = sub_q - sub_k + jnp.where(is_cur, 0, ATT_BLOCK)
        valid = (j >= 0) & (j <= ATT_BLOCK)
        neg_dist = jnp.where(valid, -(j * dil).astype(F32), -jnp.inf)
        for h in range(ATT_HEADS):
            bias_ref[h] = (slopes[h] * LOG2_E) * neg_dist

    has_prev = is_cur | (n > 0)

    lane = lax.broadcasted_iota(jnp.int32, (ATT_BLOCK, LANES), 1)
    ones = jnp.ones((nk, HEAD_DIM), BF16)
    nt = (((1,), (1,)), ((), ()))

    def head(ref, s, h):
        x = ref[_dsa_lead(layout, s) + (slice(h * HEAD_DIM, (h + 1) * HEAD_DIM),)]
        return x.reshape(ATT_BLOCK, HEAD_DIM)

    def prev_head(ref, carry_ref, s, h):
        if s == 0:
            return carry_ref[:, h * HEAD_DIM:(h + 1) * HEAD_DIM]
        return head(ref, s - 1, h)

    for s in range(qb):
        lead = _dsa_lead(layout, s)

        def scores(h, s=s):
            kcat = jnp.concatenate([prev_head(kc_ref, kp_ref, s, h), head(kc_ref, s, h)], axis=0)
            return lax.dot_general(head(q_ref, s, h), kcat, nt, preferred_element_type=F32)

        pending = [scores(h) for h in range(QK_LOOKAHEAD)]
        lse_tile = jnp.zeros((ATT_BLOCK, LANES), F32)
        for h in range(ATT_HEADS):
            sc = pending.pop(0)
            if h + QK_LOOKAHEAD < ATT_HEADS:
                pending.append(scores(h + QK_LOOKAHEAD))
            bias = bias_ref[h]
            if s == 0:
                bias = jnp.where(has_prev, bias, -jnp.inf)
            z = sc + bias
            m = jnp.max(z, axis=-1, keepdims=True)
            p = jnp.exp2(z - m).astype(BF16)
            vcat = jnp.concatenate([prev_head(vc_ref, vp_ref, s, h), head(vc_ref, s, h)], axis=0)
            acc = jnp.dot(p, jnp.concatenate([vcat, ones], axis=1),
                          preferred_element_type=F32)
            l = acc[:, HEAD_DIM:]
            o = (acc[:, :HEAD_DIM] / l).astype(o_ref.dtype)
            o_ref[lead + (slice(h * HEAD_DIM, (h + 1) * HEAD_DIM),)] = o.reshape(
                _dsa_block_shape(layout, HEAD_DIM))
            lse_tile = jnp.where(lane == h, m * LN_2 + jnp.log(l), lse_tile)
        lse_ref[lead + (slice(None),)] = lse_tile.reshape(_dsa_block_shape(layout, LANES))

    last = _dsa_lead(layout, qb - 1) + (slice(None),)
    kp_ref[...] = kc_ref[last].reshape(kp_ref.shape)
    vp_ref[...] = vc_ref[last].reshape(vp_ref.shape)


def _dsa_branch(q, q_col, kv, g, bsz, s_len):
    d = DILATIONS[g]
    assert WINDOWS[g] // d == ATT_BLOCK
    t = bsz * s_len
    ntile = s_len // PERM_TILE
    qb = min(DSA_BLOCKS_PER_STEP, s_len // d // ATT_BLOCK)
    if g == 0:
        layout = "rows"
        view = lambda x: x.reshape(bsz, s_len, x.shape[-1])
        blk = lambda w: (None, qb * ATT_BLOCK, w)
        grid = (bsz, 1, s_len // (qb * ATT_BLOCK))
        at = lambda c: (lambda b, r, n: (b, n, c))
    elif g == 1:
        layout = "tile"
        view = lambda x: x.reshape(bsz, ntile, 4, 4, PERM_RUN, x.shape[-1])
        blk = lambda w: (None, qb, None, 4, PERM_RUN, w)
        grid = (bsz, 4, ntile // qb)
        at = lambda c: (lambda b, r, n: (b, n, r, 0, 0, c))
    else:
        layout = "tiles4"
        view = lambda x: x.reshape(bsz, ntile, 16, PERM_RUN, x.shape[-1])
        blk = lambda w: (None, 4 * qb, None, PERM_RUN, w)
        grid = (bsz, 16, ntile // (4 * qb))
        at = lambda c: (lambda b, r, n: (b, n, r, 0, c))
    qv, kvv = view(q), view(kv)
    o, lse = pl.pallas_call(
        functools.partial(_dsa_kernel, dil=d, layout=layout, qb=qb,
                          slopes=_alibi_slopes(ATT_HEADS)),
        grid=grid,
        in_specs=[
            pl.BlockSpec(blk(ATT_WIDTH), at(q_col)),
            pl.BlockSpec(blk(ATT_WIDTH), at(0)),
            pl.BlockSpec(blk(ATT_WIDTH), at(1)),
        ],
        out_specs=[pl.BlockSpec(blk(ATT_WIDTH), at(0)),
                   pl.BlockSpec(blk(LANES), at(0))],
        out_shape=[jax.ShapeDtypeStruct(qv.shape[:-1] + (ATT_WIDTH,), BF16),
                   jax.ShapeDtypeStruct(qv.shape[:-1] + (LANES,), F32)],
        scratch_shapes=[pltpu.VMEM((ATT_BLOCK, ATT_WIDTH), BF16),
                        pltpu.VMEM((ATT_BLOCK, ATT_WIDTH), BF16),
                        pltpu.VMEM((ATT_HEADS, ATT_BLOCK, 2 * ATT_BLOCK), F32)],
        compiler_params=pltpu.CompilerParams(
            dimension_semantics=("arbitrary", "arbitrary", "arbitrary")),
        name=f"dsa_branch{g}",
    )(qv, kvv, kvv)
    return o.reshape(t, ATT_WIDTH), lse.reshape(t, LANES)


def _merge_kernel(qt_ref, o0_ref, o1_ref, o2_ref, l0_ref, l1_ref, l2_ref, o_ref):
    qt = qt_ref[...]
    every = slice(None)
    for hf in range(2):
        rows = slice(hf * PERM_HALF, (hf + 1) * PERM_HALF)
        l0 = l0_ref[rows, :]
        l12 = _dot01(qt, jnp.concatenate([_gather_half(l1_ref, hf, every),
                                          _gather_half(l2_ref, hf, every)], axis=1))
        l1, l2 = l12[:, :LANES], l12[:, LANES:]
        m = jnp.maximum(jnp.maximum(l0, l1), l2)
        e0, e1, e2 = jnp.exp(l0 - m), jnp.exp(l1 - m), jnp.exp(l2 - m)
        den = e0 + e1 + e2
        w1, w2 = e1 / den, e2 / den
        for hp in range(ATT_HEADS // 2):
            pair = slice(2 * hp * HEAD_DIM, (2 * hp + 2) * HEAD_DIM)
            o1 = jnp.dot(qt, _gather_half(o1_ref, hf, pair), preferred_element_type=F32)
            o2 = jnp.dot(qt, _gather_half(o2_ref, hf, pair), preferred_element_type=F32)
            for k in range(2):
                h = 2 * hp + k
                sl = slice(h * HEAD_DIM, (h + 1) * HEAD_DIM)
                in_pair = slice(k * HEAD_DIM, (k + 1) * HEAD_DIM)
                o0 = o0_ref[rows, sl].astype(F32)
                o = (o0 + w1[:, h:h + 1] * (o1[:, in_pair] - o0)
                     + w2[:, h:h + 1] * (o2[:, in_pair] - o0))
                o_ref[rows, sl] = o.astype(o_ref.dtype)


def _dsa_merge(outs, lses, perm_t):
    t, hd = outs[0].shape
    tm = PERM_TILE
    o_spec = pl.BlockSpec((tm, hd), lambda i: (i, 0))
    l_spec = pl.BlockSpec((tm, LANES), lambda i: (i, 0))
    return pl.pallas_call(
        _merge_kernel,
        grid=(t // tm,),
        in_specs=[pl.BlockSpec((PERM_HALF, PERM_HALF), lambda i: (0, 0)),
                  o_spec, o_spec, o_spec, l_spec, l_spec, l_spec],
        out_specs=o_spec,
        out_shape=jax.ShapeDtypeStruct((t, hd), BF16),
        compiler_params=pltpu.CompilerParams(dimension_semantics=("arbitrary",)),
        name="dsa_merge",
    )(perm_t, *outs, *lses)


def _ffn_up_kernel(x_ref, wu_ref, wg_ref, cw_ref, cb_ref, o_ref, wub_ref, wgb_ref,
                   us_ref, gs_ref, carry_ref, *, tm, tn, s_len):
    @pl.when(pl.program_id(1) == 0)
    def _():
        wub_ref[...] = wu_ref[...].astype(BF16)
        wgb_ref[...] = wg_ref[...].astype(BF16)

    @pl.when((pl.program_id(1) * tm) % s_len == 0)
    def _():
        carry_ref[...] = jnp.zeros_like(carry_ref)

    n_rb = tm // FFN_ROW_SUB
    subs = [(cb, rb) for cb in range(tn // FFN_COL_SUB) for rb in range(n_rb)]

    def window(idx):
        cb, rb = subs[idx]
        return (slice(rb * FFN_ROW_SUB, (rb + 1) * FFN_ROW_SUB),
                slice(cb * FFN_COL_SUB, (cb + 1) * FFN_COL_SUB))

    def project(idx):
        rows, cols = window(idx)
        slot = idx % FFN_SLOTS
        xr = x_ref[rows, :]
        us_ref[slot] = jnp.dot(xr, wub_ref[:, cols], preferred_element_type=F32)
        gs_ref[slot, 8:] = jnp.dot(xr, wgb_ref[:, cols], preferred_element_type=F32)
        if subs[idx][1] == 0:
            gs_ref[slot, :8] = carry_ref[:, cols]
        else:
            gs_ref[slot, :8] = gs_ref[(idx - 1) % FFN_SLOTS, FFN_ROW_SUB:]
        if subs[idx][1] == n_rb - 1:
            carry_ref[:, cols] = gs_ref[slot, FFN_ROW_SUB:]

    def epilogue(idx):
        rows, cols = window(idx)
        slot = idx % FFN_SLOTS
        g = gs_ref[slot, 8:]
        g1 = gs_ref[slot, 7:7 + FFN_ROW_SUB]
        g2 = gs_ref[slot, 6:6 + FFN_ROW_SUB]
        cw = 0.5 * cw_ref[:, cols]
        half = cw[0:1] * g2 + cw[1:2] * g1 + cw[2:3] * g + 0.5 * cb_ref[:, cols]
        act = half * (1.0 + lax.erf(half * (2.0 ** 0.5)))
        o_ref[rows, cols] = (act * us_ref[slot]).astype(o_ref.dtype)

    ahead = FFN_SLOTS - 1
    for idx in range(ahead):
        project(idx)
    for idx in range(len(subs)):
        if idx + ahead < len(subs):
            project(idx + ahead)
        epilogue(idx)


def _ffn_up(xn, w_up, layer, conv_w, conv_b, s_len, tm=2048, tn=512):
    m, k = xn.shape
    nn = D_FF // tn
    assert s_len % tm == 0 and tm % FFN_ROW_SUB == 0 and tn % FFN_COL_SUB == 0
    return pl.pallas_call(
        functools.partial(_ffn_up_kernel, tm=tm, tn=tn, s_len=s_len),
        grid=(nn, m // tm),
        in_specs=[
            pl.BlockSpec((tm, k), lambda j, i: (i, 0)),
            pl.BlockSpec((None, k, tn), lambda j, i: (layer, 0, j)),
            pl.BlockSpec((None, k, tn), lambda j, i: (layer, 0, nn + j)),
            pl.BlockSpec((3, tn), lambda j, i: (0, j)),
            pl.BlockSpec((1, tn), lambda j, i: (0, j)),
        ],
        out_specs=pl.BlockSpec((tm, tn), lambda j, i: (i, j)),
        out_shape=jax.ShapeDtypeStruct((m, D_FF), BF16),
        scratch_shapes=[pltpu.VMEM((k, tn), BF16), pltpu.VMEM((k, tn), BF16),
                        pltpu.VMEM((FFN_SLOTS, FFN_ROW_SUB, FFN_COL_SUB), F32),
                        pltpu.VMEM((FFN_SLOTS, 8 + FFN_ROW_SUB, FFN_COL_SUB), F32),
                        pltpu.VMEM((8, tn), F32)],
        compiler_params=pltpu.CompilerParams(
            dimension_semantics=("arbitrary", "arbitrary")),
        name="ffn_up",
    )(xn, w_up, w_up, conv_w, conv_b.reshape(1, D_FF))


def _conv_glu(h, xn, w_up, layer, conv_w, conv_b, w_down, s_len):
    act = _ffn_up(xn, w_up, layer, conv_w, conv_b, s_len)
    return _matmul(act, w_down, layer, F32, residual=h, tn=1024, weight_buffers=1)


def kernel(x, attn_norm, gla_w_in, gla_w_a2, gla_b_a2, gla_head_norm, gla_w_out, kv_norm, w_kv,
           dsa_w_q, dsa_w_out, ffn_norm, ffn_w_up, ffn_conv_w, ffn_conv_b, ffn_w_down, final_norm):
    bsz, s_len, d = x.shape
    t = bsz * s_len
    h = x.reshape(t, d)

    n_main = 2 * GLA_KEY_DIM + 2 * GLA_VAL_DIM
    w_in = gla_w_in.astype(BF16)
    w_a1 = jnp.pad(w_in[0, :, n_main:], ((0, 0), (0, LANES - GATE_RANK)))
    w_a2 = jnp.pad(gla_w_a2[0].astype(BF16), ((0, LANES - GATE_RANK), (0, 0)))
    xn, a = _rmsnorm_gate_in(h, attn_norm[0], w_a1)
    proj = _matmul(xn, w_in, 0, BF16, n_cols=n_main, tm=1024)
    o = _gla_recurrence(proj, a, w_a2, gla_b_a2[0], gla_head_norm[0], bsz, s_len)
    h, xn = _matmul_res_norm(o, gla_w_out, 0, h, ffn_norm[0])
    h = _conv_glu(h, xn, ffn_w_up, 0, ffn_conv_w[0], ffn_conv_b[0], ffn_w_down, s_len)

    perm_np = _half_perm_matrix()
    perm = jnp.asarray(perm_np, BF16)
    perm_t = jnp.asarray(perm_np.T, BF16)
    xkv, xq, xq_perm = _rmsnorm2_perm(h, kv_norm, attn_norm[1], perm)
    kv, kv_perm = _matmul_and_permuted(xkv, w_kv[None], 0, perm)

    q_scale = HEAD_DIM ** -0.5 * LOG2_E
    q0 = _matmul(xq, dsa_w_q, 0, BF16, n_cols=ATT_WIDTH, out_scale=q_scale, tm=1024)
    q12 = _matmul(xq_perm, dsa_w_q, 0, BF16, n_cols=2 * ATT_WIDTH, col_start=ATT_WIDTH,
                  out_scale=q_scale, tm=1024)
    o0, lse0 = _dsa_branch(q0, 0, kv, 0, bsz, s_len)
    o1, lse1 = _dsa_branch(q12, 0, kv_perm, 1, bsz, s_len)
    o2, lse2 = _dsa_branch(q12, 1, kv_perm, 2, bsz, s_len)
    o = _dsa_merge([o0, o1, o2], [lse0, lse1, lse2], perm_t)
    h, xn = _matmul_res_norm(o, dsa_w_out, 0, h, ffn_norm[1])
    h = _conv_glu(h, xn, ffn_w_up, 1, ffn_conv_w[1], ffn_conv_b[1], ffn_w_down, s_len)

    return _rmsnorm(h, final_norm, F32).reshape(bsz, s_len, d)
```

```python
import functools
import math

import numpy as np
import jax
import jax.numpy as jnp
from jax import lax
from jax.experimental import pallas as pl
from jax.experimental.pallas import tpu as pltpu

D_MODEL = 2048
GLA_HEADS = 4
GLA_KEY_DIM = 1024
GLA_VAL_DIM = 2048
GLA_DK = 256
GLA_DV = 512
GATE_RANK = 16
GATE_NORMALIZER = 16.0
GLA_CHUNK = 64
GLA_BLOCK = 256
GLA_HEADS_PER_STEP = 4
ATT_HEADS = 16
HEAD_DIM = 128
ATT_WIDTH = ATT_HEADS * HEAD_DIM
WINDOWS = (128, 512, 2048)
DILATIONS = (1, 4, 16)
ATT_BLOCK = 128
DSA_BLOCKS_PER_STEP = 4
DSA_SEQS_PER_STEP = 2
QK_LOOKAHEAD = 2
PERM_TILE = 512
PERM_RUN = PERM_TILE // 16
PERM_HALF = PERM_TILE // 2
HALF_RUN = PERM_RUN // 2
D_FF = 5632
EPS = 1e-6
LOG2_E = math.log2(math.e)
LN_2 = math.log(2.0)
LANES = 128
FFN_ROW_SUB = 512
FFN_COL_SUB = 256
FFN_SLOTS = 2

F32 = jnp.float32
BF16 = jnp.bfloat16


def _alibi_slopes(n):
    def pow2_slopes(m):
        start = 2.0 ** (-8.0 / m)
        return [start ** (i + 1) for i in range(m)]
    assert math.log2(n).is_integer()
    return [float(v) for v in np.array(pow2_slopes(n), dtype=np.float32)]


def _rmsnorm_kernel(x_ref, g_ref, o_ref):
    x = x_ref[...]
    ms = jnp.mean(x * x, axis=-1, keepdims=True)
    o_ref[...] = ((x * lax.rsqrt(ms + EPS)) * g_ref[...]).astype(o_ref.dtype)


def _rmsnorm(x, g, out_dtype, tm=512):
    m, d = x.shape
    return pl.pallas_call(
        _rmsnorm_kernel,
        grid=(m // tm,),
        in_specs=[pl.BlockSpec((tm, d), lambda i: (i, 0)),
                  pl.BlockSpec((1, d), lambda i: (0, 0))],
        out_specs=pl.BlockSpec((tm, d), lambda i: (i, 0)),
        out_shape=jax.ShapeDtypeStruct((m, d), out_dtype),
        compiler_params=pltpu.CompilerParams(dimension_semantics=("arbitrary",)),
        name="rmsnorm",
    )(x, g.reshape(1, d))


def _cast_weights_once(w_ref, wb_ref):
    @pl.when(pl.program_id(1) == 0)
    def _():
        wb_ref[...] = w_ref[...].astype(BF16)


def _mm_kernel(x_ref, w_ref, *rest, out_scale, has_residual, cast_weights):
    o_ref = rest[1 if has_residual else 0]
    if cast_weights:
        _cast_weights_once(w_ref, rest[-1])
        w_ref = rest[-1]
    acc = jnp.dot(x_ref[...], w_ref[...], preferred_element_type=F32)
    if has_residual:
        acc = rest[0][...] + acc
    if out_scale is not None:
        acc = acc * out_scale
    o_ref[...] = acc.astype(o_ref.dtype)


def _matmul(x, w, layer, out_dtype, n_cols=None, col_start=0, residual=None, out_scale=None,
            tm=512, tn=1024, weight_buffers=2):
    m, k = x.shape
    cast_weights = w.dtype != BF16
    n = w.shape[2] if n_cols is None else n_cols
    assert w.shape[1] == k and m % tm == 0 and n % tn == 0 and col_start % tn == 0
    j0 = col_start // tn
    in_specs = [pl.BlockSpec((tm, k), lambda j, i: (i, 0)),
                pl.BlockSpec((None, k, tn), lambda j, i: (layer, 0, j0 + j),
                             pipeline_mode=pl.Buffered(weight_buffers))]
    args = [x, w]
    if residual is not None:
        in_specs.append(pl.BlockSpec((tm, tn), lambda j, i: (i, j)))
        args.append(residual)
    return pl.pallas_call(
        functools.partial(_mm_kernel, out_scale=out_scale, has_residual=residual is not None,
                          cast_weights=cast_weights),
        grid=(n // tn, m // tm),
        in_specs=in_specs,
        out_specs=pl.BlockSpec((tm, tn), lambda j, i: (i, j)),
        out_shape=jax.ShapeDtypeStruct((m, n), out_dtype),
        scratch_shapes=[pltpu.VMEM((k, tn), BF16)] if cast_weights else [],
        compiler_params=pltpu.CompilerParams(
            dimension_semantics=("arbitrary", "arbitrary")),
        name="matmul",
    )(*args)


def _mm_perm_kernel(x_ref, w_ref, p_ref, o_ref, op_ref, wb_ref):
    _cast_weights_once(w_ref, wb_ref)
    out = jnp.dot(x_ref[...], wb_ref[...], preferred_element_type=F32).astype(BF16)
    o_ref[...] = out
    for t in range(o_ref.shape[0] // PERM_TILE):
        _store_permuted(p_ref[...], out[t * PERM_TILE:(t + 1) * PERM_TILE], op_ref, t * PERM_TILE)


def _matmul_and_permuted(x, w, layer, perm, tm=1024, tn=1024):
    m, k = x.shape
    n = w.shape[2]
    assert w.shape[1] == k and m % tm == 0 and n % tn == 0 and tm % PERM_TILE == 0
    out_spec = pl.BlockSpec((tm, tn), lambda j, i: (i, j))
    out = jax.ShapeDtypeStruct((m, n), BF16)
    return pl.pallas_call(
        _mm_perm_kernel,
        grid=(n // tn, m // tm),
        in_specs=[pl.BlockSpec((tm, k), lambda j, i: (i, 0)),
                  pl.BlockSpec((None, k, tn), lambda j, i: (layer, 0, j)),
                  pl.BlockSpec((PERM_HALF, PERM_HALF), lambda j, i: (0, 0))],
        out_specs=[out_spec, out_spec],
        out_shape=[out, out],
        scratch_shapes=[pltpu.VMEM((k, tn), BF16)],
        compiler_params=pltpu.CompilerParams(
            dimension_semantics=("arbitrary", "arbitrary")),
        name="matmul_and_permuted",
    )(x, w, perm)


def _mm_res_norm_kernel(x_ref, w_ref, r_ref, g_ref, h_ref, xn_ref, wb_ref):
    @pl.when(pl.program_id(0) == 0)
    def _():
        wb_ref[...] = w_ref[...].astype(BF16)

    h = r_ref[...] + jnp.dot(x_ref[...], wb_ref[...], preferred_element_type=F32)
    h_ref[...] = h
    ms = jnp.mean(h * h, axis=-1, keepdims=True)
    xn_ref[...] = ((h * lax.rsqrt(ms + EPS)) * g_ref[...]).astype(xn_ref.dtype)


def _matmul_res_norm(x, w, layer, residual, norm_g, tm=512):
    m, k = x.shape
    n = w.shape[2]
    assert w.shape[1] == k and residual.shape == (m, n) and m % tm == 0
    row = pl.BlockSpec((tm, n), lambda i: (i, 0))
    return pl.pallas_call(
        _mm_res_norm_kernel,
        grid=(m // tm,),
        in_specs=[pl.BlockSpec((tm, k), lambda i: (i, 0)),
                  pl.BlockSpec((None, k, n), lambda i: (layer, 0, 0),
                               pipeline_mode=pl.Buffered(1)),
                  row,
                  pl.BlockSpec((1, n), lambda i: (0, 0))],
        out_specs=[row, row],
        out_shape=[jax.ShapeDtypeStruct((m, n), F32), jax.ShapeDtypeStruct((m, n), BF16)],
        scratch_shapes=[pltpu.VMEM((k, n), BF16)],
        compiler_params=pltpu.CompilerParams(dimension_semantics=("arbitrary",)),
        name="matmul_res_norm",
    )(x, w, residual, norm_g.reshape(1, n))


def _norm2_perm_kernel(x_ref, ga_ref, gb_ref, p_ref, a_ref, b_ref, bp_ref):
    x = x_ref[...]
    ms = jnp.mean(x * x, axis=-1, keepdims=True)
    y = x * lax.rsqrt(ms + EPS)
    a_ref[...] = (y * ga_ref[...]).astype(BF16)
    b = (y * gb_ref[...]).astype(BF16)
    b_ref[...] = b
    _store_permuted(p_ref[...], b, bp_ref, 0)


def _rmsnorm2_perm(x, g_a, g_b, perm):
    m, d = x.shape
    tm = PERM_TILE
    row = pl.BlockSpec((tm, d), lambda i: (i, 0))
    vec = pl.BlockSpec((1, d), lambda i: (0, 0))
    out = jax.ShapeDtypeStruct((m, d), BF16)
    return pl.pallas_call(
        _norm2_perm_kernel,
        grid=(m // tm,),
        in_specs=[row, vec, vec, pl.BlockSpec((PERM_HALF, PERM_HALF), lambda i: (0, 0))],
        out_specs=[row, row, row],
        out_shape=[out, out, out],
        compiler_params=pltpu.CompilerParams(dimension_semantics=("arbitrary",)),
        name="rmsnorm2_perm",
    )(x, g_a.reshape(1, d), g_b.reshape(1, d), perm)


def _norm_gate_in_kernel(x_ref, g_ref, w1_ref, xn_ref, a_ref):
    x = x_ref[...]
    ms = jnp.mean(x * x, axis=-1, keepdims=True)
    xn = ((x * lax.rsqrt(ms + EPS)) * g_ref[...]).astype(BF16)
    xn_ref[...] = xn
    a_ref[...] = jnp.dot(xn, w1_ref[...], preferred_element_type=F32).astype(BF16)


def _rmsnorm_gate_in(x, g, w_a1, tm=512):
    m, d = x.shape
    return pl.pallas_call(
        _norm_gate_in_kernel,
        grid=(m // tm,),
        in_specs=[pl.BlockSpec((tm, d), lambda i: (i, 0)),
                  pl.BlockSpec((1, d), lambda i: (0, 0)),
                  pl.BlockSpec((d, LANES), lambda i: (0, 0))],
        out_specs=[pl.BlockSpec((tm, d), lambda i: (i, 0)),
                   pl.BlockSpec((tm, LANES), lambda i: (i, 0))],
        out_shape=[jax.ShapeDtypeStruct((m, d), BF16), jax.ShapeDtypeStruct((m, LANES), BF16)],
        compiler_params=pltpu.CompilerParams(dimension_semantics=("arbitrary",)),
        name="rmsnorm_gate_in",
    )(x, g.reshape(1, d), w_a1)


def _split3(x):
    hi = x.astype(BF16)
    r1 = x - hi.astype(F32)
    mid = r1.astype(BF16)
    lo = (r1 - mid.astype(F32)).astype(BF16)
    return hi, mid, lo


def _dot01(mat01, x):
    hi, mid, lo = _split3(x)
    return (jnp.dot(mat01, hi, preferred_element_type=F32)
            + jnp.dot(mat01, mid, preferred_element_type=F32)
            + jnp.dot(mat01, lo, preferred_element_type=F32))


def _gla_kernel(q_ref, k_ref, v_ref, r_ref, a_ref, w2_ref, b2_ref, hn_ref, o_ref, st_ref):
    @pl.when(pl.program_id(2) == 0)
    def _():
        st_ref[...] = jnp.zeros_like(st_ref)

    nb = GLA_BLOCK
    heads = range(GLA_HEADS_PER_STEP)
    dk = lambda h: slice(h * GLA_DK, (h + 1) * GLA_DK)
    dv = lambda h: slice(h * GLA_DV, (h + 1) * GLA_DV)
    row = lax.broadcasted_iota(jnp.int32, (nb, nb), 0)
    col = lax.broadcasted_iota(jnp.int32, (nb, nb), 1)
    same_chunk = (row // GLA_CHUNK) == (col // GLA_CHUNK)
    causal = same_chunk & (col <= row)
    tri = jnp.where(causal, 1.0, 0.0).astype(BF16)
    nt = (((1,), (1,)), ((), ()))
    tn = (((0,), (0,)), ((), ()))

    a = a_ref[...]
    cum = []
    for h in heads:
        z = jnp.dot(a, w2_ref[:, dk(h)], preferred_element_type=F32) + b2_ref[:, dk(h)]
        log_sig = jnp.minimum(z, 0.0) - jnp.log(1.0 + jnp.exp(-jnp.abs(z)))
        log2_alpha = log_sig * (LOG2_E / GATE_NORMALIZER)
        cum.append(_dot01(tri, log2_alpha))
    k = [k_ref[:, dk(h)].astype(F32) for h in heads]
    q_dec = [((q_ref[:, dk(h)].astype(F32) * (GLA_DK ** -0.5)) * jnp.exp2(cum[h])).astype(BF16)
             for h in heads]
    k_inv = [(k[h] * jnp.exp2(-cum[h])).astype(BF16) for h in heads]
    scores = [lax.dot_general(q_dec[h], k_inv[h], nt, preferred_element_type=F32)
              for h in heads]
    scores = [jnp.where(causal, s, 0.0).astype(BF16) for s in scores]
    o_intra = [jnp.dot(scores[h], v_ref[:, dv(h)], preferred_element_type=F32)
               for h in heads]

    o_parts = [[] for _ in heads]
    for c in range(nb // GLA_CHUNK):
        rows = slice(c * GLA_CHUNK, (c + 1) * GLA_CHUNK)
        for h in heads:
            last_c = cum[h][(c + 1) * GLA_CHUNK - 1:(c + 1) * GLA_CHUNK]
            k_end = (k[h][rows] * jnp.exp2(last_c - cum[h][rows])).astype(BF16)
            st = st_ref[h]
            o_inter = lax.dot_general(q_dec[h][rows], st.astype(BF16), nt,
                                      preferred_element_type=F32)
            upd = lax.dot_general(v_ref[rows, dv(h)], k_end, tn,
                                  preferred_element_type=F32)
            st_ref[h] = st * jnp.exp2(last_c) + upd
            o_parts[h].append(o_intra[h][rows] + o_inter)

    for h in heads:
        o = jnp.concatenate(o_parts[h], axis=0)
        ms = jnp.mean(o * o, axis=-1, keepdims=True)
        o = (o * lax.rsqrt(ms + EPS)) * hn_ref[...]
        r = r_ref[:, dv(h)].astype(F32)
        gate = r * (1.0 / (1.0 + jnp.exp(-r)))
        o_ref[:, dv(h)] = (o * gate).astype(o_ref.dtype)


def _gla_recurrence(proj, a, w_a2, b_a2, head_norm, bsz, s_len):
    t = bsz * s_len
    nblk = s_len // GLA_BLOCK
    hps = GLA_HEADS_PER_STEP
    wk, wv = hps * GLA_DK, hps * GLA_DV
    k_off = GLA_KEY_DIM // wk
    v_off = 2 * GLA_KEY_DIM // wv
    r_off = (2 * GLA_KEY_DIM + GLA_VAL_DIM) // wv
    row = lambda b, g, i: b * nblk + i
    return pl.pallas_call(
        _gla_kernel,
        grid=(bsz, GLA_HEADS // hps, nblk),
        in_specs=[
            pl.BlockSpec((GLA_BLOCK, wk), lambda b, g, i: (row(b, g, i), g)),
            pl.BlockSpec((GLA_BLOCK, wk), lambda b, g, i: (row(b, g, i), k_off + g)),
            pl.BlockSpec((GLA_BLOCK, wv), lambda b, g, i: (row(b, g, i), v_off + g)),
            pl.BlockSpec((GLA_BLOCK, wv), lambda b, g, i: (row(b, g, i), r_off + g)),
            pl.BlockSpec((GLA_BLOCK, LANES), lambda b, g, i: (row(b, g, i), 0)),
            pl.BlockSpec((LANES, wk), lambda b, g, i: (0, g)),
            pl.BlockSpec((1, wk), lambda b, g, i: (0, g)),
            pl.BlockSpec((1, GLA_DV), lambda b, g, i: (0, 0)),
        ],
        out_specs=pl.BlockSpec((GLA_BLOCK, wv), lambda b, g, i: (row(b, g, i), g)),
        out_shape=jax.ShapeDtypeStruct((t, GLA_VAL_DIM), BF16),
        scratch_shapes=[pltpu.VMEM((hps, GLA_DV, GLA_DK), F32)],
        compiler_params=pltpu.CompilerParams(
            dimension_semantics=("arbitrary", "arbitrary", "arbitrary")),
        name="gla_recurrence",
    )(proj, proj, proj, proj, a, w_a2, b_a2.reshape(1, GLA_KEY_DIM), head_norm.reshape(1, GLA_DV))


def _half_perm_matrix():
    q = np.zeros((PERM_HALF, PERM_HALF), np.float32)
    for i in range(HALF_RUN):
        for a in range(4):
            for r4 in range(4):
                q[(4 * r4 + a) * HALF_RUN + i, 16 * i + 4 * a + r4] = 1.0
    return q


def _store_permuted(q, x, out_ref, base):
    for hf in range(2):
        y = jnp.dot(q, x[hf * PERM_HALF:(hf + 1) * PERM_HALF],
                    preferred_element_type=F32).astype(out_ref.dtype)
        for grp in range(16):
            dst = base + grp * PERM_RUN + hf * HALF_RUN
            out_ref[dst:dst + HALF_RUN, :] = y[grp * HALF_RUN:(grp + 1) * HALF_RUN]


def _gather_half(ref, hf, cols):
    return jnp.concatenate(
        [ref[grp * PERM_RUN + hf * HALF_RUN:grp * PERM_RUN + (hf + 1) * HALF_RUN, cols]
         for grp in range(16)], axis=0)


def _dsa_lead(layout, s, u):
    if layout == "rows":
        return (slice(s * ATT_BLOCK, (s + 1) * ATT_BLOCK),)
    if layout == "tile":
        return (s, slice(None), slice(None))
    assert layout == "seqs"
    return (slice(4 * s, 4 * s + 4), u, slice(None))


def _dsa_block_shape(layout, width):
    return (ATT_BLOCK, width) if layout == "rows" else (4, PERM_RUN, width)


def _dsa_kernel(q_ref, kc_ref, vc_ref, o_ref, lse_ref, kp_ref, vp_ref, bias_ref, *, dil, layout,
                qb, n_seq, slopes):
    n = pl.program_id(2)
    nk = 2 * ATT_BLOCK
    ck = lax.broadcasted_iota(jnp.int32, (ATT_BLOCK, nk), 1)
    is_cur = ck >= ATT_BLOCK

    @pl.when(n == 0)
    def _():
        kp_ref[...] = jnp.zeros_like(kp_ref)
        vp_ref[...] = jnp.zeros_like(vp_ref)

    @pl.when((pl.program_id(0) == 0) & (pl.program_id(1) == 0) & (n == 0))
    def _():
        rq = lax.broadcasted_iota(jnp.int32, (ATT_BLOCK, nk), 0)
        ck_in = ck & (ATT_BLOCK - 1)
        if layout == "tile":
            sub_q = 4 * (rq & (PERM_RUN - 1)) + (rq >> 5)
            sub_k = 4 * (ck_in & (PERM_RUN - 1)) + (ck_in >> 5)
        else:
            sub_q, sub_k = rq, ck_in
        j = sub_q - sub_k + jnp.where(is_cur, 0, ATT_BLOCK)
        valid = (j >= 0) & (j <= ATT_BLOCK)
        neg_dist = jnp.where(valid, -(j * dil).astype(F32), -jnp.inf)
        for h in range(ATT_HEADS):
            bias_ref[h] = (slopes[h] * LOG2_E) * neg_dist

    has_prev = is_cur | (n > 0)

    lane = lax.broadcasted_iota(jnp.int32, (ATT_BLOCK, LANES), 1)
    ones = jnp.ones((nk, HEAD_DIM), BF16)
    nt = (((1,), (1,)), ((), ()))

    def head(ref, s, u, h):
        x = ref[_dsa_lead(layout, s, u) + (slice(h * HEAD_DIM, (h + 1) * HEAD_DIM),)]
        return x.reshape(ATT_BLOCK, HEAD_DIM)

    def prev_head(ref, carry_ref, s, u, h):
        if s == 0:
            return carry_ref[:, h * HEAD_DIM:(h + 1) * HEAD_DIM]
        return head(ref, s - 1, u, h)

    for u, s in [(u, s) for u in range(n_seq) for s in range(qb)]:
        lead = _dsa_lead(layout, s, u)

        def scores(h, s=s, u=u):
            kcat = jnp.concatenate([prev_head(kc_ref, kp_ref, s, u, h), head(kc_ref, s, u, h)],
                                   axis=0)
            return lax.dot_general(head(q_ref, s, u, h), kcat, nt, preferred_element_type=F32)

        pending = [scores(h) for h in range(QK_LOOKAHEAD)]
        lse_tile = jnp.zeros((ATT_BLOCK, LANES), F32)
        for h in range(ATT_HEADS):
            sc = pending.pop(0)
            if h + QK_LOOKAHEAD < ATT_HEADS:
                pending.append(scores(h + QK_LOOKAHEAD))
            bias = bias_ref[h]
            if s == 0:
                bias = jnp.where(has_prev, bias, -jnp.inf)
            z = sc + bias
            m = jnp.max(z, axis=-1, keepdims=True)
            p = jnp.exp2(z - m).astype(BF16)
            vcat = jnp.concatenate([prev_head(vc_ref, vp_ref, s, u, h), head(vc_ref, s, u, h)],
                                   axis=0)
            acc = jnp.dot(p, jnp.concatenate([vcat, ones], axis=1),
                          preferred_element_type=F32)
            l = acc[:, HEAD_DIM:]
            o = (acc[:, :HEAD_DIM] / l).astype(o_ref.dtype)
            o_ref[lead + (slice(h * HEAD_DIM, (h + 1) * HEAD_DIM),)] = o.reshape(
                _dsa_block_shape(layout, HEAD_DIM))
            lse_tile = jnp.where(lane == h, m * LN_2 + jnp.log(l), lse_tile)
        lse_ref[lead + (slice(None),)] = lse_tile.reshape(_dsa_block_shape(layout, LANES))

    last = _dsa_lead(layout, qb - 1, n_seq - 1) + (slice(None),)
    kp_ref[...] = kc_ref[last].reshape(kp_ref.shape)
    vp_ref[...] = vc_ref[last].reshape(vp_ref.shape)


def _dsa_branch(q, q_col, kv, g, bsz, s_len):
    d = DILATIONS[g]
    assert WINDOWS[g] // d == ATT_BLOCK
    t = bsz * s_len
    ntile = s_len // PERM_TILE
    qb = min(DSA_BLOCKS_PER_STEP, s_len // d // ATT_BLOCK)
    n_seq = 1
    if g == 0:
        layout = "rows"
        view = lambda x: x.reshape(bsz, s_len, x.shape[-1])
        blk = lambda w: (None, qb * ATT_BLOCK, w)
        grid = (bsz, 1, s_len // (qb * ATT_BLOCK))
        at = lambda c: (lambda b, r, n: (b, n, c))
    elif g == 1:
        layout = "tile"
        view = lambda x: x.reshape(bsz, ntile, 4, 4, PERM_RUN, x.shape[-1])
        blk = lambda w: (None, qb, None, 4, PERM_RUN, w)
        grid = (bsz, 4, ntile // qb)
        at = lambda c: (lambda b, r, n: (b, n, r, 0, 0, c))
    else:
        layout = "seqs"
        n_seq = DSA_SEQS_PER_STEP
        assert ntile == 4 * qb
        view = lambda x: x.reshape(bsz, ntile, 16, PERM_RUN, x.shape[-1])
        blk = lambda w: (None, ntile, n_seq, PERM_RUN, w)
        grid = (bsz, 16 // n_seq, 1)
        at = lambda c: (lambda b, r, n: (b, 0, r, 0, c))
    qv, kvv = view(q), view(kv)
    o, lse = pl.pallas_call(
        functools.partial(_dsa_kernel, dil=d, layout=layout, qb=qb, n_seq=n_seq,
                          slopes=_alibi_slopes(ATT_HEADS)),
        grid=grid,
        in_specs=[
            pl.BlockSpec(blk(ATT_WIDTH), at(q_col)),
            pl.BlockSpec(blk(ATT_WIDTH), at(0)),
            pl.BlockSpec(blk(ATT_WIDTH), at(1)),
        ],
        out_specs=[pl.BlockSpec(blk(ATT_WIDTH), at(0)),
                   pl.BlockSpec(blk(LANES), at(0))],
        out_shape=[jax.ShapeDtypeStruct(qv.shape[:-1] + (ATT_WIDTH,), BF16),
                   jax.ShapeDtypeStruct(qv.shape[:-1] + (LANES,), F32)],
        scratch_shapes=[pltpu.VMEM((ATT_BLOCK, ATT_WIDTH), BF16),
                        pltpu.VMEM((ATT_BLOCK, ATT_WIDTH), BF16),
                        pltpu.VMEM((ATT_HEADS, ATT_BLOCK, 2 * ATT_BLOCK), F32)],
        compiler_params=pltpu.CompilerParams(
            dimension_semantics=("arbitrary", "arbitrary", "arbitrary")),
        name=f"dsa_branch{g}",
    )(qv, kvv, kvv)
    return o.reshape(t, ATT_WIDTH), lse.reshape(t, LANES)


def _merge_kernel(qt_ref, o0_ref, o1_ref, o2_ref, l0_ref, l1_ref, l2_ref, o_ref):
    qt = qt_ref[...]
    every = slice(None)
    for hf in range(2):
        rows = slice(hf * PERM_HALF, (hf + 1) * PERM_HALF)
        l0 = l0_ref[rows, :]
        l12 = _dot01(qt, jnp.concatenate([_gather_half(l1_ref, hf, every),
                                          _gather_half(l2_ref, hf, every)], axis=1))
        l1, l2 = l12[:, :LANES], l12[:, LANES:]
        m = jnp.maximum(jnp.maximum(l0, l1), l2)
        e0, e1, e2 = jnp.exp(l0 - m), jnp.exp(l1 - m), jnp.exp(l2 - m)
        den = e0 + e1 + e2
        w1, w2 = e1 / den, e2 / den
        for hp in range(ATT_HEADS // 2):
            pair = slice(2 * hp * HEAD_DIM, (2 * hp + 2) * HEAD_DIM)
            o1 = jnp.dot(qt, _gather_half(o1_ref, hf, pair), preferred_element_type=F32)
            o2 = jnp.dot(qt, _gather_half(o2_ref, hf, pair), preferred_element_type=F32)
            for k in range(2):
                h = 2 * hp + k
                sl = slice(h * HEAD_DIM, (h + 1) * HEAD_DIM)
                in_pair = slice(k * HEAD_DIM, (k + 1) * HEAD_DIM)
                o0 = o0_ref[rows, sl].astype(F32)
                o = (o0 + w1[:, h:h + 1] * (o1[:, in_pair] - o0)
                     + w2[:, h:h + 1] * (o2[:, in_pair] - o0))
                o_ref[rows, sl] = o.astype(o_ref.dtype)


def _dsa_merge(outs, lses, perm_t):
    t, hd = outs[0].shape
    tm = PERM_TILE
    o_spec = pl.BlockSpec((tm, hd), lambda i: (i, 0))
    l_spec = pl.BlockSpec((tm, LANES), lambda i: (i, 0))
    return pl.pallas_call(
        _merge_kernel,
        grid=(t // tm,),
        in_specs=[pl.BlockSpec((PERM_HALF, PERM_HALF), lambda i: (0, 0)),
                  o_spec, o_spec, o_spec, l_spec, l_spec, l_spec],
        out_specs=o_spec,
        out_shape=jax.ShapeDtypeStruct((t, hd), BF16),
        compiler_params=pltpu.CompilerParams(dimension_semantics=("arbitrary",)),
        name="dsa_merge",
    )(perm_t, *outs, *lses)


def _ffn_up_kernel(x_ref, wu_ref, wg_ref, cw_ref, cb_ref, o_ref, wub_ref, wgb_ref,
                   us_ref, gs_ref, carry_ref, *, tm, tn, s_len):
    @pl.when(pl.program_id(1) == 0)
    def _():
        wub_ref[...] = wu_ref[...].astype(BF16)
        wgb_ref[...] = wg_ref[...].astype(BF16)

    @pl.when((pl.program_id(1) * tm) % s_len == 0)
    def _():
        carry_ref[...] = jnp.zeros_like(carry_ref)

    n_rb = tm // FFN_ROW_SUB
    subs = [(cb, rb) for cb in range(tn // FFN_COL_SUB) for rb in range(n_rb)]

    def window(idx):
        cb, rb = subs[idx]
        return (slice(rb * FFN_ROW_SUB, (rb + 1) * FFN_ROW_SUB),
                slice(cb * FFN_COL_SUB, (cb + 1) * FFN_COL_SUB))

    def project(idx):
        rows, cols = window(idx)
        slot = idx % FFN_SLOTS
        xr = x_ref[rows, :]
        us_ref[slot] = jnp.dot(xr, wub_ref[:, cols], preferred_element_type=F32)
        gs_ref[slot, 8:] = jnp.dot(xr, wgb_ref[:, cols], preferred_element_type=F32)
        if subs[idx][1] == 0:
            gs_ref[slot, :8] = carry_ref[:, cols]
        else:
            gs_ref[slot, :8] = gs_ref[(idx - 1) % FFN_SLOTS, FFN_ROW_SUB:]
        if subs[idx][1] == n_rb - 1:
            carry_ref[:, cols] = gs_ref[slot, FFN_ROW_SUB:]

    def epilogue(idx):
        rows, cols = window(idx)
        slot = idx % FFN_SLOTS
        g = gs_ref[slot, 8:]
        g1 = gs_ref[slot, 7:7 + FFN_ROW_SUB]
        g2 = gs_ref[slot, 6:6 + FFN_ROW_SUB]
        cw = 0.5 * cw_ref[:, cols]
        half = cw[0:1] * g2 + cw[1:2] * g1 + cw[2:3] * g + 0.5 * cb_ref[:, cols]
        act = half * (1.0 + lax.erf(half * (2.0 ** 0.5)))
        o_ref[rows, cols] = (act * us_ref[slot]).astype(o_ref.dtype)

    ahead = FFN_SLOTS - 1
    for idx in range(ahead):
        project(idx)
    for idx in range(len(subs)):
        if idx + ahead < len(subs):
            project(idx + ahead)
        epilogue(idx)


def _ffn_up(xn, w_up, layer, conv_w, conv_b, s_len, tm=2048, tn=512):
    m, k = xn.shape
    nn = D_FF // tn
    assert s_len % tm == 0 and tm % FFN_ROW_SUB == 0 and tn % FFN_COL_SUB == 0
    return pl.pallas_call(
        functools.partial(_ffn_up_kernel, tm=tm, tn=tn, s_len=s_len),
        grid=(nn, m // tm),
        in_specs=[
            pl.BlockSpec((tm, k), lambda j, i: (i, 0)),
            pl.BlockSpec((None, k, tn), lambda j, i: (layer, 0, j)),
            pl.BlockSpec((None, k, tn), lambda j, i: (layer, 0, nn + j)),
            pl.BlockSpec((3, tn), lambda j, i: (0, j)),
            pl.BlockSpec((1, tn), lambda j, i: (0, j)),
        ],
        out_specs=pl.BlockSpec((tm, tn), lambda j, i: (i, j)),
        out_shape=jax.ShapeDtypeStruct((m, D_FF), BF16),
        scratch_shapes=[pltpu.VMEM((k, tn), BF16), pltpu.VMEM((k, tn), BF16),
                        pltpu.VMEM((FFN_SLOTS, FFN_ROW_SUB, FFN_COL_SUB), F32),
                        pltpu.VMEM((FFN_SLOTS, 8 + FFN_ROW_SUB, FFN_COL_SUB), F32),
                        pltpu.VMEM((8, tn), F32)],
        compiler_params=pltpu.CompilerParams(
            dimension_semantics=("arbitrary", "arbitrary")),
        name="ffn_up",
    )(xn, w_up, w_up, conv_w, conv_b.reshape(1, D_FF))


def _conv_glu(h, xn, w_up, layer, conv_w, conv_b, w_down, s_len):
    act = _ffn_up(xn, w_up, layer, conv_w, conv_b, s_len)
    return _matmul(act, w_down, layer, F32, residual=h, tn=1024, weight_buffers=1)


def kernel(x, attn_norm, gla_w_in, gla_w_a2, gla_b_a2, gla_head_norm, gla_w_out, kv_norm, w_kv,
           dsa_w_q, dsa_w_out, ffn_norm, ffn_w_up, ffn_conv_w, ffn_conv_b, ffn_w_down, final_norm):
    bsz, s_len, d = x.shape
    t = bsz * s_len
    h = x.reshape(t, d)

    n_main = 2 * GLA_KEY_DIM + 2 * GLA_VAL_DIM
    w_in = gla_w_in.astype(BF16)
    w_a1 = jnp.pad(w_in[0, :, n_main:], ((0, 0), (0, LANES - GATE_RANK)))
    w_a2 = jnp.pad(gla_w_a2[0].astype(BF16), ((0, LANES - GATE_RANK), (0, 0)))
    xn, a = _rmsnorm_gate_in(h, attn_norm[0], w_a1)
    proj = _matmul(xn, w_in, 0, BF16, n_cols=n_main, tm=1024)
    o = _gla_recurrence(proj, a, w_a2, gla_b_a2[0], gla_head_norm[0], bsz, s_len)
    h, xn = _matmul_res_norm(o, gla_w_out, 0, h, ffn_norm[0])
    h = _conv_glu(h, xn, ffn_w_up, 0, ffn_conv_w[0], ffn_conv_b[0], ffn_w_down, s_len)

    perm_np = _half_perm_matrix()
    perm = jnp.asarray(perm_np, BF16)
    perm_t = jnp.asarray(perm_np.T, BF16)
    xkv, xq, xq_perm = _rmsnorm2_perm(h, kv_norm, attn_norm[1], perm)
    kv, kv_perm = _matmul_and_permuted(xkv, w_kv[None], 0, perm)

    q_scale = HEAD_DIM ** -0.5 * LOG2_E
    q0 = _matmul(xq, dsa_w_q, 0, BF16, n_cols=ATT_WIDTH, out_scale=q_scale, tm=1024)
    q12 = _matmul(xq_perm, dsa_w_q, 0, BF16, n_cols=2 * ATT_WIDTH, col_start=ATT_WIDTH,
                  out_scale=q_scale, tm=1024)
    o0, lse0 = _dsa_branch(q0, 0, kv, 0, bsz, s_len)
    o1, lse1 = _dsa_branch(q12, 0, kv_perm, 1, bsz, s_len)
    o2, lse2 = _dsa_branch(q12, 1, kv_perm, 2, bsz, s_len)
    o = _dsa_merge([o0, o1, o2], [lse0, lse1, lse2], perm_t)
    h, xn = _matmul_res_norm(o, dsa_w_out, 0, h, ffn_norm[1])
    h = _conv_glu(h, xn, ffn_w_up, 1, ffn_conv_w[1], ffn_conv_b[1], ffn_w_down, s_len)

    return _rmsnorm(h, final_norm, F32).reshape(bsz, s_len, d)
```

```python
import functools
import math

import numpy as np
import jax
import jax.numpy as jnp
from jax import lax
from jax.experimental import pallas as pl
from jax.experimental.pallas import tpu as pltpu

D_MODEL = 2048
GLA_HEADS = 4
GLA_KEY_DIM = 1024
GLA_VAL_DIM = 2048
GLA_DK = 256
GLA_DV = 512
GATE_RANK = 16
GATE_NORMALIZER = 16.0
GLA_CHUNK = 64
GLA_BLOCK = 256
GLA_HEADS_PER_STEP = 4
ATT_HEADS = 16
HEAD_DIM = 128
ATT_WIDTH = ATT_HEADS * HEAD_DIM
WINDOWS = (128, 512, 2048)
DILATIONS = (1, 4, 16)
ATT_BLOCK = 128
DSA_BLOCKS_PER_STEP = 4
DSA_SEQS_PER_STEP = 2
QK_LOOKAHEAD = 2
PERM_TILE = 512
PERM_RUN = PERM_TILE // 16
PERM_HALF = PERM_TILE // 2
HALF_RUN = PERM_RUN // 2
D_FF = 5632
EPS = 1e-6
LOG2_E = math.log2(math.e)
LN_2 = math.log(2.0)
LANES = 128
FFN_ROW_SUB = 512
FFN_COL_SUB = 256
FFN_SLOTS = 2

F32 = jnp.float32
BF16 = jnp.bfloat16


def _alibi_slopes(n):
    def pow2_slopes(m):
        start = 2.0 ** (-8.0 / m)
        return [start ** (i + 1) for i in range(m)]
    assert math.log2(n).is_integer()
    return [float(v) for v in np.array(pow2_slopes(n), dtype=np.float32)]


def _rmsnorm_kernel(x_ref, g_ref, o_ref):
    x = x_ref[...]
    ms = jnp.mean(x * x, axis=-1, keepdims=True)
    o_ref[...] = ((x * lax.rsqrt(ms + EPS)) * g_ref[...]).astype(o_ref.dtype)


def _rmsnorm(x, g, out_dtype, tm=512):
    m, d = x.shape
    return pl.pallas_call(
        _rmsnorm_kernel,
        grid=(m // tm,),
        in_specs=[pl.BlockSpec((tm, d), lambda i: (i, 0)),
                  pl.BlockSpec((1, d), lambda i: (0, 0))],
        out_specs=pl.BlockSpec((tm, d), lambda i: (i, 0)),
        out_shape=jax.ShapeDtypeStruct((m, d), out_dtype),
        compiler_params=pltpu.CompilerParams(dimension_semantics=("arbitrary",)),
        name="rmsnorm",
    )(x, g.reshape(1, d))


def _cast_weights_once(w_ref, wb_ref):
    @pl.when(pl.program_id(1) == 0)
    def _():
        wb_ref[...] = w_ref[...].astype(BF16)


def _mm_kernel(x_ref, w_ref, *rest, out_scale, has_residual, cast_weights):
    o_ref = rest[1 if has_residual else 0]
    if cast_weights:
        _cast_weights_once(w_ref, rest[-1])
        w_ref = rest[-1]
    acc = jnp.dot(x_ref[...], w_ref[...], preferred_element_type=F32)
    if has_residual:
        acc = rest[0][...] + acc
    if out_scale is not None:
        acc = acc * out_scale
    o_ref[...] = acc.astype(o_ref.dtype)


def _matmul(x, w, layer, out_dtype, n_cols=None, col_start=0, residual=None, out_scale=None,
            tm=512, tn=1024, weight_buffers=2):
    m, k = x.shape
    cast_weights = w.dtype != BF16
    n = w.shape[2] if n_cols is None else n_cols
    assert w.shape[1] == k and m % tm == 0 and n % tn == 0 and col_start % tn == 0
    j0 = col_start // tn
    in_specs = [pl.BlockSpec((tm, k), lambda j, i: (i, 0)),
                pl.BlockSpec((None, k, tn), lambda j, i: (layer, 0, j0 + j),
                             pipeline_mode=pl.Buffered(weight_buffers))]
    args = [x, w]
    if residual is not None:
        in_specs.append(pl.BlockSpec((tm, tn), lambda j, i: (i, j)))
        args.append(residual)
    return pl.pallas_call(
        functools.partial(_mm_kernel, out_scale=out_scale, has_residual=residual is not None,
                          cast_weights=cast_weights),
        grid=(n // tn, m // tm),
        in_specs=in_specs,
        out_specs=pl.BlockSpec((tm, tn), lambda j, i: (i, j)),
        out_shape=jax.ShapeDtypeStruct((m, n), out_dtype),
        scratch_shapes=[pltpu.VMEM((k, tn), BF16)] if cast_weights else [],
        compiler_params=pltpu.CompilerParams(
            dimension_semantics=("arbitrary", "arbitrary")),
        name="matmul",
    )(*args)


def _mm_perm_kernel(x_ref, w_ref, p_ref, o_ref, op_ref, wb_ref):
    _cast_weights_once(w_ref, wb_ref)
    out = jnp.dot(x_ref[...], wb_ref[...], preferred_element_type=F32).astype(BF16)
    o_ref[...] = out
    for t in range(o_ref.shape[0] // PERM_TILE):
        _store_permuted(p_ref[...], out[t * PERM_TILE:(t + 1) * PERM_TILE], op_ref, t * PERM_TILE)


def _matmul_and_permuted(x, w, layer, perm, tm=1024, tn=1024):
    m, k = x.shape
    n = w.shape[2]
    assert w.shape[1] == k and m % tm == 0 and n % tn == 0 and tm % PERM_TILE == 0
    out_spec = pl.BlockSpec((tm, tn), lambda j, i: (i, j))
    out = jax.ShapeDtypeStruct((m, n), BF16)
    return pl.pallas_call(
        _mm_perm_kernel,
        grid=(n // tn, m // tm),
        in_specs=[pl.BlockSpec((tm, k), lambda j, i: (i, 0)),
                  pl.BlockSpec((None, k, tn), lambda j, i: (layer, 0, j)),
                  pl.BlockSpec((PERM_HALF, PERM_HALF), lambda j, i: (0, 0))],
        out_specs=[out_spec, out_spec],
        out_shape=[out, out],
        scratch_shapes=[pltpu.VMEM((k, tn), BF16)],
        compiler_params=pltpu.CompilerParams(
            dimension_semantics=("arbitrary", "arbitrary")),
        name="matmul_and_permuted",
    )(x, w, perm)


def _mm_res_norm_kernel(x_ref, w_ref, r_ref, g_ref, h_ref, xn_ref, wb_ref):
    @pl.when(pl.program_id(0) == 0)
    def _():
        wb_ref[...] = w_ref[...].astype(BF16)

    h = r_ref[...] + jnp.dot(x_ref[...], wb_ref[...], preferred_element_type=F32)
    h_ref[...] = h
    ms = jnp.mean(h * h, axis=-1, keepdims=True)
    xn_ref[...] = ((h * lax.rsqrt(ms + EPS)) * g_ref[...]).astype(xn_ref.dtype)


def _matmul_res_norm(x, w, layer, residual, norm_g, tm=512):
    m, k = x.shape
    n = w.shape[2]
    assert w.shape[1] == k and residual.shape == (m, n) and m % tm == 0
    row = pl.BlockSpec((tm, n), lambda i: (i, 0))
    return pl.pallas_call(
        _mm_res_norm_kernel,
        grid=(m // tm,),
        in_specs=[pl.BlockSpec((tm, k), lambda i: (i, 0)),
                  pl.BlockSpec((None, k, n), lambda i: (layer, 0, 0),
                               pipeline_mode=pl.Buffered(1)),
                  row,
                  pl.BlockSpec((1, n), lambda i: (0, 0))],
        out_specs=[row, row],
        out_shape=[jax.ShapeDtypeStruct((m, n), F32), jax.ShapeDtypeStruct((m, n), BF16)],
        scratch_shapes=[pltpu.VMEM((k, n), BF16)],
        compiler_params=pltpu.CompilerParams(dimension_semantics=("arbitrary",)),
        name="matmul_res_norm",
    )(x, w, residual, norm_g.reshape(1, n))


def _norm2_perm_kernel(x_ref, ga_ref, gb_ref, p_ref, a_ref, b_ref, bp_ref):
    x = x_ref[...]
    ms = jnp.mean(x * x, axis=-1, keepdims=True)
    y = x * lax.rsqrt(ms + EPS)
    a_ref[...] = (y * ga_ref[...]).astype(BF16)
    b = (y * gb_ref[...]).astype(BF16)
    b_ref[...] = b
    _store_permuted(p_ref[...], b, bp_ref, 0)


def _rmsnorm2_perm(x, g_a, g_b, perm):
    m, d = x.shape
    tm = PERM_TILE
    row = pl.BlockSpec((tm, d), lambda i: (i, 0))
    vec = pl.BlockSpec((1, d), lambda i: (0, 0))
    out = jax.ShapeDtypeStruct((m, d), BF16)
    return pl.pallas_call(
        _norm2_perm_kernel,
        grid=(m // tm,),
        in_specs=[row, vec, vec, pl.BlockSpec((PERM_HALF, PERM_HALF), lambda i: (0, 0))],
        out_specs=[row, row, row],
        out_shape=[out, out, out],
        compiler_params=pltpu.CompilerParams(dimension_semantics=("arbitrary",)),
        name="rmsnorm2_perm",
    )(x, g_a.reshape(1, d), g_b.reshape(1, d), perm)


def _norm_gate_in_kernel(x_ref, g_ref, w1_ref, xn_ref, a_ref):
    x = x_ref[...]
    ms = jnp.mean(x * x, axis=-1, keepdims=True)
    xn = ((x * lax.rsqrt(ms + EPS)) * g_ref[...]).astype(BF16)
    xn_ref[...] = xn
    a_ref[...] = jnp.dot(xn, w1_ref[...], preferred_element_type=F32).astype(BF16)


def _rmsnorm_gate_in(x, g, w_a1, tm=512):
    m, d = x.shape
    return pl.pallas_call(
        _norm_gate_in_kernel,
        grid=(m // tm,),
        in_specs=[pl.BlockSpec((tm, d), lambda i: (i, 0)),
                  pl.BlockSpec((1, d), lambda i: (0, 0)),
                  pl.BlockSpec((d, LANES), lambda i: (0, 0))],
        out_specs=[pl.BlockSpec((tm, d), lambda i: (i, 0)),
                   pl.BlockSpec((tm, LANES), lambda i: (i, 0))],
        out_shape=[jax.ShapeDtypeStruct((m, d), BF16), jax.ShapeDtypeStruct((m, LANES), BF16)],
        compiler_params=pltpu.CompilerParams(dimension_semantics=("arbitrary",)),
        name="rmsnorm_gate_in",
    )(x, g.reshape(1, d), w_a1)


def _split3(x):
    hi = x.astype(BF16)
    r1 = x - hi.astype(F32)
    mid = r1.astype(BF16)
    lo = (r1 - mid.astype(F32)).astype(BF16)
    return hi, mid, lo


def _dot01(mat01, x):
    hi, mid, lo = _split3(x)
    return (jnp.dot(mat01, hi, preferred_element_type=F32)
            + jnp.dot(mat01, mid, preferred_element_type=F32)
            + jnp.dot(mat01, lo, preferred_element_type=F32))


def _gla_kernel(q_ref, k_ref, v_ref, r_ref, a_ref, w2_ref, b2_ref, hn_ref, o_ref, st_ref):
    @pl.when(pl.program_id(2) == 0)
    def _():
        st_ref[...] = jnp.zeros_like(st_ref)

    nb = GLA_BLOCK
    heads = range(GLA_HEADS_PER_STEP)
    dk = lambda h: slice(h * GLA_DK, (h + 1) * GLA_DK)
    dv = lambda h: slice(h * GLA_DV, (h + 1) * GLA_DV)
    row = lax.broadcasted_iota(jnp.int32, (nb, nb), 0)
    col = lax.broadcasted_iota(jnp.int32, (nb, nb), 1)
    same_chunk = (row // GLA_CHUNK) == (col // GLA_CHUNK)
    causal = same_chunk & (col <= row)
    tri = jnp.where(causal, 1.0, 0.0).astype(BF16)
    nt = (((1,), (1,)), ((), ()))
    tn = (((0,), (0,)), ((), ()))

    a = a_ref[...]
    cum = []
    for h in heads:
        z = jnp.dot(a, w2_ref[:, dk(h)], preferred_element_type=F32) + b2_ref[:, dk(h)]
        log_sig = jnp.minimum(z, 0.0) - jnp.log(1.0 + jnp.exp(-jnp.abs(z)))
        log2_alpha = log_sig * (LOG2_E / GATE_NORMALIZER)
        cum.append(_dot01(tri, log2_alpha))
    k = [k_ref[:, dk(h)].astype(F32) for h in heads]
    q_dec = [((q_ref[:, dk(h)].astype(F32) * (GLA_DK ** -0.5)) * jnp.exp2(cum[h])).astype(BF16)
             for h in heads]
    k_inv = [(k[h] * jnp.exp2(-cum[h])).astype(BF16) for h in heads]
    scores = [lax.dot_general(q_dec[h], k_inv[h], nt, preferred_element_type=F32)
              for h in heads]
    scores = [jnp.where(causal, s, 0.0).astype(BF16) for s in scores]
    o_intra = [jnp.dot(scores[h], v_ref[:, dv(h)], preferred_element_type=F32)
               for h in heads]

    o_parts = [[] for _ in heads]
    for c in range(nb // GLA_CHUNK):
        rows = slice(c * GLA_CHUNK, (c + 1) * GLA_CHUNK)
        for h in heads:
            last_c = cum[h][(c + 1) * GLA_CHUNK - 1:(c + 1) * GLA_CHUNK]
            k_end = (k[h][rows] * jnp.exp2(last_c - cum[h][rows])).astype(BF16)
            st = st_ref[h]
            o_inter = lax.dot_general(q_dec[h][rows], st.astype(BF16), nt,
                                      preferred_element_type=F32)
            upd = lax.dot_general(v_ref[rows, dv(h)], k_end, tn,
                                  preferred_element_type=F32)
            st_ref[h] = st * jnp.exp2(last_c) + upd
            o_parts[h].append(o_intra[h][rows] + o_inter)

    for h in heads:
        o = jnp.concatenate(o_parts[h], axis=0)
        ms = jnp.mean(o * o, axis=-1, keepdims=True)
        o = (o * lax.rsqrt(ms + EPS)) * hn_ref[...]
        r = r_ref[:, dv(h)].astype(F32)
        gate = r * (1.0 / (1.0 + jnp.exp(-r)))
        o_ref[:, dv(h)] = (o * gate).astype(o_ref.dtype)


def _gla_recurrence(proj, a, w_a2, b_a2, head_norm, bsz, s_len):
    t = bsz * s_len
    nblk = s_len // GLA_BLOCK
    hps = GLA_HEADS_PER_STEP
    wk, wv = hps * GLA_DK, hps * GLA_DV
    k_off = GLA_KEY_DIM // wk
    v_off = 2 * GLA_KEY_DIM // wv
    r_off = (2 * GLA_KEY_DIM + GLA_VAL_DIM) // wv
    row = lambda b, g, i: b * nblk + i
    return pl.pallas_call(
        _gla_kernel,
        grid=(bsz, GLA_HEADS // hps, nblk),
        in_specs=[
            pl.BlockSpec((GLA_BLOCK, wk), lambda b, g, i: (row(b, g, i), g)),
            pl.BlockSpec((GLA_BLOCK, wk), lambda b, g, i: (row(b, g, i), k_off + g)),
            pl.BlockSpec((GLA_BLOCK, wv), lambda b, g, i: (row(b, g, i), v_off + g)),
            pl.BlockSpec((GLA_BLOCK, wv), lambda b, g, i: (row(b, g, i), r_off + g)),
            pl.BlockSpec((GLA_BLOCK, LANES), lambda b, g, i: (row(b, g, i), 0)),
            pl.BlockSpec((LANES, wk), lambda b, g, i: (0, g)),
            pl.BlockSpec((1, wk), lambda b, g, i: (0, g)),
            pl.BlockSpec((1, GLA_DV), lambda b, g, i: (0, 0)),
        ],
        out_specs=pl.BlockSpec((GLA_BLOCK, wv), lambda b, g, i: (row(b, g, i), g)),
        out_shape=jax.ShapeDtypeStruct((t, GLA_VAL_DIM), BF16),
        scratch_shapes=[pltpu.VMEM((hps, GLA_DV, GLA_DK), F32)],
        compiler_params=pltpu.CompilerParams(
            dimension_semantics=("arbitrary", "arbitrary", "arbitrary")),
        name="gla_recurrence",
    )(proj, proj, proj, proj, a, w_a2, b_a2.reshape(1, GLA_KEY_DIM), head_norm.reshape(1, GLA_DV))


def _half_perm_matrix():
    q = np.zeros((PERM_HALF, PERM_HALF), np.float32)
    for i in range(HALF_RUN):
        for a in range(4):
            for r4 in range(4):
                q[(4 * r4 + a) * HALF_RUN + i, 16 * i + 4 * a + r4] = 1.0
    return q


def _store_permuted(q, x, out_ref, base):
    for hf in range(2):
        y = jnp.dot(q, x[hf * PERM_HALF:(hf + 1) * PERM_HALF],
                    preferred_element_type=F32).astype(out_ref.dtype)
        for grp in range(16):
            dst = base + grp * PERM_RUN + hf * HALF_RUN
            out_ref[dst:dst + HALF_RUN, :] = y[grp * HALF_RUN:(grp + 1) * HALF_RUN]


def _gather_half(ref, hf, cols):
    return jnp.concatenate(
        [ref[grp * PERM_RUN + hf * HALF_RUN:grp * PERM_RUN + (hf + 1) * HALF_RUN, cols]
         for grp in range(16)], axis=0)


def _dsa_lead(layout, s, u):
    if layout == "rows":
        return (slice(s * ATT_BLOCK, (s + 1) * ATT_BLOCK),)
    if layout == "tile":
        return (s, slice(None), slice(None))
    assert layout == "seqs"
    return (slice(4 * s, 4 * s + 4), u, slice(None))


def _dsa_block_shape(layout, width):
    return (ATT_BLOCK, width) if layout == "rows" else (4, PERM_RUN, width)


def _dsa_kernel(q_ref, kc_ref, vc_ref, o_ref, lse_ref, kp_ref, vp_ref, bias_ref, *, dil, layout,
                qb, n_seq, slopes):
    n = pl.program_id(2)
    nk = 2 * ATT_BLOCK
    ck = lax.broadcasted_iota(jnp.int32, (ATT_BLOCK, nk), 1)
    is_cur = ck >= ATT_BLOCK

    @pl.when(n == 0)
    def _():
        kp_ref[...] = jnp.zeros_like(kp_ref)
        vp_ref[...] = jnp.zeros_like(vp_ref)

    @pl.when((pl.program_id(0) == 0) & (pl.program_id(1) == 0) & (n == 0))
    def _():
        rq = lax.broadcasted_iota(jnp.int32, (ATT_BLOCK, nk), 0)
        ck_in = ck & (ATT_BLOCK - 1)
        if layout == "tile":
            sub_q = 4 * (rq & (PERM_RUN - 1)) + (rq >> 5)
            sub_k = 4 * (ck_in & (PERM_RUN - 1)) + (ck_in >> 5)
        else:
            sub_q, sub_k = rq, ck_in
        j = sub_q - sub_k + jnp.where(is_cur, 0, ATT_BLOCK)
        valid = (j >= 0) & (j <= ATT_BLOCK)
        neg_dist = jnp.where(valid, -(j * dil).astype(F32), -jnp.inf)
        for h in range(ATT_HEADS):
            bias_ref[h] = (slopes[h] * LOG2_E) * neg_dist

    has_prev = is_cur | (n > 0)

    lane = lax.broadcasted_iota(jnp.int32, (ATT_BLOCK, LANES), 1)
    ones = jnp.ones((nk, HEAD_DIM), BF16)
    nt = (((1,), (1,)), ((), ()))

    def head(ref, s, u, h):
        x = ref[_dsa_lead(layout, s, u) + (slice(h * HEAD_DIM, (h + 1) * HEAD_DIM),)]
        return x.reshape(ATT_BLOCK, HEAD_DIM)

    def prev_head(ref, carry_ref, s, u, h):
        if s == 0:
            return carry_ref[:, h * HEAD_DIM:(h + 1) * HEAD_DIM]
        return head(ref, s - 1, u, h)

    for u, s in [(u, s) for u in range(n_seq) for s in range(qb)]:
        lead = _dsa_lead(layout, s, u)

        def scores(h, s=s, u=u):
            kcat = jnp.concatenate([prev_head(kc_ref, kp_ref, s, u, h), head(kc_ref, s, u, h)],
                                   axis=0)
            return lax.dot_general(head(q_ref, s, u, h), kcat, nt, preferred_element_type=F32)

        pending = [scores(h) for h in range(QK_LOOKAHEAD)]
        lse_tile = jnp.zeros((ATT_BLOCK, LANES), F32)
        for h in range(ATT_HEADS):
            sc = pending.pop(0)
            if h + QK_LOOKAHEAD < ATT_HEADS:
                pending.append(scores(h + QK_LOOKAHEAD))
            bias = bias_ref[h]
            if s == 0:
                bias = jnp.where(has_prev, bias, -jnp.inf)
            z = sc + bias
            m = jnp.max(z, axis=-1, keepdims=True)
            p = jnp.exp2(z - m).astype(BF16)
            vcat = jnp.concatenate([prev_head(vc_ref, vp_ref, s, u, h), head(vc_ref, s, u, h)],
                                   axis=0)
            acc = jnp.dot(p, jnp.concatenate([vcat, ones], axis=1),
                          preferred_element_type=F32)
            l = acc[:, HEAD_DIM:]
            o = (acc[:, :HEAD_DIM] / l).astype(o_ref.dtype)
            o_ref[lead + (slice(h * HEAD_DIM, (h + 1) * HEAD_DIM),)] = o.reshape(
                _dsa_block_shape(layout, HEAD_DIM))
            lse_tile = jnp.where(lane == h, m * LN_2 + jnp.log(l), lse_tile)
        lse_ref[lead + (slice(None),)] = lse_tile.reshape(_dsa_block_shape(layout, LANES))

    last = _dsa_lead(layout, qb - 1, n_seq - 1) + (slice(None),)
    kp_ref[...] = kc_ref[last].reshape(kp_ref.shape)
    vp_ref[...] = vc_ref[last].reshape(vp_ref.shape)


def _dsa_branch(q, q_col, kv, g, bsz, s_len):
    d = DILATIONS[g]
    assert WINDOWS[g] // d == ATT_BLOCK
    t = bsz * s_len
    ntile = s_len // PERM_TILE
    qb = min(DSA_BLOCKS_PER_STEP, s_len // d // ATT_BLOCK)
    n_seq = 1
    if g == 0:
        layout = "rows"
        view = lambda x: x.reshape(bsz, s_len, x.shape[-1])
        blk = lambda w: (None, qb * ATT_BLOCK, w)
        grid = (bsz, 1, s_len // (qb * ATT_BLOCK))
        at = lambda c: (lambda b, r, n: (b, n, c))
    elif g == 1:
        layout = "tile"
        view = lambda x: x.reshape(bsz, ntile, 4, 4, PERM_RUN, x.shape[-1])
        blk = lambda w: (None, qb, None, 4, PERM_RUN, w)
        grid = (bsz, 4, ntile // qb)
        at = lambda c: (lambda b, r, n: (b, n, r, 0, 0, c))
    else:
        layout = "seqs"
        n_seq = DSA_SEQS_PER_STEP
        assert ntile == 4 * qb
        view = lambda x: x.reshape(bsz, ntile, 16, PERM_RUN, x.shape[-1])
        blk = lambda w: (None, ntile, n_seq, PERM_RUN, w)
        grid = (bsz, 16 // n_seq, 1)
        at = lambda c: (lambda b, r, n: (b, 0, r, 0, c))
    qv, kvv = view(q), view(kv)
    o, lse = pl.pallas_call(
        functools.partial(_dsa_kernel, dil=d, layout=layout, qb=qb, n_seq=n_seq,
                          slopes=_alibi_slopes(ATT_HEADS)),
        grid=grid,
        in_specs=[
            pl.BlockSpec(blk(ATT_WIDTH), at(q_col)),
            pl.BlockSpec(blk(ATT_WIDTH), at(0)),
            pl.BlockSpec(blk(ATT_WIDTH), at(1)),
        ],
        out_specs=[pl.BlockSpec(blk(ATT_WIDTH), at(0)),
                   pl.BlockSpec(blk(LANES), at(0))],
        out_shape=[jax.ShapeDtypeStruct(qv.shape[:-1] + (ATT_WIDTH,), BF16),
                   jax.ShapeDtypeStruct(qv.shape[:-1] + (LANES,), F32)],
        scratch_shapes=[pltpu.VMEM((ATT_BLOCK, ATT_WIDTH), BF16),
                        pltpu.VMEM((ATT_BLOCK, ATT_WIDTH), BF16),
                        pltpu.VMEM((ATT_HEADS, ATT_BLOCK, 2 * ATT_BLOCK), F32)],
        compiler_params=pltpu.CompilerParams(
            dimension_semantics=("arbitrary", "arbitrary", "arbitrary")),
        name=f"dsa_branch{g}",
    )(qv, kvv, kvv)
    return o.reshape(t, ATT_WIDTH), lse.reshape(t, LANES)


def _merge_kernel(qt_ref, o0_ref, o1_ref, o2_ref, l0_ref, l1_ref, l2_ref, o_ref):
    qt = qt_ref[...]
    every = slice(None)
    for hf in range(2):
        rows = slice(hf * PERM_HALF, (hf + 1) * PERM_HALF)
        l0 = l0_ref[rows, :]
        l12 = _dot01(qt, jnp.concatenate([_gather_half(l1_ref, hf, every),
                                          _gather_half(l2_ref, hf, every)], axis=1))
        l1, l2 = l12[:, :LANES], l12[:, LANES:]
        m = jnp.maximum(jnp.maximum(l0, l1), l2)
        e0, e1, e2 = jnp.exp(l0 - m), jnp.exp(l1 - m), jnp.exp(l2 - m)
        den = e0 + e1 + e2
        w1, w2 = e1 / den, e2 / den
        for hp in range(ATT_HEADS // 2):
            pair = slice(2 * hp * HEAD_DIM, (2 * hp + 2) * HEAD_DIM)
            o1 = jnp.dot(qt, _gather_half(o1_ref, hf, pair), preferred_element_type=F32)
            o2 = jnp.dot(qt, _gather_half(o2_ref, hf, pair), preferred_element_type=F32)
            for k in range(2):
                h = 2 * hp + k
                sl = slice(h * HEAD_DIM, (h + 1) * HEAD_DIM)
                in_pair = slice(k * HEAD_DIM, (k + 1) * HEAD_DIM)
                o0 = o0_ref[rows, sl].astype(F32)
                o = (o0 + w1[:, h:h + 1] * (o1[:, in_pair] - o0)
                     + w2[:, h:h + 1] * (o2[:, in_pair] - o0))
                o_ref[rows, sl] = o.astype(o_ref.dtype)


def _dsa_merge(outs, lses, perm_t):
    t, hd = outs[0].shape
    tm = PERM_TILE
    o_spec = pl.BlockSpec((tm, hd), lambda i: (i, 0))
    l_spec = pl.BlockSpec((tm, LANES), lambda i: (i, 0))
    return pl.pallas_call(
        _merge_kernel,
        grid=(t // tm,),
        in_specs=[pl.BlockSpec((PERM_HALF, PERM_HALF), lambda i: (0, 0)),
                  o_spec, o_spec, o_spec, l_spec, l_spec, l_spec],
        out_specs=o_spec,
        out_shape=jax.ShapeDtypeStruct((t, hd), BF16),
        compiler_params=pltpu.CompilerParams(dimension_semantics=("arbitrary",)),
        name="dsa_merge",
    )(perm_t, *outs, *lses)


def _ffn_up_kernel(x_ref, wu_ref, wg_ref, cw_ref, cb_ref, wd_ref, o_ref, wdb_ref, wub_ref, wgb_ref,
                   us_ref, gs_ref, carry_ref, *, tm, tn, s_len):
    wdb_ref[...] = wd_ref[...].astype(BF16)

    @pl.when(pl.program_id(1) == 0)
    def _():
        wub_ref[...] = wu_ref[...].astype(BF16)
        wgb_ref[...] = wg_ref[...].astype(BF16)

    @pl.when((pl.program_id(1) * tm) % s_len == 0)
    def _():
        carry_ref[...] = jnp.zeros_like(carry_ref)

    n_rb = tm // FFN_ROW_SUB
    subs = [(cb, rb) for cb in range(tn // FFN_COL_SUB) for rb in range(n_rb)]

    def window(idx):
        cb, rb = subs[idx]
        return (slice(rb * FFN_ROW_SUB, (rb + 1) * FFN_ROW_SUB),
                slice(cb * FFN_COL_SUB, (cb + 1) * FFN_COL_SUB))

    def project(idx):
        rows, cols = window(idx)
        slot = idx % FFN_SLOTS
        xr = x_ref[rows, :]
        us_ref[slot] = jnp.dot(xr, wub_ref[:, cols], preferred_element_type=F32)
        gs_ref[slot, 8:] = jnp.dot(xr, wgb_ref[:, cols], preferred_element_type=F32)
        if subs[idx][1] == 0:
            gs_ref[slot, :8] = carry_ref[:, cols]
        else:
            gs_ref[slot, :8] = gs_ref[(idx - 1) % FFN_SLOTS, FFN_ROW_SUB:]
        if subs[idx][1] == n_rb - 1:
            carry_ref[:, cols] = gs_ref[slot, FFN_ROW_SUB:]

    def epilogue(idx):
        rows, cols = window(idx)
        slot = idx % FFN_SLOTS
        g = gs_ref[slot, 8:]
        g1 = gs_ref[slot, 7:7 + FFN_ROW_SUB]
        g2 = gs_ref[slot, 6:6 + FFN_ROW_SUB]
        cw = 0.5 * cw_ref[:, cols]
        half = cw[0:1] * g2 + cw[1:2] * g1 + cw[2:3] * g + 0.5 * cb_ref[:, cols]
        act = half * (1.0 + lax.erf(half * (2.0 ** 0.5)))
        o_ref[rows, cols] = (act * us_ref[slot]).astype(o_ref.dtype)

    ahead = FFN_SLOTS - 1
    for idx in range(ahead):
        project(idx)
    for idx in range(len(subs)):
        if idx + ahead < len(subs):
            project(idx + ahead)
        epilogue(idx)


def _ffn_up(xn, w_up, layer, conv_w, conv_b, w_down, s_len, tm=2048, tn=512):
    m, k = xn.shape
    nn = D_FF // tn
    n_steps = nn * (m // tm)
    slab = D_FF // n_steps
    assert s_len % tm == 0 and tm % FFN_ROW_SUB == 0 and tn % FFN_COL_SUB == 0
    assert slab * n_steps == D_FF and slab % 16 == 0
    d_out = w_down.shape[2]
    step = lambda j, i: j * (m // tm) + i
    return pl.pallas_call(
        functools.partial(_ffn_up_kernel, tm=tm, tn=tn, s_len=s_len),
        grid=(nn, m // tm),
        in_specs=[
            pl.BlockSpec((tm, k), lambda j, i: (i, 0)),
            pl.BlockSpec((None, k, tn), lambda j, i: (layer, 0, j)),
            pl.BlockSpec((None, k, tn), lambda j, i: (layer, 0, nn + j)),
            pl.BlockSpec((3, tn), lambda j, i: (0, j)),
            pl.BlockSpec((1, tn), lambda j, i: (0, j)),
            pl.BlockSpec((None, slab, d_out), lambda j, i: (layer, step(j, i), 0)),
        ],
        out_specs=[pl.BlockSpec((tm, tn), lambda j, i: (i, j)),
                   pl.BlockSpec((None, slab, d_out), lambda j, i: (0, step(j, i), 0))],
        out_shape=[jax.ShapeDtypeStruct((m, D_FF), BF16),
                   jax.ShapeDtypeStruct((1, D_FF, d_out), BF16)],
        scratch_shapes=[pltpu.VMEM((k, tn), BF16), pltpu.VMEM((k, tn), BF16),
                        pltpu.VMEM((FFN_SLOTS, FFN_ROW_SUB, FFN_COL_SUB), F32),
                        pltpu.VMEM((FFN_SLOTS, 8 + FFN_ROW_SUB, FFN_COL_SUB), F32),
                        pltpu.VMEM((8, tn), F32)],
        compiler_params=pltpu.CompilerParams(
            dimension_semantics=("arbitrary", "arbitrary")),
        name="ffn_up",
    )(xn, w_up, w_up, conv_w, conv_b.reshape(1, D_FF), w_down)


def _conv_glu(h, xn, w_up, layer, conv_w, conv_b, w_down, s_len):
    act, w_down_bf16 = _ffn_up(xn, w_up, layer, conv_w, conv_b, w_down, s_len)
    return _matmul(act, w_down_bf16, 0, F32, residual=h, tn=1024)


def kernel(x, attn_norm, gla_w_in, gla_w_a2, gla_b_a2, gla_head_norm, gla_w_out, kv_norm, w_kv,
           dsa_w_q, dsa_w_out, ffn_norm, ffn_w_up, ffn_conv_w, ffn_conv_b, ffn_w_down, final_norm):
    bsz, s_len, d = x.shape
    t = bsz * s_len
    h = x.reshape(t, d)

    n_main = 2 * GLA_KEY_DIM + 2 * GLA_VAL_DIM
    w_in = gla_w_in.astype(BF16)
    w_a1 = jnp.pad(w_in[0, :, n_main:], ((0, 0), (0, LANES - GATE_RANK)))
    w_a2 = jnp.pad(gla_w_a2[0].astype(BF16), ((0, LANES - GATE_RANK), (0, 0)))
    xn, a = _rmsnorm_gate_in(h, attn_norm[0], w_a1)
    proj = _matmul(xn, w_in, 0, BF16, n_cols=n_main, tm=1024)
    o = _gla_recurrence(proj, a, w_a2, gla_b_a2[0], gla_head_norm[0], bsz, s_len)
    h, xn = _matmul_res_norm(o, gla_w_out, 0, h, ffn_norm[0])
    h = _conv_glu(h, xn, ffn_w_up, 0, ffn_conv_w[0], ffn_conv_b[0], ffn_w_down, s_len)

    perm_np = _half_perm_matrix()
    perm = jnp.asarray(perm_np, BF16)
    perm_t = jnp.asarray(perm_np.T, BF16)
    xkv, xq, xq_perm = _rmsnorm2_perm(h, kv_norm, attn_norm[1], perm)
    kv, kv_perm = _matmul_and_permuted(xkv, w_kv[None], 0, perm)

    q_scale = HEAD_DIM ** -0.5 * LOG2_E
    q0 = _matmul(xq, dsa_w_q, 0, BF16, n_cols=ATT_WIDTH, out_scale=q_scale, tm=1024)
    q12 = _matmul(xq_perm, dsa_w_q, 0, BF16, n_cols=2 * ATT_WIDTH, col_start=ATT_WIDTH,
                  out_scale=q_scale, tm=1024)
    o0, lse0 = _dsa_branch(q0, 0, kv, 0, bsz, s_len)
    o1, lse1 = _dsa_branch(q12, 0, kv_perm, 1, bsz, s_len)
    o2, lse2 = _dsa_branch(q12, 1, kv_perm, 2, bsz, s_len)
    o = _dsa_merge([o0, o1, o2], [lse0, lse1, lse2], perm_t)
    h, xn = _matmul_res_norm(o, dsa_w_out, 0, h, ffn_norm[1])
    h = _conv_glu(h, xn, ffn_w_up, 1, ffn_conv_w[1], ffn_conv_b[1], ffn_w_down, s_len)

    return _rmsnorm(h, final_norm, F32).reshape(bsz, s_len, d)
```

```python
import functools
import math

import numpy as np
import jax
import jax.numpy as jnp
from jax import lax
from jax.experimental import pallas as pl
from jax.experimental.pallas import tpu as pltpu

D_MODEL = 2048
GLA_HEADS = 4
GLA_KEY_DIM = 1024
GLA_VAL_DIM = 2048
GLA_DK = 256
GLA_DV = 512
GATE_RANK = 16
GATE_NORMALIZER = 16.0
GLA_CHUNK = 64
GLA_BLOCK = 256
GLA_HEADS_PER_STEP = 4
ATT_HEADS = 16
HEAD_DIM = 128
ATT_WIDTH = ATT_HEADS * HEAD_DIM
WINDOWS = (128, 512, 2048)
DILATIONS = (1, 4, 16)
ATT_BLOCK = 128
DSA_BLOCKS_PER_STEP = 4
DSA_SEQS_PER_STEP = 2
QK_LOOKAHEAD = 2
PERM_TILE = 512
PERM_RUN = PERM_TILE // 16
PERM_HALF = PERM_TILE // 2
HALF_RUN = PERM_RUN // 2
D_FF = 5632
EPS = 1e-6
LOG2_E = math.log2(math.e)
LN_2 = math.log(2.0)
LANES = 128
FFN_ROW_SUB = 512
FFN_COL_SUB = 256
FFN_SLOTS = 2

F32 = jnp.float32
BF16 = jnp.bfloat16


def _alibi_slopes(n):
    def pow2_slopes(m):
        start = 2.0 ** (-8.0 / m)
        return [start ** (i + 1) for i in range(m)]
    assert math.log2(n).is_integer()
    return [float(v) for v in np.array(pow2_slopes(n), dtype=np.float32)]


def _rmsnorm_kernel(x_ref, g_ref, o_ref):
    x = x_ref[...]
    ms = jnp.mean(x * x, axis=-1, keepdims=True)
    o_ref[...] = ((x * lax.rsqrt(ms + EPS)) * g_ref[...]).astype(o_ref.dtype)


def _rmsnorm(x, g, out_dtype, tm=512):
    m, d = x.shape
    return pl.pallas_call(
        _rmsnorm_kernel,
        grid=(m // tm,),
        in_specs=[pl.BlockSpec((tm, d), lambda i: (i, 0)),
                  pl.BlockSpec((1, d), lambda i: (0, 0))],
        out_specs=pl.BlockSpec((tm, d), lambda i: (i, 0)),
        out_shape=jax.ShapeDtypeStruct((m, d), out_dtype),
        compiler_params=pltpu.CompilerParams(dimension_semantics=("arbitrary",)),
        name="rmsnorm",
    )(x, g.reshape(1, d))


def _cast_weights_once(w_ref, wb_ref):
    @pl.when(pl.program_id(1) == 0)
    def _():
        wb_ref[...] = w_ref[...].astype(BF16)


def _mm_kernel(x_ref, w_ref, *rest, out_scale, has_residual, cast_weights):
    o_ref = rest[1 if has_residual else 0]
    if cast_weights:
        _cast_weights_once(w_ref, rest[-1])
        w_ref = rest[-1]
    acc = jnp.dot(x_ref[...], w_ref[...], preferred_element_type=F32)
    if has_residual:
        acc = rest[0][...] + acc
    if out_scale is not None:
        acc = acc * out_scale
    o_ref[...] = acc.astype(o_ref.dtype)


def _matmul(x, w, layer, out_dtype, n_cols=None, col_start=0, residual=None, out_scale=None,
            tm=512, tn=1024):
    m, k = x.shape
    cast_weights = w.dtype != BF16
    n = w.shape[2] if n_cols is None else n_cols
    assert w.shape[1] == k and m % tm == 0 and n % tn == 0 and col_start % tn == 0
    j0 = col_start // tn
    in_specs = [pl.BlockSpec((tm, k), lambda j, i: (i, 0)),
                pl.BlockSpec((None, k, tn), lambda j, i: (layer, 0, j0 + j))]
    args = [x, w]
    if residual is not None:
        in_specs.append(pl.BlockSpec((tm, tn), lambda j, i: (i, j)))
        args.append(residual)
    return pl.pallas_call(
        functools.partial(_mm_kernel, out_scale=out_scale, has_residual=residual is not None,
                          cast_weights=cast_weights),
        grid=(n // tn, m // tm),
        in_specs=in_specs,
        out_specs=pl.BlockSpec((tm, tn), lambda j, i: (i, j)),
        out_shape=jax.ShapeDtypeStruct((m, n), out_dtype),
        scratch_shapes=[pltpu.VMEM((k, tn), BF16)] if cast_weights else [],
        compiler_params=pltpu.CompilerParams(
            dimension_semantics=("arbitrary", "arbitrary")),
        name="matmul",
    )(*args)


def _mm_perm_kernel(x_ref, w_ref, p_ref, o_ref, op_ref, wb_ref):
    _cast_weights_once(w_ref, wb_ref)
    out = jnp.dot(x_ref[...], wb_ref[...], preferred_element_type=F32).astype(BF16)
    o_ref[...] = out
    for t in range(o_ref.shape[0] // PERM_TILE):
        _store_permuted(p_ref[...], out[t * PERM_TILE:(t + 1) * PERM_TILE], op_ref, t * PERM_TILE)


def _matmul_and_permuted(x, w, layer, perm, tm=1024, tn=1024):
    m, k = x.shape
    n = w.shape[2]
    assert w.shape[1] == k and m % tm == 0 and n % tn == 0 and tm % PERM_TILE == 0
    out_spec = pl.BlockSpec((tm, tn), lambda j, i: (i, j))
    out = jax.ShapeDtypeStruct((m, n), BF16)
    return pl.pallas_call(
        _mm_perm_kernel,
        grid=(n // tn, m // tm),
        in_specs=[pl.BlockSpec((tm, k), lambda j, i: (i, 0)),
                  pl.BlockSpec((None, k, tn), lambda j, i: (layer, 0, j)),
                  pl.BlockSpec((PERM_HALF, PERM_HALF), lambda j, i: (0, 0))],
        out_specs=[out_spec, out_spec],
        out_shape=[out, out],
        scratch_shapes=[pltpu.VMEM((k, tn), BF16)],
        compiler_params=pltpu.CompilerParams(
            dimension_semantics=("arbitrary", "arbitrary")),
        name="matmul_and_permuted",
    )(x, w, perm)


def _mm_res_norm_kernel(x_ref, w_ref, r_ref, g_ref, h_ref, xn_ref, wb_ref):
    @pl.when(pl.program_id(0) == 0)
    def _():
        wb_ref[...] = w_ref[...].astype(BF16)

    h = r_ref[...] + jnp.dot(x_ref[...], wb_ref[...], preferred_element_type=F32)
    h_ref[...] = h
    ms = jnp.mean(h * h, axis=-1, keepdims=True)
    xn_ref[...] = ((h * lax.rsqrt(ms + EPS)) * g_ref[...]).astype(xn_ref.dtype)


def _matmul_res_norm(x, w, layer, residual, norm_g, tm=512):
    m, k = x.shape
    n = w.shape[2]
    assert w.shape[1] == k and residual.shape == (m, n) and m % tm == 0
    row = pl.BlockSpec((tm, n), lambda i: (i, 0))
    return pl.pallas_call(
        _mm_res_norm_kernel,
        grid=(m // tm,),
        in_specs=[pl.BlockSpec((tm, k), lambda i: (i, 0)),
                  pl.BlockSpec((None, k, n), lambda i: (layer, 0, 0),
                               pipeline_mode=pl.Buffered(1)),
                  row,
                  pl.BlockSpec((1, n), lambda i: (0, 0))],
        out_specs=[row, row],
        out_shape=[jax.ShapeDtypeStruct((m, n), F32), jax.ShapeDtypeStruct((m, n), BF16)],
        scratch_shapes=[pltpu.VMEM((k, n), BF16)],
        compiler_params=pltpu.CompilerParams(dimension_semantics=("arbitrary",)),
        name="matmul_res_norm",
    )(x, w, residual, norm_g.reshape(1, n))


def _norm2_perm_kernel(x_ref, ga_ref, gb_ref, p_ref, a_ref, b_ref, bp_ref):
    x = x_ref[...]
    ms = jnp.mean(x * x, axis=-1, keepdims=True)
    y = x * lax.rsqrt(ms + EPS)
    a_ref[...] = (y * ga_ref[...]).astype(BF16)
    b = (y * gb_ref[...]).astype(BF16)
    b_ref[...] = b
    _store_permuted(p_ref[...], b, bp_ref, 0)


def _rmsnorm2_perm(x, g_a, g_b, perm):
    m, d = x.shape
    tm = PERM_TILE
    row = pl.BlockSpec((tm, d), lambda i: (i, 0))
    vec = pl.BlockSpec((1, d), lambda i: (0, 0))
    out = jax.ShapeDtypeStruct((m, d), BF16)
    return pl.pallas_call(
        _norm2_perm_kernel,
        grid=(m // tm,),
        in_specs=[row, vec, vec, pl.BlockSpec((PERM_HALF, PERM_HALF), lambda i: (0, 0))],
        out_specs=[row, row, row],
        out_shape=[out, out, out],
        compiler_params=pltpu.CompilerParams(dimension_semantics=("arbitrary",)),
        name="rmsnorm2_perm",
    )(x, g_a.reshape(1, d), g_b.reshape(1, d), perm)


def _norm_gate_in_kernel(x_ref, g_ref, w1_ref, xn_ref, a_ref):
    x = x_ref[...]
    ms = jnp.mean(x * x, axis=-1, keepdims=True)
    xn = ((x * lax.rsqrt(ms + EPS)) * g_ref[...]).astype(BF16)
    xn_ref[...] = xn
    a_ref[...] = jnp.dot(xn, w1_ref[...], preferred_element_type=F32).astype(BF16)


def _rmsnorm_gate_in(x, g, w_a1, tm=512):
    m, d = x.shape
    return pl.pallas_call(
        _norm_gate_in_kernel,
        grid=(m // tm,),
        in_specs=[pl.BlockSpec((tm, d), lambda i: (i, 0)),
                  pl.BlockSpec((1, d), lambda i: (0, 0)),
                  pl.BlockSpec((d, LANES), lambda i: (0, 0))],
        out_specs=[pl.BlockSpec((tm, d), lambda i: (i, 0)),
                   pl.BlockSpec((tm, LANES), lambda i: (i, 0))],
        out_shape=[jax.ShapeDtypeStruct((m, d), BF16), jax.ShapeDtypeStruct((m, LANES), BF16)],
        compiler_params=pltpu.CompilerParams(dimension_semantics=("arbitrary",)),
        name="rmsnorm_gate_in",
    )(x, g.reshape(1, d), w_a1)


def _split3(x):
    hi = x.astype(BF16)
    r1 = x - hi.astype(F32)
    mid = r1.astype(BF16)
    lo = (r1 - mid.astype(F32)).astype(BF16)
    return hi, mid, lo


def _dot01(mat01, x):
    hi, mid, lo = _split3(x)
    return (jnp.dot(mat01, hi, preferred_element_type=F32)
            + jnp.dot(mat01, mid, preferred_element_type=F32)
            + jnp.dot(mat01, lo, preferred_element_type=F32))


def _gla_kernel(q_ref, k_ref, v_ref, r_ref, a_ref, w2_ref, b2_ref, hn_ref, o_ref, st_ref):
    @pl.when(pl.program_id(2) == 0)
    def _():
        st_ref[...] = jnp.zeros_like(st_ref)

    nb = GLA_BLOCK
    heads = range(GLA_HEADS_PER_STEP)
    dk = lambda h: slice(h * GLA_DK, (h + 1) * GLA_DK)
    dv = lambda h: slice(h * GLA_DV, (h + 1) * GLA_DV)
    row = lax.broadcasted_iota(jnp.int32, (nb, nb), 0)
    col = lax.broadcasted_iota(jnp.int32, (nb, nb), 1)
    same_chunk = (row // GLA_CHUNK) == (col // GLA_CHUNK)
    causal = same_chunk & (col <= row)
    tri = jnp.where(causal, 1.0, 0.0).astype(BF16)
    nt = (((1,), (1,)), ((), ()))
    tn = (((0,), (0,)), ((), ()))

    a = a_ref[...]
    cum = []
    for h in heads:
        z = jnp.dot(a, w2_ref[:, dk(h)], preferred_element_type=F32) + b2_ref[:, dk(h)]
        log_sig = jnp.minimum(z, 0.0) - jnp.log(1.0 + jnp.exp(-jnp.abs(z)))
        log2_alpha = log_sig * (LOG2_E / GATE_NORMALIZER)
        cum.append(_dot01(tri, log2_alpha))
    k = [k_ref[:, dk(h)].astype(F32) for h in heads]
    q_dec = [((q_ref[:, dk(h)].astype(F32) * (GLA_DK ** -0.5)) * jnp.exp2(cum[h])).astype(BF16)
             for h in heads]
    k_inv = [(k[h] * jnp.exp2(-cum[h])).astype(BF16) for h in heads]
    scores = [lax.dot_general(q_dec[h], k_inv[h], nt, preferred_element_type=F32)
              for h in heads]
    scores = [jnp.where(causal, s, 0.0).astype(BF16) for s in scores]
    o_intra = [jnp.dot(scores[h], v_ref[:, dv(h)], preferred_element_type=F32)
               for h in heads]

    o_parts = [[] for _ in heads]
    for c in range(nb // GLA_CHUNK):
        rows = slice(c * GLA_CHUNK, (c + 1) * GLA_CHUNK)
        for h in heads:
            last_c = cum[h][(c + 1) * GLA_CHUNK - 1:(c + 1) * GLA_CHUNK]
            k_end = (k[h][rows] * jnp.exp2(last_c - cum[h][rows])).astype(BF16)
            st = st_ref[h]
            o_inter = lax.dot_general(q_dec[h][rows], st.astype(BF16), nt,
                                      preferred_element_type=F32)
            upd = lax.dot_general(v_ref[rows, dv(h)], k_end, tn,
                                  preferred_element_type=F32)
            st_ref[h] = st * jnp.exp2(last_c) + upd
            o_parts[h].append(o_intra[h][rows] + o_inter)

    for h in heads:
        o = jnp.concatenate(o_parts[h], axis=0)
        ms = jnp.mean(o * o, axis=-1, keepdims=True)
        o = (o * lax.rsqrt(ms + EPS)) * hn_ref[...]
        r = r_ref[:, dv(h)].astype(F32)
        gate = r * (1.0 / (1.0 + jnp.exp(-r)))
        o_ref[:, dv(h)] = (o * gate).astype(o_ref.dtype)


def _gla_recurrence(proj, a, w_a2, b_a2, head_norm, bsz, s_len):
    t = bsz * s_len
    nblk = s_len // GLA_BLOCK
    hps = GLA_HEADS_PER_STEP
    wk, wv = hps * GLA_DK, hps * GLA_DV
    k_off = GLA_KEY_DIM // wk
    v_off = 2 * GLA_KEY_DIM // wv
    r_off = (2 * GLA_KEY_DIM + GLA_VAL_DIM) // wv
    row = lambda b, g, i: b * nblk + i
    return pl.pallas_call(
        _gla_kernel,
        grid=(bsz, GLA_HEADS // hps, nblk),
        in_specs=[
            pl.BlockSpec((GLA_BLOCK, wk), lambda b, g, i: (row(b, g, i), g)),
            pl.BlockSpec((GLA_BLOCK, wk), lambda b, g, i: (row(b, g, i), k_off + g)),
            pl.BlockSpec((GLA_BLOCK, wv), lambda b, g, i: (row(b, g, i), v_off + g)),
            pl.BlockSpec((GLA_BLOCK, wv), lambda b, g, i: (row(b, g, i), r_off + g)),
            pl.BlockSpec((GLA_BLOCK, LANES), lambda b, g, i: (row(b, g, i), 0)),
            pl.BlockSpec((LANES, wk), lambda b, g, i: (0, g)),
            pl.BlockSpec((1, wk), lambda b, g, i: (0, g)),
            pl.BlockSpec((1, GLA_DV), lambda b, g, i: (0, 0)),
        ],
        out_specs=pl.BlockSpec((GLA_BLOCK, wv), lambda b, g, i: (row(b, g, i), g)),
        out_shape=jax.ShapeDtypeStruct((t, GLA_VAL_DIM), BF16),
        scratch_shapes=[pltpu.VMEM((hps, GLA_DV, GLA_DK), F32)],
        compiler_params=pltpu.CompilerParams(
            dimension_semantics=("arbitrary", "arbitrary", "arbitrary")),
        name="gla_recurrence",
    )(proj, proj, proj, proj, a, w_a2, b_a2.reshape(1, GLA_KEY_DIM), head_norm.reshape(1, GLA_DV))


def _half_perm_matrix():
    q = np.zeros((PERM_HALF, PERM_HALF), np.float32)
    for i in range(HALF_RUN):
        for a in range(4):
            for r4 in range(4):
                q[(4 * r4 + a) * HALF_RUN + i, 16 * i + 4 * a + r4] = 1.0
    return q


def _store_permuted(q, x, out_ref, base):
    for hf in range(2):
        y = jnp.dot(q, x[hf * PERM_HALF:(hf + 1) * PERM_HALF],
                    preferred_element_type=F32).astype(out_ref.dtype)
        for grp in range(16):
            dst = base + grp * PERM_RUN + hf * HALF_RUN
            out_ref[dst:dst + HALF_RUN, :] = y[grp * HALF_RUN:(grp + 1) * HALF_RUN]


def _gather_half(ref, hf, cols):
    return jnp.concatenate(
        [ref[grp * PERM_RUN + hf * HALF_RUN:grp * PERM_RUN + (hf + 1) * HALF_RUN, cols]
         for grp in range(16)], axis=0)


def _dsa_lead(layout, s, u):
    if layout == "rows":
        return (slice(s * ATT_BLOCK, (s + 1) * ATT_BLOCK),)
    if layout == "tile":
        return (s, slice(None), slice(None))
    assert layout == "seqs"
    return (slice(4 * s, 4 * s + 4), u, slice(None))


def _dsa_block_shape(layout, width):
    return (ATT_BLOCK, width) if layout == "rows" else (4, PERM_RUN, width)


def _dsa_kernel(q_ref, kc_ref, vc_ref, o_ref, lse_ref, kp_ref, vp_ref, bias_ref, *, dil, layout,
                qb, n_seq, slopes):
    n = pl.program_id(2)
    nk = 2 * ATT_BLOCK
    ck = lax.broadcasted_iota(jnp.int32, (ATT_BLOCK, nk), 1)
    is_cur = ck >= ATT_BLOCK

    @pl.when(n == 0)
    def _():
        kp_ref[...] = jnp.zeros_like(kp_ref)
        vp_ref[...] = jnp.zeros_like(vp_ref)

    @pl.when((pl.program_id(0) == 0) & (pl.program_id(1) == 0) & (n == 0))
    def _():
        rq = lax.broadcasted_iota(jnp.int32, (ATT_BLOCK, nk), 0)
        ck_in = ck & (ATT_BLOCK - 1)
        if layout == "tile":
            sub_q = 4 * (rq & (PERM_RUN - 1)) + (rq >> 5)
            sub_k = 4 * (ck_in & (PERM_RUN - 1)) + (ck_in >> 5)
        else:
            sub_q, sub_k = rq, ck_in
        j = sub_q - sub_k + jnp.where(is_cur, 0, ATT_BLOCK)
        valid = (j >= 0) & (j <= ATT_BLOCK)
        neg_dist = jnp.where(valid, -(j * dil).astype(F32), -jnp.inf)
        for h in range(ATT_HEADS):
            bias_ref[h] = (slopes[h] * LOG2_E) * neg_dist

    has_prev = is_cur | (n > 0)

    lane = lax.broadcasted_iota(jnp.int32, (ATT_BLOCK, LANES), 1)
    ones = jnp.ones((nk, HEAD_DIM), BF16)
    nt = (((1,), (1,)), ((), ()))

    def head(ref, s, u, h):
        x = ref[_dsa_lead(layout, s, u) + (slice(h * HEAD_DIM, (h + 1) * HEAD_DIM),)]
        return x.reshape(ATT_BLOCK, HEAD_DIM)

    def prev_head(ref, carry_ref, s, u, h):
        if s == 0:
            return carry_ref[:, h * HEAD_DIM:(h + 1) * HEAD_DIM]
        return head(ref, s - 1, u, h)

    for u, s in [(u, s) for u in range(n_seq) for s in range(qb)]:
        lead = _dsa_lead(layout, s, u)

        def scores(h, s=s, u=u):
            kcat = jnp.concatenate([prev_head(kc_ref, kp_ref, s, u, h), head(kc_ref, s, u, h)],
                                   axis=0)
            return lax.dot_general(head(q_ref, s, u, h), kcat, nt, preferred_element_type=F32)

        pending = [scores(h) for h in range(QK_LOOKAHEAD)]
        lse_tile = jnp.zeros((ATT_BLOCK, LANES), F32)
        for h in range(ATT_HEADS):
            sc = pending.pop(0)
            if h + QK_LOOKAHEAD < ATT_HEADS:
                pending.append(scores(h + QK_LOOKAHEAD))
            bias = bias_ref[h]
            if s == 0:
                bias = jnp.where(has_prev, bias, -jnp.inf)
            z = sc + bias
            m = jnp.max(z, axis=-1, keepdims=True)
            p = jnp.exp2(z - m).astype(BF16)
            vcat = jnp.concatenate([prev_head(vc_ref, vp_ref, s, u, h), head(vc_ref, s, u, h)],
                                   axis=0)
            acc = jnp.dot(p, jnp.concatenate([vcat, ones], axis=1),
                          preferred_element_type=F32)
            l = acc[:, HEAD_DIM:]
            o = (acc[:, :HEAD_DIM] / l).astype(o_ref.dtype)
            o_ref[lead + (slice(h * HEAD_DIM, (h + 1) * HEAD_DIM),)] = o.reshape(
                _dsa_block_shape(layout, HEAD_DIM))
            lse_tile = jnp.where(lane == h, m * LN_2 + jnp.log(l), lse_tile)
        lse_ref[lead + (slice(None),)] = lse_tile.reshape(_dsa_block_shape(layout, LANES))

    last = _dsa_lead(layout, qb - 1, n_seq - 1) + (slice(None),)
    kp_ref[...] = kc_ref[last].reshape(kp_ref.shape)
    vp_ref[...] = vc_ref[last].reshape(vp_ref.shape)


def _dsa_branch(q, q_col, kv, g, bsz, s_len):
    d = DILATIONS[g]
    assert WINDOWS[g] // d == ATT_BLOCK
    t = bsz * s_len
    ntile = s_len // PERM_TILE
    qb = min(DSA_BLOCKS_PER_STEP, s_len // d // ATT_BLOCK)
    n_seq = 1
    if g == 0:
        layout = "rows"
        view = lambda x: x.reshape(bsz, s_len, x.shape[-1])
        blk = lambda w: (None, qb * ATT_BLOCK, w)
        grid = (bsz, 1, s_len // (qb * ATT_BLOCK))
        at = lambda c: (lambda b, r, n: (b, n, c))
    elif g == 1:
        layout = "tile"
        view = lambda x: x.reshape(bsz, ntile, 4, 4, PERM_RUN, x.shape[-1])
        blk = lambda w: (None, qb, None, 4, PERM_RUN, w)
        grid = (bsz, 4, ntile // qb)
        at = lambda c: (lambda b, r, n: (b, n, r, 0, 0, c))
    else:
        layout = "seqs"
        n_seq = DSA_SEQS_PER_STEP
        assert ntile == 4 * qb
        view = lambda x: x.reshape(bsz, ntile, 16, PERM_RUN, x.shape[-1])
        blk = lambda w: (None, ntile, n_seq, PERM_RUN, w)
        grid = (bsz, 16 // n_seq, 1)
        at = lambda c: (lambda b, r, n: (b, 0, r, 0, c))
    qv, kvv = view(q), view(kv)
    o, lse = pl.pallas_call(
        functools.partial(_dsa_kernel, dil=d, layout=layout, qb=qb, n_seq=n_seq,
                          slopes=_alibi_slopes(ATT_HEADS)),
        grid=grid,
        in_specs=[
            pl.BlockSpec(blk(ATT_WIDTH), at(q_col)),
            pl.BlockSpec(blk(ATT_WIDTH), at(0)),
            pl.BlockSpec(blk(ATT_WIDTH), at(1)),
        ],
        out_specs=[pl.BlockSpec(blk(ATT_WIDTH), at(0)),
                   pl.BlockSpec(blk(LANES), at(0))],
        out_shape=[jax.ShapeDtypeStruct(qv.shape[:-1] + (ATT_WIDTH,), BF16),
                   jax.ShapeDtypeStruct(qv.shape[:-1] + (LANES,), F32)],
        scratch_shapes=[pltpu.VMEM((ATT_BLOCK, ATT_WIDTH), BF16),
                        pltpu.VMEM((ATT_BLOCK, ATT_WIDTH), BF16),
                        pltpu.VMEM((ATT_HEADS, ATT_BLOCK, 2 * ATT_BLOCK), F32)],
        compiler_params=pltpu.CompilerParams(
            dimension_semantics=("arbitrary", "arbitrary", "arbitrary")),
        name=f"dsa_branch{g}",
    )(qv, kvv, kvv)
    return o.reshape(t, ATT_WIDTH), lse.reshape(t, LANES)


def _merge_kernel(qt_ref, o0_ref, o1_ref, o2_ref, l0_ref, l1_ref, l2_ref, o_ref):
    qt = qt_ref[...]
    every = slice(None)
    for hf in range(2):
        rows = slice(hf * PERM_HALF, (hf + 1) * PERM_HALF)
        l0 = l0_ref[rows, :]
        l12 = _dot01(qt, jnp.concatenate([_gather_half(l1_ref, hf, every),
                                          _gather_half(l2_ref, hf, every)], axis=1))
        l1, l2 = l12[:, :LANES], l12[:, LANES:]
        m = jnp.maximum(jnp.maximum(l0, l1), l2)
        e0, e1, e2 = jnp.exp(l0 - m), jnp.exp(l1 - m), jnp.exp(l2 - m)
        den = e0 + e1 + e2
        w1, w2 = e1 / den, e2 / den
        for hp in range(ATT_HEADS // 2):
            pair = slice(2 * hp * HEAD_DIM, (2 * hp + 2) * HEAD_DIM)
            o1 = jnp.dot(qt, _gather_half(o1_ref, hf, pair), preferred_element_type=F32)
            o2 = jnp.dot(qt, _gather_half(o2_ref, hf, pair), preferred_element_type=F32)
            for k in range(2):
                h = 2 * hp + k
                sl = slice(h * HEAD_DIM, (h + 1) * HEAD_DIM)
                in_pair = slice(k * HEAD_DIM, (k + 1) * HEAD_DIM)
                o0 = o0_ref[rows, sl].astype(F32)
                o = (o0 + w1[:, h:h + 1] * (o1[:, in_pair] - o0)
                     + w2[:, h:h + 1] * (o2[:, in_pair] - o0))
                o_ref[rows, sl] = o.astype(o_ref.dtype)


def _dsa_merge(outs, lses, perm_t):
    t, hd = outs[0].shape
    tm = PERM_TILE
    o_spec = pl.BlockSpec((tm, hd), lambda i: (i, 0))
    l_spec = pl.BlockSpec((tm, LANES), lambda i: (i, 0))
    return pl.pallas_call(
        _merge_kernel,
        grid=(t // tm,),
        in_specs=[pl.BlockSpec((PERM_HALF, PERM_HALF), lambda i: (0, 0)),
                  o_spec, o_spec, o_spec, l_spec, l_spec, l_spec],
        out_specs=o_spec,
        out_shape=jax.ShapeDtypeStruct((t, hd), BF16),
        compiler_params=pltpu.CompilerParams(dimension_semantics=("arbitrary",)),
        name="dsa_merge",
    )(perm_t, *outs, *lses)


def _ffn_up_kernel(x_ref, wu_ref, wg_ref, cw_ref, cb_ref, wd_ref, o_ref, wdb_ref, wub_ref, wgb_ref,
                   us_ref, gs_ref, carry_ref, *, tm, tn, s_len):
    wdb_ref[...] = wd_ref[...].astype(BF16)

    @pl.when(pl.program_id(1) == 0)
    def _():
        wub_ref[...] = wu_ref[...].astype(BF16)
        wgb_ref[...] = wg_ref[...].astype(BF16)

    @pl.when((pl.program_id(1) * tm) % s_len == 0)
    def _():
        carry_ref[...] = jnp.zeros_like(carry_ref)

    n_rb = tm // FFN_ROW_SUB
    subs = [(cb, rb) for cb in range(tn // FFN_COL_SUB) for rb in range(n_rb)]

    def window(idx):
        cb, rb = subs[idx]
        return (slice(rb * FFN_ROW_SUB, (rb + 1) * FFN_ROW_SUB),
                slice(cb * FFN_COL_SUB, (cb + 1) * FFN_COL_SUB))

    def project(idx):
        rows, cols = window(idx)
        slot = idx % FFN_SLOTS
        xr = x_ref[rows, :]
        us_ref[slot] = jnp.dot(xr, wub_ref[:, cols], preferred_element_type=F32)
        gs_ref[slot, 8:] = jnp.dot(xr, wgb_ref[:, cols], preferred_element_type=F32)
        if subs[idx][1] == 0:
            gs_ref[slot, :8] = carry_ref[:, cols]
        else:
            gs_ref[slot, :8] = gs_ref[(idx - 1) % FFN_SLOTS, FFN_ROW_SUB:]
        if subs[idx][1] == n_rb - 1:
            carry_ref[:, cols] = gs_ref[slot, FFN_ROW_SUB:]

    def epilogue(idx):
        rows, cols = window(idx)
        slot = idx % FFN_SLOTS
        g = gs_ref[slot, 8:]
        g1 = gs_ref[slot, 7:7 + FFN_ROW_SUB]
        g2 = gs_ref[slot, 6:6 + FFN_ROW_SUB]
        cw = 0.5 * cw_ref[:, cols]
        half = cw[0:1] * g2 + cw[1:2] * g1 + cw[2:3] * g + 0.5 * cb_ref[:, cols]
        act = half * (1.0 + lax.erf(half * (2.0 ** 0.5)))
        o_ref[rows, cols] = (act * us_ref[slot]).astype(o_ref.dtype)

    ahead = FFN_SLOTS - 1
    for idx in range(ahead):
        project(idx)
    for idx in range(len(subs)):
        if idx + ahead < len(subs):
            project(idx + ahead)
        epilogue(idx)


def _ffn_up(xn, w_up, layer, conv_w, conv_b, w_down, s_len, tm=2048, tn=512):
    m, k = xn.shape
    nn = D_FF // tn
    n_steps = nn * (m // tm)
    slab = D_FF // n_steps
    assert s_len % tm == 0 and tm % FFN_ROW_SUB == 0 and tn % FFN_COL_SUB == 0
    assert slab * n_steps == D_FF and slab % 16 == 0
    d_out = w_down.shape[2]
    step = lambda j, i: j * (m // tm) + i
    return pl.pallas_call(
        functools.partial(_ffn_up_kernel, tm=tm, tn=tn, s_len=s_len),
        grid=(nn, m // tm),
        in_specs=[
            pl.BlockSpec((tm, k), lambda j, i: (i, 0)),
            pl.BlockSpec((None, k, tn), lambda j, i: (layer, 0, j)),
            pl.BlockSpec((None, k, tn), lambda j, i: (layer, 0, nn + j)),
            pl.BlockSpec((3, tn), lambda j, i: (0, j)),
            pl.BlockSpec((1, tn), lambda j, i: (0, j)),
            pl.BlockSpec((None, slab, d_out), lambda j, i: (layer, step(j, i), 0)),
        ],
        out_specs=[pl.BlockSpec((tm, tn), lambda j, i: (i, j)),
                   pl.BlockSpec((None, slab, d_out), lambda j, i: (0, step(j, i), 0))],
        out_shape=[jax.ShapeDtypeStruct((m, D_FF), BF16),
                   jax.ShapeDtypeStruct((1, D_FF, d_out), BF16)],
        scratch_shapes=[pltpu.VMEM((k, tn), BF16), pltpu.VMEM((k, tn), BF16),
                        pltpu.VMEM((FFN_SLOTS, FFN_ROW_SUB, FFN_COL_SUB), F32),
                        pltpu.VMEM((FFN_SLOTS, 8 + FFN_ROW_SUB, FFN_COL_SUB), F32),
                        pltpu.VMEM((8, tn), F32)],
        compiler_params=pltpu.CompilerParams(
            dimension_semantics=("arbitrary", "arbitrary")),
        name="ffn_up",
    )(xn, w_up, w_up, conv_w, conv_b.reshape(1, D_FF), w_down)


def _conv_glu(h, xn, w_up, layer, conv_w, conv_b, w_down, s_len):
    act, w_down_bf16 = _ffn_up(xn, w_up, layer, conv_w, conv_b, w_down, s_len)
    return _matmul(act, w_down_bf16, 0, F32, residual=h, tn=1024)


def kernel(x, attn_norm, gla_w_in, gla_w_a2, gla_b_a2, gla_head_norm, gla_w_out, kv_norm, w_kv,
           dsa_w_q, dsa_w_out, ffn_norm, ffn_w_up, ffn_conv_w, ffn_conv_b, ffn_w_down, final_norm):
    bsz, s_len, d = x.shape
    t = bsz * s_len
    h = x.reshape(t, d)

    n_main = 2 * GLA_KEY_DIM + 2 * GLA_VAL_DIM
    w_in = gla_w_in.astype(BF16)
    w_a1 = jnp.pad(w_in[0, :, n_main:], ((0, 0), (0, LANES - GATE_RANK)))
    w_a2 = jnp.pad(gla_w_a2[0].astype(BF16), ((0, LANES - GATE_RANK), (0, 0)))
    xn, a = _rmsnorm_gate_in(h, attn_norm[0], w_a1)
    proj = _matmul(xn, w_in, 0, BF16, n_cols=n_main, tm=1024, tn=2048)
    o = _gla_recurrence(proj, a, w_a2, gla_b_a2[0], gla_head_norm[0], bsz, s_len)
    h, xn = _matmul_res_norm(o, gla_w_out, 0, h, ffn_norm[0])
    h = _conv_glu(h, xn, ffn_w_up, 0, ffn_conv_w[0], ffn_conv_b[0], ffn_w_down, s_len)

    perm_np = _half_perm_matrix()
    perm = jnp.asarray(perm_np, BF16)
    perm_t = jnp.asarray(perm_np.T, BF16)
    xkv, xq, xq_perm = _rmsnorm2_perm(h, kv_norm, attn_norm[1], perm)
    kv, kv_perm = _matmul_and_permuted(xkv, w_kv[None], 0, perm)

    q_scale = HEAD_DIM ** -0.5 * LOG2_E
    q0 = _matmul(xq, dsa_w_q, 0, BF16, n_cols=ATT_WIDTH, out_scale=q_scale, tm=1024)
    q12 = _matmul(xq_perm, dsa_w_q, 0, BF16, n_cols=2 * ATT_WIDTH, col_start=ATT_WIDTH,
                  out_scale=q_scale, tm=1024)
    o0, lse0 = _dsa_branch(q0, 0, kv, 0, bsz, s_len)
    o1, lse1 = _dsa_branch(q12, 0, kv_perm, 1, bsz, s_len)
    o2, lse2 = _dsa_branch(q12, 1, kv_perm, 2, bsz, s_len)
    o = _dsa_merge([o0, o1, o2], [lse0, lse1, lse2], perm_t)
    h, xn = _matmul_res_norm(o, dsa_w_out, 0, h, ffn_norm[1])
    h = _conv_glu(h, xn, ffn_w_up, 1, ffn_conv_w[1], ffn_conv_b[1], ffn_w_down, s_len)

    return _rmsnorm(h, final_norm, F32).reshape(bsz, s_len, d)
```

```python
import functools
import math

import numpy as np
import jax
import jax.numpy as jnp
from jax import lax
from jax.experimental import pallas as pl
from jax.experimental.pallas import tpu as pltpu

D_MODEL = 2048
GLA_HEADS = 4
GLA_KEY_DIM = 1024
GLA_VAL_DIM = 2048
GLA_DK = 256
GLA_DV = 512
GATE_RANK = 16
GATE_NORMALIZER = 16.0
GLA_CHUNK = 64
GLA_BLOCK = 256
GLA_HEADS_PER_STEP = 4
ATT_HEADS = 16
HEAD_DIM = 128
ATT_WIDTH = ATT_HEADS * HEAD_DIM
WINDOWS = (128, 512, 2048)
DILATIONS = (1, 4, 16)
ATT_BLOCK = 128
DSA_BLOCKS_PER_STEP = 4
DSA_SEQS_PER_STEP = 2
QK_LOOKAHEAD = 2
PERM_TILE = 512
PERM_RUN = PERM_TILE // 16
PERM_HALF = PERM_TILE // 2
HALF_RUN = PERM_RUN // 2
D_FF = 5632
EPS = 1e-6
LOG2_E = math.log2(math.e)
LN_2 = math.log(2.0)
LANES = 128
FFN_ROW_SUB = 512
FFN_COL_SUB = 256
FFN_SLOTS = 2

F32 = jnp.float32
BF16 = jnp.bfloat16


def _alibi_slopes(n):
    def pow2_slopes(m):
        start = 2.0 ** (-8.0 / m)
        return [start ** (i + 1) for i in range(m)]
    assert math.log2(n).is_integer()
    return [float(v) for v in np.array(pow2_slopes(n), dtype=np.float32)]


def _rmsnorm_kernel(x_ref, g_ref, o_ref):
    x = x_ref[...]
    ms = jnp.mean(x * x, axis=-1, keepdims=True)
    o_ref[...] = ((x * lax.rsqrt(ms + EPS)) * g_ref[...]).astype(o_ref.dtype)


def _rmsnorm(x, g, out_dtype, tm=1024):
    m, d = x.shape
    return pl.pallas_call(
        _rmsnorm_kernel,
        grid=(m // tm,),
        in_specs=[pl.BlockSpec((tm, d), lambda i: (i, 0)),
                  pl.BlockSpec((1, d), lambda i: (0, 0))],
        out_specs=pl.BlockSpec((tm, d), lambda i: (i, 0)),
        out_shape=jax.ShapeDtypeStruct((m, d), out_dtype),
        compiler_params=pltpu.CompilerParams(dimension_semantics=("arbitrary",)),
        name="rmsnorm",
    )(x, g.reshape(1, d))


def _cast_weights_once(w_ref, wb_ref):
    @pl.when(pl.program_id(1) == 0)
    def _():
        wb_ref[...] = w_ref[...].astype(BF16)


def _mm_kernel(x_ref, w_ref, *rest, out_scale, has_residual, cast_weights):
    o_ref = rest[1 if has_residual else 0]
    if cast_weights:
        _cast_weights_once(w_ref, rest[-1])
        w_ref = rest[-1]
    acc = jnp.dot(x_ref[...], w_ref[...], preferred_element_type=F32)
    if has_residual:
        acc = rest[0][...] + acc
    if out_scale is not None:
        acc = acc * out_scale
    o_ref[...] = acc.astype(o_ref.dtype)


def _matmul(x, w, layer, out_dtype, n_cols=None, col_start=0, residual=None, out_scale=None,
            tm=512, tn=1024):
    m, k = x.shape
    cast_weights = w.dtype != BF16
    n = w.shape[2] if n_cols is None else n_cols
    assert w.shape[1] == k and m % tm == 0 and n % tn == 0 and col_start % tn == 0
    j0 = col_start // tn
    in_specs = [pl.BlockSpec((tm, k), lambda j, i: (i, 0)),
                pl.BlockSpec((None, k, tn), lambda j, i: (layer, 0, j0 + j))]
    args = [x, w]
    if residual is not None:
        in_specs.append(pl.BlockSpec((tm, tn), lambda j, i: (i, j)))
        args.append(residual)
    return pl.pallas_call(
        functools.partial(_mm_kernel, out_scale=out_scale, has_residual=residual is not None,
                          cast_weights=cast_weights),
        grid=(n // tn, m // tm),
        in_specs=in_specs,
        out_specs=pl.BlockSpec((tm, tn), lambda j, i: (i, j)),
        out_shape=jax.ShapeDtypeStruct((m, n), out_dtype),
        scratch_shapes=[pltpu.VMEM((k, tn), BF16)] if cast_weights else [],
        compiler_params=pltpu.CompilerParams(
            dimension_semantics=("arbitrary", "arbitrary")),
        name="matmul",
    )(*args)


def _mm_perm_kernel(x_ref, w_ref, p_ref, o_ref, op_ref, wb_ref):
    _cast_weights_once(w_ref, wb_ref)
    out = jnp.dot(x_ref[...], wb_ref[...], preferred_element_type=F32).astype(BF16)
    o_ref[...] = out
    for t in range(o_ref.shape[0] // PERM_TILE):
        _store_permuted(p_ref[...], out[t * PERM_TILE:(t + 1) * PERM_TILE], op_ref, t * PERM_TILE)


def _matmul_and_permuted(x, w, layer, perm, tm=1024, tn=1024):
    m, k = x.shape
    n = w.shape[2]
    assert w.shape[1] == k and m % tm == 0 and n % tn == 0 and tm % PERM_TILE == 0
    out_spec = pl.BlockSpec((tm, tn), lambda j, i: (i, j))
    out = jax.ShapeDtypeStruct((m, n), BF16)
    return pl.pallas_call(
        _mm_perm_kernel,
        grid=(n // tn, m // tm),
        in_specs=[pl.BlockSpec((tm, k), lambda j, i: (i, 0)),
                  pl.BlockSpec((None, k, tn), lambda j, i: (layer, 0, j)),
                  pl.BlockSpec((PERM_HALF, PERM_HALF), lambda j, i: (0, 0))],
        out_specs=[out_spec, out_spec],
        out_shape=[out, out],
        scratch_shapes=[pltpu.VMEM((k, tn), BF16)],
        compiler_params=pltpu.CompilerParams(
            dimension_semantics=("arbitrary", "arbitrary")),
        name="matmul_and_permuted",
    )(x, w, perm)


def _mm_res_norm_kernel(x_ref, w_ref, r_ref, g_ref, h_ref, xn_ref, wb_ref):
    @pl.when(pl.program_id(0) == 0)
    def _():
        wb_ref[...] = w_ref[...].astype(BF16)

    h = r_ref[...] + jnp.dot(x_ref[...], wb_ref[...], preferred_element_type=F32)
    h_ref[...] = h
    ms = jnp.mean(h * h, axis=-1, keepdims=True)
    xn_ref[...] = ((h * lax.rsqrt(ms + EPS)) * g_ref[...]).astype(xn_ref.dtype)


def _matmul_res_norm(x, w, layer, residual, norm_g, tm=512):
    m, k = x.shape
    n = w.shape[2]
    assert w.shape[1] == k and residual.shape == (m, n) and m % tm == 0
    row = pl.BlockSpec((tm, n), lambda i: (i, 0))
    return pl.pallas_call(
        _mm_res_norm_kernel,
        grid=(m // tm,),
        in_specs=[pl.BlockSpec((tm, k), lambda i: (i, 0)),
                  pl.BlockSpec((None, k, n), lambda i: (layer, 0, 0),
                               pipeline_mode=pl.Buffered(1)),
                  row,
                  pl.BlockSpec((1, n), lambda i: (0, 0))],
        out_specs=[row, row],
        out_shape=[jax.ShapeDtypeStruct((m, n), F32), jax.ShapeDtypeStruct((m, n), BF16)],
        scratch_shapes=[pltpu.VMEM((k, n), BF16)],
        compiler_params=pltpu.CompilerParams(dimension_semantics=("arbitrary",)),
        name="matmul_res_norm",
    )(x, w, residual, norm_g.reshape(1, n))


def _norm2_perm_kernel(x_ref, ga_ref, gb_ref, p_ref, a_ref, b_ref, bp_ref):
    x = x_ref[...]
    ms = jnp.mean(x * x, axis=-1, keepdims=True)
    y = x * lax.rsqrt(ms + EPS)
    a_ref[...] = (y * ga_ref[...]).astype(BF16)
    b = (y * gb_ref[...]).astype(BF16)
    b_ref[...] = b
    for t in range(b_ref.shape[0] // PERM_TILE):
        _store_permuted(p_ref[...], b[t * PERM_TILE:(t + 1) * PERM_TILE], bp_ref, t * PERM_TILE)


def _rmsnorm2_perm(x, g_a, g_b, perm, tm=1024):
    m, d = x.shape
    assert m % tm == 0 and tm % PERM_TILE == 0
    row = pl.BlockSpec((tm, d), lambda i: (i, 0))
    vec = pl.BlockSpec((1, d), lambda i: (0, 0))
    out = jax.ShapeDtypeStruct((m, d), BF16)
    return pl.pallas_call(
        _norm2_perm_kernel,
        grid=(m // tm,),
        in_specs=[row, vec, vec, pl.BlockSpec((PERM_HALF, PERM_HALF), lambda i: (0, 0))],
        out_specs=[row, row, row],
        out_shape=[out, out, out],
        compiler_params=pltpu.CompilerParams(dimension_semantics=("arbitrary",)),
        name="rmsnorm2_perm",
    )(x, g_a.reshape(1, d), g_b.reshape(1, d), perm)


def _norm_gate_in_kernel(x_ref, g_ref, w1_ref, xn_ref, a_ref):
    x = x_ref[...]
    ms = jnp.mean(x * x, axis=-1, keepdims=True)
    xn = ((x * lax.rsqrt(ms + EPS)) * g_ref[...]).astype(BF16)
    xn_ref[...] = xn
    a_ref[...] = jnp.dot(xn, w1_ref[...], preferred_element_type=F32).astype(BF16)


def _rmsnorm_gate_in(x, g, w_a1, tm=1024):
    m, d = x.shape
    return pl.pallas_call(
        _norm_gate_in_kernel,
        grid=(m // tm,),
        in_specs=[pl.BlockSpec((tm, d), lambda i: (i, 0)),
                  pl.BlockSpec((1, d), lambda i: (0, 0)),
                  pl.BlockSpec((d, LANES), lambda i: (0, 0))],
        out_specs=[pl.BlockSpec((tm, d), lambda i: (i, 0)),
                   pl.BlockSpec((tm, LANES), lambda i: (i, 0))],
        out_shape=[jax.ShapeDtypeStruct((m, d), BF16), jax.ShapeDtypeStruct((m, LANES), BF16)],
        compiler_params=pltpu.CompilerParams(dimension_semantics=("arbitrary",)),
        name="rmsnorm_gate_in",
    )(x, g.reshape(1, d), w_a1)


def _split3(x):
    hi = x.astype(BF16)
    r1 = x - hi.astype(F32)
    mid = r1.astype(BF16)
    lo = (r1 - mid.astype(F32)).astype(BF16)
    return hi, mid, lo


def _dot01(mat01, x):
    hi, mid, lo = _split3(x)
    return (jnp.dot(mat01, hi, preferred_element_type=F32)
            + jnp.dot(mat01, mid, preferred_element_type=F32)
            + jnp.dot(mat01, lo, preferred_element_type=F32))


def _gla_kernel(q_ref, k_ref, v_ref, r_ref, a_ref, w2_ref, b2_ref, hn_ref, o_ref, st_ref):
    @pl.when(pl.program_id(2) == 0)
    def _():
        st_ref[...] = jnp.zeros_like(st_ref)

    nb = GLA_BLOCK
    heads = range(GLA_HEADS_PER_STEP)
    dk = lambda h: slice(h * GLA_DK, (h + 1) * GLA_DK)
    dv = lambda h: slice(h * GLA_DV, (h + 1) * GLA_DV)
    row = lax.broadcasted_iota(jnp.int32, (nb, nb), 0)
    col = lax.broadcasted_iota(jnp.int32, (nb, nb), 1)
    same_chunk = (row // GLA_CHUNK) == (col // GLA_CHUNK)
    causal = same_chunk & (col <= row)
    tri = jnp.where(causal, 1.0, 0.0).astype(BF16)
    nt = (((1,), (1,)), ((), ()))
    tn = (((0,), (0,)), ((), ()))

    a = a_ref[...]
    cum = []
    for h in heads:
        z = jnp.dot(a, w2_ref[:, dk(h)], preferred_element_type=F32) + b2_ref[:, dk(h)]
        log_sig = jnp.minimum(z, 0.0) - jnp.log(1.0 + jnp.exp(-jnp.abs(z)))
        log2_alpha = log_sig * (LOG2_E / GATE_NORMALIZER)
        cum.append(_dot01(tri, log2_alpha))
    k = [k_ref[:, dk(h)].astype(F32) for h in heads]
    q_dec = [((q_ref[:, dk(h)].astype(F32) * (GLA_DK ** -0.5)) * jnp.exp2(cum[h])).astype(BF16)
             for h in heads]
    k_inv = [(k[h] * jnp.exp2(-cum[h])).astype(BF16) for h in heads]
    scores = [lax.dot_general(q_dec[h], k_inv[h], nt, preferred_element_type=F32)
              for h in heads]
    scores = [jnp.where(causal, s, 0.0).astype(BF16) for s in scores]
    o_intra = [jnp.dot(scores[h], v_ref[:, dv(h)], preferred_element_type=F32)
               for h in heads]

    o_parts = [[] for _ in heads]
    for c in range(nb // GLA_CHUNK):
        rows = slice(c * GLA_CHUNK, (c + 1) * GLA_CHUNK)
        for h in heads:
            last_c = cum[h][(c + 1) * GLA_CHUNK - 1:(c + 1) * GLA_CHUNK]
            k_end = (k[h][rows] * jnp.exp2(last_c - cum[h][rows])).astype(BF16)
            st = st_ref[h]
            o_inter = lax.dot_general(q_dec[h][rows], st.astype(BF16), nt,
                                      preferred_element_type=F32)
            upd = lax.dot_general(v_ref[rows, dv(h)], k_end, tn,
                                  preferred_element_type=F32)
            st_ref[h] = st * jnp.exp2(last_c) + upd
            o_parts[h].append(o_intra[h][rows] + o_inter)

    for h in heads:
        o = jnp.concatenate(o_parts[h], axis=0)
        ms = jnp.mean(o * o, axis=-1, keepdims=True)
        o = (o * lax.rsqrt(ms + EPS)) * hn_ref[...]
        r = r_ref[:, dv(h)].astype(F32)
        gate = r * (1.0 / (1.0 + jnp.exp(-r)))
        o_ref[:, dv(h)] = (o * gate).astype(o_ref.dtype)


def _gla_recurrence(proj, a, w_a2, b_a2, head_norm, bsz, s_len):
    t = bsz * s_len
    nblk = s_len // GLA_BLOCK
    hps = GLA_HEADS_PER_STEP
    wk, wv = hps * GLA_DK, hps * GLA_DV
    k_off = GLA_KEY_DIM // wk
    v_off = 2 * GLA_KEY_DIM // wv
    r_off = (2 * GLA_KEY_DIM + GLA_VAL_DIM) // wv
    row = lambda b, g, i: b * nblk + i
    return pl.pallas_call(
        _gla_kernel,
        grid=(bsz, GLA_HEADS // hps, nblk),
        in_specs=[
            pl.BlockSpec((GLA_BLOCK, wk), lambda b, g, i: (row(b, g, i), g)),
            pl.BlockSpec((GLA_BLOCK, wk), lambda b, g, i: (row(b, g, i), k_off + g)),
            pl.BlockSpec((GLA_BLOCK, wv), lambda b, g, i: (row(b, g, i), v_off + g)),
            pl.BlockSpec((GLA_BLOCK, wv), lambda b, g, i: (row(b, g, i), r_off + g)),
            pl.BlockSpec((GLA_BLOCK, LANES), lambda b, g, i: (row(b, g, i), 0)),
            pl.BlockSpec((LANES, wk), lambda b, g, i: (0, g)),
            pl.BlockSpec((1, wk), lambda b, g, i: (0, g)),
            pl.BlockSpec((1, GLA_DV), lambda b, g, i: (0, 0)),
        ],
        out_specs=pl.BlockSpec((GLA_BLOCK, wv), lambda b, g, i: (row(b, g, i), g)),
        out_shape=jax.ShapeDtypeStruct((t, GLA_VAL_DIM), BF16),
        scratch_shapes=[pltpu.VMEM((hps, GLA_DV, GLA_DK), F32)],
        compiler_params=pltpu.CompilerParams(
            dimension_semantics=("arbitrary", "arbitrary", "arbitrary")),
        name="gla_recurrence",
    )(proj, proj, proj, proj, a, w_a2, b_a2.reshape(1, GLA_KEY_DIM), head_norm.reshape(1, GLA_DV))


def _half_perm_matrix():
    q = np.zeros((PERM_HALF, PERM_HALF), np.float32)
    for i in range(HALF_RUN):
        for a in range(4):
            for r4 in range(4):
                q[(4 * r4 + a) * HALF_RUN + i, 16 * i + 4 * a + r4] = 1.0
    return q


def _store_permuted(q, x, out_ref, base):
    for hf in range(2):
        y = jnp.dot(q, x[hf * PERM_HALF:(hf + 1) * PERM_HALF],
                    preferred_element_type=F32).astype(out_ref.dtype)
        for grp in range(16):
            dst = base + grp * PERM_RUN + hf * HALF_RUN
            out_ref[dst:dst + HALF_RUN, :] = y[grp * HALF_RUN:(grp + 1) * HALF_RUN]


def _gather_half(ref, base, hf, cols):
    return jnp.concatenate(
        [ref[base + grp * PERM_RUN + hf * HALF_RUN:base + grp * PERM_RUN + (hf + 1) * HALF_RUN, cols]
         for grp in range(16)], axis=0)


def _dsa_lead(layout, s, u):
    if layout == "rows":
        return (slice(s * ATT_BLOCK, (s + 1) * ATT_BLOCK),)
    if layout == "tile":
        return (s, slice(None), slice(None))
    assert layout == "seqs"
    return (slice(4 * s, 4 * s + 4), u, slice(None))


def _dsa_block_shape(layout, width):
    return (ATT_BLOCK, width) if layout == "rows" else (4, PERM_RUN, width)


def _dsa_kernel(q_ref, kc_ref, vc_ref, o_ref, lse_ref, kp_ref, vp_ref, bias_ref, *, dil, layout,
                qb, n_seq, slopes):
    n = pl.program_id(2)
    nk = 2 * ATT_BLOCK
    ck = lax.broadcasted_iota(jnp.int32, (ATT_BLOCK, nk), 1)
    is_cur = ck >= ATT_BLOCK

    @pl.when(n == 0)
    def _():
        kp_ref[...] = jnp.zeros_like(kp_ref)
        vp_ref[...] = jnp.zeros_like(vp_ref)

    @pl.when((pl.program_id(0) == 0) & (pl.program_id(1) == 0) & (n == 0))
    def _():
        rq = lax.broadcasted_iota(jnp.int32, (ATT_BLOCK, nk), 0)
        ck_in = ck & (ATT_BLOCK - 1)
        if layout == "tile":
            sub_q = 4 * (rq & (PERM_RUN - 1)) + (rq >> 5)
            sub_k = 4 * (ck_in & (PERM_RUN - 1)) + (ck_in >> 5)
        else:
            sub_q, sub_k = rq, ck_in
        j = sub_q - sub_k + jnp.where(is_cur, 0, ATT_BLOCK)
        valid = (j >= 0) & (j <= ATT_BLOCK)
        neg_dist = jnp.where(valid, -(j * dil).astype(F32), -jnp.inf)
        for h in range(ATT_HEADS):
            bias_ref[h] = (slopes[h] * LOG2_E) * neg_dist

    has_prev = is_cur | (n > 0)

    lane = lax.broadcasted_iota(jnp.int32, (ATT_BLOCK, LANES), 1)
    ones = jnp.ones((nk, HEAD_DIM), BF16)
    nt = (((1,), (1,)), ((), ()))

    def head(ref, s, u, h):
        x = ref[_dsa_lead(layout, s, u) + (slice(h * HEAD_DIM, (h + 1) * HEAD_DIM),)]
        return x.reshape(ATT_BLOCK, HEAD_DIM)

    def prev_head(ref, carry_ref, s, u, h):
        if s == 0:
            return carry_ref[:, h * HEAD_DIM:(h + 1) * HEAD_DIM]
        return head(ref, s - 1, u, h)

    for u, s in [(u, s) for u in range(n_seq) for s in range(qb)]:
        lead = _dsa_lead(layout, s, u)

        def scores(h, s=s, u=u):
            kcat = jnp.concatenate([prev_head(kc_ref, kp_ref, s, u, h), head(kc_ref, s, u, h)],
                                   axis=0)
            return lax.dot_general(head(q_ref, s, u, h), kcat, nt, preferred_element_type=F32)

        pending = [scores(h) for h in range(QK_LOOKAHEAD)]
        lse_tile = jnp.zeros((ATT_BLOCK, LANES), F32)
        for h in range(ATT_HEADS):
            sc = pending.pop(0)
            if h + QK_LOOKAHEAD < ATT_HEADS:
                pending.append(scores(h + QK_LOOKAHEAD))
            bias = bias_ref[h]
            if s == 0:
                bias = jnp.where(has_prev, bias, -jnp.inf)
            z = sc + bias
            m = jnp.max(z, axis=-1, keepdims=True)
            p = jnp.exp2(z - m).astype(BF16)
            vcat = jnp.concatenate([prev_head(vc_ref, vp_ref, s, u, h), head(vc_ref, s, u, h)],
                                   axis=0)
            acc = jnp.dot(p, jnp.concatenate([vcat, ones], axis=1),
                          preferred_element_type=F32)
            l = acc[:, HEAD_DIM:]
            o = (acc[:, :HEAD_DIM] / l).astype(o_ref.dtype)
            o_ref[lead + (slice(h * HEAD_DIM, (h + 1) * HEAD_DIM),)] = o.reshape(
                _dsa_block_shape(layout, HEAD_DIM))
            lse_tile = jnp.where(lane == h, m * LN_2 + jnp.log(l), lse_tile)
        lse_ref[lead + (slice(None),)] = lse_tile.reshape(_dsa_block_shape(layout, LANES))

    last = _dsa_lead(layout, qb - 1, n_seq - 1) + (slice(None),)
    kp_ref[...] = kc_ref[last].reshape(kp_ref.shape)
    vp_ref[...] = vc_ref[last].reshape(vp_ref.shape)


def _dsa_branch(q, q_col, kv, g, bsz, s_len):
    d = DILATIONS[g]
    assert WINDOWS[g] // d == ATT_BLOCK
    t = bsz * s_len
    ntile = s_len // PERM_TILE
    qb = min(DSA_BLOCKS_PER_STEP, s_len // d // ATT_BLOCK)
    n_seq = 1
    if g == 0:
        layout = "rows"
        view = lambda x: x.reshape(bsz, s_len, x.shape[-1])
        blk = lambda w: (None, qb * ATT_BLOCK, w)
        grid = (bsz, 1, s_len // (qb * ATT_BLOCK))
        at = lambda c: (lambda b, r, n: (b, n, c))
    elif g == 1:
        layout = "tile"
        view = lambda x: x.reshape(bsz, ntile, 4, 4, PERM_RUN, x.shape[-1])
        blk = lambda w: (None, qb, None, 4, PERM_RUN, w)
        grid = (bsz, 4, ntile // qb)
        at = lambda c: (lambda b, r, n: (b, n, r, 0, 0, c))
    else:
        layout = "seqs"
        n_seq = DSA_SEQS_PER_STEP
        assert ntile == 4 * qb
        view = lambda x: x.reshape(bsz, ntile, 16, PERM_RUN, x.shape[-1])
        blk = lambda w: (None, ntile, n_seq, PERM_RUN, w)
        grid = (bsz, 16 // n_seq, 1)
        at = lambda c: (lambda b, r, n: (b, 0, r, 0, c))
    qv, kvv = view(q), view(kv)
    o, lse = pl.pallas_call(
        functools.partial(_dsa_kernel, dil=d, layout=layout, qb=qb, n_seq=n_seq,
                          slopes=_alibi_slopes(ATT_HEADS)),
        grid=grid,
        in_specs=[
            pl.BlockSpec(blk(ATT_WIDTH), at(q_col)),
            pl.BlockSpec(blk(ATT_WIDTH), at(0)),
            pl.BlockSpec(blk(ATT_WIDTH), at(1)),
        ],
        out_specs=[pl.BlockSpec(blk(ATT_WIDTH), at(0)),
                   pl.BlockSpec(blk(LANES), at(0))],
        out_shape=[jax.ShapeDtypeStruct(qv.shape[:-1] + (ATT_WIDTH,), BF16),
                   jax.ShapeDtypeStruct(qv.shape[:-1] + (LANES,), F32)],
        scratch_shapes=[pltpu.VMEM((ATT_BLOCK, ATT_WIDTH), BF16),
                        pltpu.VMEM((ATT_BLOCK, ATT_WIDTH), BF16),
                        pltpu.VMEM((ATT_HEADS, ATT_BLOCK, 2 * ATT_BLOCK), F32)],
        compiler_params=pltpu.CompilerParams(
            dimension_semantics=("arbitrary", "arbitrary", "arbitrary")),
        name=f"dsa_branch{g}",
    )(qv, kvv, kvv)
    return o.reshape(t, ATT_WIDTH), lse.reshape(t, LANES)


def _merge_kernel(qt_ref, o0_ref, o1_ref, o2_ref, l0_ref, l1_ref, l2_ref, o_ref):
    qt = qt_ref[...]
    every = slice(None)
    for base, hf in [(t * PERM_TILE, hf) for t in range(o_ref.shape[0] // PERM_TILE)
                     for hf in range(2)]:
        rows = slice(base + hf * PERM_HALF, base + (hf + 1) * PERM_HALF)
        l0 = l0_ref[rows, :]
        l12 = _dot01(qt, jnp.concatenate([_gather_half(l1_ref, base, hf, every),
                                          _gather_half(l2_ref, base, hf, every)], axis=1))
        l1, l2 = l12[:, :LANES], l12[:, LANES:]
        m = jnp.maximum(jnp.maximum(l0, l1), l2)
        e0, e1, e2 = jnp.exp(l0 - m), jnp.exp(l1 - m), jnp.exp(l2 - m)
        den = e0 + e1 + e2
        w1, w2 = e1 / den, e2 / den
        for hp in range(ATT_HEADS // 2):
            pair = slice(2 * hp * HEAD_DIM, (2 * hp + 2) * HEAD_DIM)
            o1 = jnp.dot(qt, _gather_half(o1_ref, base, hf, pair), preferred_element_type=F32)
            o2 = jnp.dot(qt, _gather_half(o2_ref, base, hf, pair), preferred_element_type=F32)
            for k in range(2):
                h = 2 * hp + k
                sl = slice(h * HEAD_DIM, (h + 1) * HEAD_DIM)
                in_pair = slice(k * HEAD_DIM, (k + 1) * HEAD_DIM)
                o0 = o0_ref[rows, sl].astype(F32)
                o = (o0 + w1[:, h:h + 1] * (o1[:, in_pair] - o0)
                     + w2[:, h:h + 1] * (o2[:, in_pair] - o0))
                o_ref[rows, sl] = o.astype(o_ref.dtype)


def _dsa_merge(outs, lses, perm_t, tm=1024):
    t, hd = outs[0].shape
    assert t % tm == 0 and tm % PERM_TILE == 0
    o_spec = pl.BlockSpec((tm, hd), lambda i: (i, 0))
    l_spec = pl.BlockSpec((tm, LANES), lambda i: (i, 0))
    return pl.pallas_call(
        _merge_kernel,
        grid=(t // tm,),
        in_specs=[pl.BlockSpec((PERM_HALF, PERM_HALF), lambda i: (0, 0)),
                  o_spec, o_spec, o_spec, l_spec, l_spec, l_spec],
        out_specs=o_spec,
        out_shape=jax.ShapeDtypeStruct((t, hd), BF16),
        compiler_params=pltpu.CompilerParams(dimension_semantics=("arbitrary",)),
        name="dsa_merge",
    )(perm_t, *outs, *lses)


def _ffn_up_kernel(x_ref, wu_ref, wg_ref, cw_ref, cb_ref, wd_ref, o_ref, wdb_ref, wub_ref, wgb_ref,
                   us_ref, gs_ref, carry_ref, *, tm, tn, s_len):
    wdb_ref[...] = wd_ref[...].astype(BF16)

    @pl.when(pl.program_id(1) == 0)
    def _():
        wub_ref[...] = wu_ref[...].astype(BF16)
        wgb_ref[...] = wg_ref[...].astype(BF16)

    @pl.when((pl.program_id(1) * tm) % s_len == 0)
    def _():
        carry_ref[...] = jnp.zeros_like(carry_ref)

    n_rb = tm // FFN_ROW_SUB
    subs = [(cb, rb) for cb in range(tn // FFN_COL_SUB) for rb in range(n_rb)]

    def window(idx):
        cb, rb = subs[idx]
        return (slice(rb * FFN_ROW_SUB, (rb + 1) * FFN_ROW_SUB),
                slice(cb * FFN_COL_SUB, (cb + 1) * FFN_COL_SUB))

    def project(idx):
        rows, cols = window(idx)
        slot = idx % FFN_SLOTS
        xr = x_ref[rows, :]
        us_ref[slot] = jnp.dot(xr, wub_ref[:, cols], preferred_element_type=F32)
        gs_ref[slot, 8:] = jnp.dot(xr, wgb_ref[:, cols], preferred_element_type=F32)
        if subs[idx][1] == 0:
            gs_ref[slot, :8] = carry_ref[:, cols]
        else:
            gs_ref[slot, :8] = gs_ref[(idx - 1) % FFN_SLOTS, FFN_ROW_SUB:]
        if subs[idx][1] == n_rb - 1:
            carry_ref[:, cols] = gs_ref[slot, FFN_ROW_SUB:]

    def epilogue(idx):
        rows, cols = window(idx)
        slot = idx % FFN_SLOTS
        g = gs_ref[slot, 8:]
        g1 = gs_ref[slot, 7:7 + FFN_ROW_SUB]
        g2 = gs_ref[slot, 6:6 + FFN_ROW_SUB]
        cw = 0.5 * cw_ref[:, cols]
        half = cw[0:1] * g2 + cw[1:2] * g1 + cw[2:3] * g + 0.5 * cb_ref[:, cols]
        act = half * (1.0 + lax.erf(half * (2.0 ** 0.5)))
        o_ref[rows, cols] = (act * us_ref[slot]).astype(o_ref.dtype)

    ahead = FFN_SLOTS - 1
    for idx in range(ahead):
        project(idx)
    for idx in range(len(subs)):
        if idx + ahead < len(subs):
            project(idx + ahead)
        epilogue(idx)


def _ffn_up(xn, w_up, layer, conv_w, conv_b, w_down, s_len, tm=2048, tn=512):
    m, k = xn.shape
    nn = D_FF // tn
    n_steps = nn * (m // tm)
    slab = D_FF // n_steps
    assert s_len % tm == 0 and tm % FFN_ROW_SUB == 0 and tn % FFN_COL_SUB == 0
    assert slab * n_steps == D_FF and slab % 16 == 0
    d_out = w_down.shape[2]
    step = lambda j, i: j * (m // tm) + i
    return pl.pallas_call(
        functools.partial(_ffn_up_kernel, tm=tm, tn=tn, s_len=s_len),
        grid=(nn, m // tm),
        in_specs=[
            pl.BlockSpec((tm, k), lambda j, i: (i, 0)),
            pl.BlockSpec((None, k, tn), lambda j, i: (layer, 0, j)),
            pl.BlockSpec((None, k, tn), lambda j, i: (layer, 0, nn + j)),
            pl.BlockSpec((3, tn), lambda j, i: (0, j)),
            pl.BlockSpec((1, tn), lambda j, i: (0, j)),
            pl.BlockSpec((None, slab, d_out), lambda j, i: (layer, step(j, i), 0)),
        ],
        out_specs=[pl.BlockSpec((tm, tn), lambda j, i: (i, j)),
                   pl.BlockSpec((None, slab, d_out), lambda j, i: (0, step(j, i), 0))],
        out_shape=[jax.ShapeDtypeStruct((m, D_FF), BF16),
                   jax.ShapeDtypeStruct((1, D_FF, d_out), BF16)],
        scratch_shapes=[pltpu.VMEM((k, tn), BF16), pltpu.VMEM((k, tn), BF16),
                        pltpu.VMEM((FFN_SLOTS, FFN_ROW_SUB, FFN_COL_SUB), F32),
                        pltpu.VMEM((FFN_SLOTS, 8 + FFN_ROW_SUB, FFN_COL_SUB), F32),
                        pltpu.VMEM((8, tn), F32)],
        compiler_params=pltpu.CompilerParams(
            dimension_semantics=("arbitrary", "arbitrary")),
        name="ffn_up",
    )(xn, w_up, w_up, conv_w, conv_b.reshape(1, D_FF), w_down)


def _conv_glu(h, xn, w_up, layer, conv_w, conv_b, w_down, s_len):
    act, w_down_bf16 = _ffn_up(xn, w_up, layer, conv_w, conv_b, w_down, s_len)
    return _matmul(act, w_down_bf16, 0, F32, residual=h, tn=1024)


def kernel(x, attn_norm, gla_w_in, gla_w_a2, gla_b_a2, gla_head_norm, gla_w_out, kv_norm, w_kv,
           dsa_w_q, dsa_w_out, ffn_norm, ffn_w_up, ffn_conv_w, ffn_conv_b, ffn_w_down, final_norm):
    bsz, s_len, d = x.shape
    t = bsz * s_len
    h = x.reshape(t, d)

    n_main = 2 * GLA_KEY_DIM + 2 * GLA_VAL_DIM
    w_in = gla_w_in.astype(BF16)
    w_a1 = jnp.pad(w_in[0, :, n_main:], ((0, 0), (0, LANES - GATE_RANK)))
    w_a2 = jnp.pad(gla_w_a2[0].astype(BF16), ((0, LANES - GATE_RANK), (0, 0)))
    xn, a = _rmsnorm_gate_in(h, attn_norm[0], w_a1)
    proj = _matmul(xn, w_in, 0, BF16, n_cols=n_main, tm=1024, tn=2048)
    o = _gla_recurrence(proj, a, w_a2, gla_b_a2[0], gla_head_norm[0], bsz, s_len)
    h, xn = _matmul_res_norm(o, gla_w_out, 0, h, ffn_norm[0])
    h = _conv_glu(h, xn, ffn_w_up, 0, ffn_conv_w[0], ffn_conv_b[0], ffn_w_down, s_len)

    perm_np = _half_perm_matrix()
    perm = jnp.asarray(perm_np, BF16)
    perm_t = jnp.asarray(perm_np.T, BF16)
    xkv, xq, xq_perm = _rmsnorm2_perm(h, kv_norm, attn_norm[1], perm)
    kv, kv_perm = _matmul_and_permuted(xkv, w_kv[None], 0, perm)

    q_scale = HEAD_DIM ** -0.5 * LOG2_E
    q0 = _matmul(xq, dsa_w_q, 0, BF16, n_cols=ATT_WIDTH, out_scale=q_scale, tm=1024)
    q12 = _matmul(xq_perm, dsa_w_q, 0, BF16, n_cols=2 * ATT_WIDTH, col_start=ATT_WIDTH,
                  out_scale=q_scale, tm=1024)
    o0, lse0 = _dsa_branch(q0, 0, kv, 0, bsz, s_len)
    o1, lse1 = _dsa_branch(q12, 0, kv_perm, 1, bsz, s_len)
    o2, lse2 = _dsa_branch(q12, 1, kv_perm, 2, bsz, s_len)
    o = _dsa_merge([o0, o1, o2], [lse0, lse1, lse2], perm_t)
    h, xn = _matmul_res_norm(o, dsa_w_out, 0, h, ffn_norm[1])
    h = _conv_glu(h, xn, ffn_w_up, 1, ffn_conv_w[1], ffn_conv_b[1], ffn_w_down, s_len)

    return _rmsnorm(h, final_norm, F32).reshape(bsz, s_len, d)
```

```python
import functools
import math

import numpy as np
import jax
import jax.numpy as jnp
from jax import lax
from jax.experimental import pallas as pl
from jax.experimental.pallas import tpu as pltpu

D_MODEL = 2048
GLA_HEADS = 4
GLA_KEY_DIM = 1024
GLA_VAL_DIM = 2048
GLA_DK = 256
GLA_DV = 512
GATE_RANK = 16
GATE_NORMALIZER = 16.0
GLA_CHUNK = 64
GLA_BLOCK = 256
GLA_HEADS_PER_STEP = 4
ATT_HEADS = 16
HEAD_DIM = 128
ATT_WIDTH = ATT_HEADS * HEAD_DIM
WINDOWS = (128, 512, 2048)
DILATIONS = (1, 4, 16)
ATT_BLOCK = 128
DSA_BLOCKS_PER_STEP = 4
DSA_SEQS_PER_STEP = 2
QK_LOOKAHEAD = 2
PERM_TILE = 512
PERM_RUN = PERM_TILE // 16
PERM_HALF = PERM_TILE // 2
HALF_RUN = PERM_RUN // 2
D_FF = 5632
EPS = 1e-6
LOG2_E = math.log2(math.e)
LN_2 = math.log(2.0)
LANES = 128
FFN_ROW_SUB = 512
FFN_COL_SUB = 256
FFN_SLOTS = 2

F32 = jnp.float32
BF16 = jnp.bfloat16


def _alibi_slopes(n):
    def pow2_slopes(m):
        start = 2.0 ** (-8.0 / m)
        return [start ** (i + 1) for i in range(m)]
    assert math.log2(n).is_integer()
    return [float(v) for v in np.array(pow2_slopes(n), dtype=np.float32)]


def _rmsnorm_kernel(x_ref, g_ref, o_ref):
    x = x_ref[...]
    ms = jnp.mean(x * x, axis=-1, keepdims=True)
    o_ref[...] = ((x * lax.rsqrt(ms + EPS)) * g_ref[...]).astype(o_ref.dtype)


def _rmsnorm(x, g, out_dtype, tm=1024):
    m, d = x.shape
    return pl.pallas_call(
        _rmsnorm_kernel,
        grid=(m // tm,),
        in_specs=[pl.BlockSpec((tm, d), lambda i: (i, 0)),
                  pl.BlockSpec((1, d), lambda i: (0, 0))],
        out_specs=pl.BlockSpec((tm, d), lambda i: (i, 0)),
        out_shape=jax.ShapeDtypeStruct((m, d), out_dtype),
        compiler_params=pltpu.CompilerParams(dimension_semantics=("arbitrary",)),
        name="rmsnorm",
    )(x, g.reshape(1, d))


def _rmsnorm_streamed(x, g, out_dtype, tm=512, in_buffers=3):
    m, d = x.shape
    assert m % tm == 0

    def outer(x_hbm, g_ref, o_hbm):
        def body(x_ref, o_ref):
            _rmsnorm_kernel(x_ref, g_ref, o_ref)

        pltpu.emit_pipeline(
            body,
            grid=(m // tm,),
            in_specs=[pl.BlockSpec((tm, d), lambda i: (i, 0),
                                   pipeline_mode=pl.Buffered(in_buffers))],
            out_specs=[pl.BlockSpec((tm, d), lambda i: (i, 0))],
        )(x_hbm, o_hbm)

    return pl.pallas_call(
        outer,
        in_specs=[pl.BlockSpec(memory_space=pl.ANY),
                  pl.BlockSpec(memory_space=pltpu.VMEM)],
        out_specs=pl.BlockSpec(memory_space=pl.ANY),
        out_shape=jax.ShapeDtypeStruct((m, d), out_dtype),
        name="rmsnorm_streamed",
    )(x, g.reshape(1, d))


def _cast_weights_once(w_ref, wb_ref):
    @pl.when(pl.program_id(1) == 0)
    def _():
        wb_ref[...] = w_ref[...].astype(BF16)


def _mm_kernel(x_ref, w_ref, *rest, out_scale, has_residual, cast_weights):
    o_ref = rest[1 if has_residual else 0]
    if cast_weights:
        _cast_weights_once(w_ref, rest[-1])
        w_ref = rest[-1]
    acc = jnp.dot(x_ref[...], w_ref[...], preferred_element_type=F32)
    if has_residual:
        acc = rest[0][...] + acc
    if out_scale is not None:
        acc = acc * out_scale
    o_ref[...] = acc.astype(o_ref.dtype)


def _matmul(x, w, layer, out_dtype, n_cols=None, col_start=0, residual=None, out_scale=None,
            tm=512, tn=1024):
    m, k = x.shape
    cast_weights = w.dtype != BF16
    n = w.shape[2] if n_cols is None else n_cols
    assert w.shape[1] == k and m % tm == 0 and n % tn == 0 and col_start % tn == 0
    j0 = col_start // tn
    in_specs = [pl.BlockSpec((tm, k), lambda j, i: (i, 0)),
                pl.BlockSpec((None, k, tn), lambda j, i: (layer, 0, j0 + j))]
    args = [x, w]
    if residual is not None:
        in_specs.append(pl.BlockSpec((tm, tn), lambda j, i: (i, j)))
        args.append(residual)
    return pl.pallas_call(
        functools.partial(_mm_kernel, out_scale=out_scale, has_residual=residual is not None,
                          cast_weights=cast_weights),
        grid=(n // tn, m // tm),
        in_specs=in_specs,
        out_specs=pl.BlockSpec((tm, tn), lambda j, i: (i, j)),
        out_shape=jax.ShapeDtypeStruct((m, n), out_dtype),
        scratch_shapes=[pltpu.VMEM((k, tn), BF16)] if cast_weights else [],
        compiler_params=pltpu.CompilerParams(
            dimension_semantics=("arbitrary", "arbitrary")),
        name="matmul",
    )(*args)


def _mm_perm_kernel(x_ref, w_ref, p_ref, o_ref, op_ref, wb_ref):
    _cast_weights_once(w_ref, wb_ref)
    out = jnp.dot(x_ref[...], wb_ref[...], preferred_element_type=F32).astype(BF16)
    o_ref[...] = out
    for t in range(o_ref.shape[0] // PERM_TILE):
        _store_permuted(p_ref[...], out[t * PERM_TILE:(t + 1) * PERM_TILE], op_ref, t * PERM_TILE)


def _matmul_and_permuted(x, w, layer, perm, tm=1024, tn=1024):
    m, k = x.shape
    n = w.shape[2]
    assert w.shape[1] == k and m % tm == 0 and n % tn == 0 and tm % PERM_TILE == 0
    out_spec = pl.BlockSpec((tm, tn), lambda j, i: (i, j))
    out = jax.ShapeDtypeStruct((m, n), BF16)
    return pl.pallas_call(
        _mm_perm_kernel,
        grid=(n // tn, m // tm),
        in_specs=[pl.BlockSpec((tm, k), lambda j, i: (i, 0)),
                  pl.BlockSpec((None, k, tn), lambda j, i: (layer, 0, j)),
                  pl.BlockSpec((PERM_HALF, PERM_HALF), lambda j, i: (0, 0))],
        out_specs=[out_spec, out_spec],
        out_shape=[out, out],
        scratch_shapes=[pltpu.VMEM((k, tn), BF16)],
        compiler_params=pltpu.CompilerParams(
            dimension_semantics=("arbitrary", "arbitrary")),
        name="matmul_and_permuted",
    )(x, w, perm)


def _mm_res_norm_kernel(x_ref, w_ref, r_ref, g_ref, h_ref, xn_ref, wb_ref):
    @pl.when(pl.program_id(0) == 0)
    def _():
        wb_ref[...] = w_ref[...].astype(BF16)

    h = r_ref[...] + jnp.dot(x_ref[...], wb_ref[...], preferred_element_type=F32)
    h_ref[...] = h
    ms = jnp.mean(h * h, axis=-1, keepdims=True)
    xn_ref[...] = ((h * lax.rsqrt(ms + EPS)) * g_ref[...]).astype(xn_ref.dtype)


def _matmul_res_norm(x, w, layer, residual, norm_g, tm=512):
    m, k = x.shape
    n = w.shape[2]
    assert w.shape[1] == k and residual.shape == (m, n) and m % tm == 0
    row = pl.BlockSpec((tm, n), lambda i: (i, 0))
    return pl.pallas_call(
        _mm_res_norm_kernel,
        grid=(m // tm,),
        in_specs=[pl.BlockSpec((tm, k), lambda i: (i, 0)),
                  pl.BlockSpec((None, k, n), lambda i: (layer, 0, 0),
                               pipeline_mode=pl.Buffered(1)),
                  row,
                  pl.BlockSpec((1, n), lambda i: (0, 0))],
        out_specs=[row, row],
        out_shape=[jax.ShapeDtypeStruct((m, n), F32), jax.ShapeDtypeStruct((m, n), BF16)],
        scratch_shapes=[pltpu.VMEM((k, n), BF16)],
        compiler_params=pltpu.CompilerParams(dimension_semantics=("arbitrary",)),
        name="matmul_res_norm",
    )(x, w, residual, norm_g.reshape(1, n))


def _norm2_perm_kernel(x_ref, ga_ref, gb_ref, p_ref, a_ref, b_ref, bp_ref):
    x = x_ref[...]
    ms = jnp.mean(x * x, axis=-1, keepdims=True)
    y = x * lax.rsqrt(ms + EPS)
    a_ref[...] = (y * ga_ref[...]).astype(BF16)
    b = (y * gb_ref[...]).astype(BF16)
    b_ref[...] = b
    for t in range(b_ref.shape[0] // PERM_TILE):
        _store_permuted(p_ref[...], b[t * PERM_TILE:(t + 1) * PERM_TILE], bp_ref, t * PERM_TILE)


def _rmsnorm2_perm(x, g_a, g_b, perm, tm=1024):
    m, d = x.shape
    assert m % tm == 0 and tm % PERM_TILE == 0
    row = pl.BlockSpec((tm, d), lambda i: (i, 0))
    vec = pl.BlockSpec((1, d), lambda i: (0, 0))
    out = jax.ShapeDtypeStruct((m, d), BF16)
    return pl.pallas_call(
        _norm2_perm_kernel,
        grid=(m // tm,),
        in_specs=[row, vec, vec, pl.BlockSpec((PERM_HALF, PERM_HALF), lambda i: (0, 0))],
        out_specs=[row, row, row],
        out_shape=[out, out, out],
        compiler_params=pltpu.CompilerParams(dimension_semantics=("arbitrary",)),
        name="rmsnorm2_perm",
    )(x, g_a.reshape(1, d), g_b.reshape(1, d), perm)


def _norm_gate_in_kernel(x_ref, g_ref, w1_ref, xn_ref, a_ref):
    x = x_ref[...]
    ms = jnp.mean(x * x, axis=-1, keepdims=True)
    xn = ((x * lax.rsqrt(ms + EPS)) * g_ref[...]).astype(BF16)
    xn_ref[...] = xn
    a_ref[...] = jnp.dot(xn, w1_ref[...], preferred_element_type=F32).astype(BF16)


def _rmsnorm_gate_in(x, g, w_a1, tm=1024):
    m, d = x.shape
    return pl.pallas_call(
        _norm_gate_in_kernel,
        grid=(m // tm,),
        in_specs=[pl.BlockSpec((tm, d), lambda i: (i, 0)),
                  pl.BlockSpec((1, d), lambda i: (0, 0)),
                  pl.BlockSpec((d, LANES), lambda i: (0, 0))],
        out_specs=[pl.BlockSpec((tm, d), lambda i: (i, 0)),
                   pl.BlockSpec((tm, LANES), lambda i: (i, 0))],
        out_shape=[jax.ShapeDtypeStruct((m, d), BF16), jax.ShapeDtypeStruct((m, LANES), BF16)],
        compiler_params=pltpu.CompilerParams(dimension_semantics=("arbitrary",)),
        name="rmsnorm_gate_in",
    )(x, g.reshape(1, d), w_a1)


def _split3(x):
    hi = x.astype(BF16)
    r1 = x - hi.astype(F32)
    mid = r1.astype(BF16)
    lo = (r1 - mid.astype(F32)).astype(BF16)
    return hi, mid, lo


def _dot01(mat01, x):
    hi, mid, lo = _split3(x)
    return (jnp.dot(mat01, hi, preferred_element_type=F32)
            + jnp.dot(mat01, mid, preferred_element_type=F32)
            + jnp.dot(mat01, lo, preferred_element_type=F32))


def _gla_kernel(q_ref, k_ref, v_ref, r_ref, a_ref, w2_ref, b2_ref, hn_ref, o_ref, st_ref):
    @pl.when(pl.program_id(2) == 0)
    def _():
        st_ref[...] = jnp.zeros_like(st_ref)

    nb = GLA_BLOCK
    heads = range(GLA_HEADS_PER_STEP)
    dk = lambda h: slice(h * GLA_DK, (h + 1) * GLA_DK)
    dv = lambda h: slice(h * GLA_DV, (h + 1) * GLA_DV)
    row = lax.broadcasted_iota(jnp.int32, (nb, nb), 0)
    col = lax.broadcasted_iota(jnp.int32, (nb, nb), 1)
    same_chunk = (row // GLA_CHUNK) == (col // GLA_CHUNK)
    causal = same_chunk & (col <= row)
    tri = jnp.where(causal, 1.0, 0.0).astype(BF16)
    nt = (((1,), (1,)), ((), ()))
    tn = (((0,), (0,)), ((), ()))

    a = a_ref[...]
    cum = []
    for h in heads:
        z = jnp.dot(a, w2_ref[:, dk(h)], preferred_element_type=F32) + b2_ref[:, dk(h)]
        log_sig = jnp.minimum(z, 0.0) - jnp.log(1.0 + jnp.exp(-jnp.abs(z)))
        log2_alpha = log_sig * (LOG2_E / GATE_NORMALIZER)
        cum.append(_dot01(tri, log2_alpha))
    k = [k_ref[:, dk(h)].astype(F32) for h in heads]
    q_dec = [((q_ref[:, dk(h)].astype(F32) * (GLA_DK ** -0.5)) * jnp.exp2(cum[h])).astype(BF16)
             for h in heads]
    k_inv = [(k[h] * jnp.exp2(-cum[h])).astype(BF16) for h in heads]
    scores = [lax.dot_general(q_dec[h], k_inv[h], nt, preferred_element_type=F32)
              for h in heads]
    scores = [jnp.where(causal, s, 0.0).astype(BF16) for s in scores]
    o_intra = [jnp.dot(scores[h], v_ref[:, dv(h)], preferred_element_type=F32)
               for h in heads]

    o_parts = [[] for _ in heads]
    for c in range(nb // GLA_CHUNK):
        rows = slice(c * GLA_CHUNK, (c + 1) * GLA_CHUNK)
        for h in heads:
            last_c = cum[h][(c + 1) * GLA_CHUNK - 1:(c + 1) * GLA_CHUNK]
            k_end = (k[h][rows] * jnp.exp2(last_c - cum[h][rows])).astype(BF16)
            st = st_ref[h]
            o_inter = lax.dot_general(q_dec[h][rows], st.astype(BF16), nt,
                                      preferred_element_type=F32)
            upd = lax.dot_general(v_ref[rows, dv(h)], k_end, tn,
                                  preferred_element_type=F32)
            st_ref[h] = st * jnp.exp2(last_c) + upd
            o_parts[h].append(o_intra[h][rows] + o_inter)

    for h in heads:
        o = jnp.concatenate(o_parts[h], axis=0)
        ms = jnp.mean(o * o, axis=-1, keepdims=True)
        o = (o * lax.rsqrt(ms + EPS)) * hn_ref[...]
        r = r_ref[:, dv(h)].astype(F32)
        gate = r * (1.0 / (1.0 + jnp.exp(-r)))
        o_ref[:, dv(h)] = (o * gate).astype(o_ref.dtype)


def _gla_recurrence(proj, a, w_a2, b_a2, head_norm, bsz, s_len):
    t = bsz * s_len
    nblk = s_len // GLA_BLOCK
    hps = GLA_HEADS_PER_STEP
    wk, wv = hps * GLA_DK, hps * GLA_DV
    k_off = GLA_KEY_DIM // wk
    v_off = 2 * GLA_KEY_DIM // wv
    r_off = (2 * GLA_KEY_DIM + GLA_VAL_DIM) // wv
    row = lambda b, g, i: b * nblk + i
    return pl.pallas_call(
        _gla_kernel,
        grid=(bsz, GLA_HEADS // hps, nblk),
        in_specs=[
            pl.BlockSpec((GLA_BLOCK, wk), lambda b, g, i: (row(b, g, i), g)),
            pl.BlockSpec((GLA_BLOCK, wk), lambda b, g, i: (row(b, g, i), k_off + g)),
            pl.BlockSpec((GLA_BLOCK, wv), lambda b, g, i: (row(b, g, i), v_off + g)),
            pl.BlockSpec((GLA_BLOCK, wv), lambda b, g, i: (row(b, g, i), r_off + g)),
            pl.BlockSpec((GLA_BLOCK, LANES), lambda b, g, i: (row(b, g, i), 0)),
            pl.BlockSpec((LANES, wk), lambda b, g, i: (0, g)),
            pl.BlockSpec((1, wk), lambda b, g, i: (0, g)),
            pl.BlockSpec((1, GLA_DV), lambda b, g, i: (0, 0)),
        ],
        out_specs=pl.BlockSpec((GLA_BLOCK, wv), lambda b, g, i: (row(b, g, i), g)),
        out_shape=jax.ShapeDtypeStruct((t, GLA_VAL_DIM), BF16),
        scratch_shapes=[pltpu.VMEM((hps, GLA_DV, GLA_DK), F32)],
        compiler_params=pltpu.CompilerParams(
            dimension_semantics=("arbitrary", "arbitrary", "arbitrary")),
        name="gla_recurrence",
    )(proj, proj, proj, proj, a, w_a2, b_a2.reshape(1, GLA_KEY_DIM), head_norm.reshape(1, GLA_DV))


def _half_perm_matrix():
    q = np.zeros((PERM_HALF, PERM_HALF), np.float32)
    for i in range(HALF_RUN):
        for a in range(4):
            for r4 in range(4):
                q[(4 * r4 + a) * HALF_RUN + i, 16 * i + 4 * a + r4] = 1.0
    return q


def _store_permuted(q, x, out_ref, base):
    for hf in range(2):
        y = jnp.dot(q, x[hf * PERM_HALF:(hf + 1) * PERM_HALF],
                    preferred_element_type=F32).astype(out_ref.dtype)
        for grp in range(16):
            dst = base + grp * PERM_RUN + hf * HALF_RUN
            out_ref[dst:dst + HALF_RUN, :] = y[grp * HALF_RUN:(grp + 1) * HALF_RUN]


def _gather_half(ref, base, hf, cols):
    return jnp.concatenate(
        [ref[base + grp * PERM_RUN + hf * HALF_RUN:base + grp * PERM_RUN + (hf + 1) * HALF_RUN, cols]
         for grp in range(16)], axis=0)


def _dsa_lead(layout, s, u):
    if layout == "rows":
        return (slice(s * ATT_BLOCK, (s + 1) * ATT_BLOCK),)
    if layout == "tile":
        return (s, slice(None), slice(None))
    assert layout == "seqs"
    return (slice(4 * s, 4 * s + 4), u, slice(None))


def _dsa_block_shape(layout, width):
    return (ATT_BLOCK, width) if layout == "rows" else (4, PERM_RUN, width)


def _dsa_kernel(q_ref, kc_ref, vc_ref, o_ref, lse_ref, kp_ref, vp_ref, bias_ref, *, dil, layout,
                qb, n_seq, slopes):
    n = pl.program_id(2)
    nk = 2 * ATT_BLOCK
    ck = lax.broadcasted_iota(jnp.int32, (ATT_BLOCK, nk), 1)
    is_cur = ck >= ATT_BLOCK

    @pl.when(n == 0)
    def _():
        kp_ref[...] = jnp.zeros_like(kp_ref)
        vp_ref[...] = jnp.zeros_like(vp_ref)

    @pl.when((pl.program_id(0) == 0) & (pl.program_id(1) == 0) & (n == 0))
    def _():
        rq = lax.broadcasted_iota(jnp.int32, (ATT_BLOCK, nk), 0)
        ck_in = ck & (ATT_BLOCK - 1)
        if layout == "tile":
            sub_q = 4 * (rq & (PERM_RUN - 1)) + (rq >> 5)
            sub_k = 4 * (ck_in & (PERM_RUN - 1)) + (ck_in >> 5)
        else:
            sub_q, sub_k = rq, ck_in
        j = sub_q - sub_k + jnp.where(is_cur, 0, ATT_BLOCK)
        valid = (j >= 0) & (j <= ATT_BLOCK)
        neg_dist = jnp.where(valid, -(j * dil).astype(F32), -jnp.inf)
        for h in range(ATT_HEADS):
            bias_ref[h] = (slopes[h] * LOG2_E) * neg_dist

    has_prev = is_cur | (n > 0)

    lane = lax.broadcasted_iota(jnp.int32, (ATT_BLOCK, LANES), 1)
    ones = jnp.ones((nk, HEAD_DIM), BF16)
    nt = (((1,), (1,)), ((), ()))

    def head(ref, s, u, h):
        x = ref[_dsa_lead(layout, s, u) + (slice(h * HEAD_DIM, (h + 1) * HEAD_DIM),)]
        return x.reshape(ATT_BLOCK, HEAD_DIM)

    def prev_head(ref, carry_ref, s, u, h):
        if s == 0:
            return carry_ref[:, h * HEAD_DIM:(h + 1) * HEAD_DIM]
        return head(ref, s - 1, u, h)

    for u, s in [(u, s) for u in range(n_seq) for s in range(qb)]:
        lead = _dsa_lead(layout, s, u)

        def scores(h, s=s, u=u):
            kcat = jnp.concatenate([prev_head(kc_ref, kp_ref, s, u, h), head(kc_ref, s, u, h)],
                                   axis=0)
            return lax.dot_general(head(q_ref, s, u, h), kcat, nt, preferred_element_type=F32)

        pending = [scores(h) for h in range(QK_LOOKAHEAD)]
        lse_tile = jnp.zeros((ATT_BLOCK, LANES), F32)
        for h in range(ATT_HEADS):
            sc = pending.pop(0)
            if h + QK_LOOKAHEAD < ATT_HEADS:
                pending.append(scores(h + QK_LOOKAHEAD))
            bias = bias_ref[h]
            if s == 0:
                bias = jnp.where(has_prev, bias, -jnp.inf)
            z = sc + bias
            m = jnp.max(z, axis=-1, keepdims=True)
            p = jnp.exp2(z - m).astype(BF16)
            vcat = jnp.concatenate([prev_head(vc_ref, vp_ref, s, u, h), head(vc_ref, s, u, h)],
                                   axis=0)
            acc = jnp.dot(p, jnp.concatenate([vcat, ones], axis=1),
                          preferred_element_type=F32)
            l = acc[:, HEAD_DIM:]
            o = (acc[:, :HEAD_DIM] / l).astype(o_ref.dtype)
            o_ref[lead + (slice(h * HEAD_DIM, (h + 1) * HEAD_DIM),)] = o.reshape(
                _dsa_block_shape(layout, HEAD_DIM))
            lse_tile = jnp.where(lane == h, m * LN_2 + jnp.log(l), lse_tile)
        lse_ref[lead + (slice(None),)] = lse_tile.reshape(_dsa_block_shape(layout, LANES))

    last = _dsa_lead(layout, qb - 1, n_seq - 1) + (slice(None),)
    kp_ref[...] = kc_ref[last].reshape(kp_ref.shape)
    vp_ref[...] = vc_ref[last].reshape(vp_ref.shape)


def _dsa_branch(q, q_col, kv, g, bsz, s_len):
    d = DILATIONS[g]
    assert WINDOWS[g] // d == ATT_BLOCK
    t = bsz * s_len
    ntile = s_len // PERM_TILE
    qb = min(DSA_BLOCKS_PER_STEP, s_len // d // ATT_BLOCK)
    n_seq = 1
    if g == 0:
        layout = "rows"
        view = lambda x: x.reshape(bsz, s_len, x.shape[-1])
        blk = lambda w: (None, qb * ATT_BLOCK, w)
        grid = (bsz, 1, s_len // (qb * ATT_BLOCK))
        at = lambda c: (lambda b, r, n: (b, n, c))
    elif g == 1:
        layout = "tile"
        view = lambda x: x.reshape(bsz, ntile, 4, 4, PERM_RUN, x.shape[-1])
        blk = lambda w: (None, qb, None, 4, PERM_RUN, w)
        grid = (bsz, 4, ntile // qb)
        at = lambda c: (lambda b, r, n: (b, n, r, 0, 0, c))
    else:
        layout = "seqs"
        n_seq = DSA_SEQS_PER_STEP
        assert ntile == 4 * qb
        view = lambda x: x.reshape(bsz, ntile, 16, PERM_RUN, x.shape[-1])
        blk = lambda w: (None, ntile, n_seq, PERM_RUN, w)
        grid = (bsz, 16 // n_seq, 1)
        at = lambda c: (lambda b, r, n: (b, 0, r, 0, c))
    qv, kvv = view(q), view(kv)
    o, lse = pl.pallas_call(
        functools.partial(_dsa_kernel, dil=d, layout=layout, qb=qb, n_seq=n_seq,
                          slopes=_alibi_slopes(ATT_HEADS)),
        grid=grid,
        in_specs=[
            pl.BlockSpec(blk(ATT_WIDTH), at(q_col)),
            pl.BlockSpec(blk(ATT_WIDTH), at(0)),
            pl.BlockSpec(blk(ATT_WIDTH), at(1)),
        ],
        out_specs=[pl.BlockSpec(blk(ATT_WIDTH), at(0)),
                   pl.BlockSpec(blk(LANES), at(0))],
        out_shape=[jax.ShapeDtypeStruct(qv.shape[:-1] + (ATT_WIDTH,), BF16),
                   jax.ShapeDtypeStruct(qv.shape[:-1] + (LANES,), F32)],
        scratch_shapes=[pltpu.VMEM((ATT_BLOCK, ATT_WIDTH), BF16),
                        pltpu.VMEM((ATT_BLOCK, ATT_WIDTH), BF16),
                        pltpu.VMEM((ATT_HEADS, ATT_BLOCK, 2 * ATT_BLOCK), F32)],
        compiler_params=pltpu.CompilerParams(
            dimension_semantics=("arbitrary", "arbitrary", "arbitrary")),
        name=f"dsa_branch{g}",
    )(qv, kvv, kvv)
    return o.reshape(t, ATT_WIDTH), lse.reshape(t, LANES)


def _merge_kernel(qt_ref, o0_ref, o1_ref, o2_ref, l0_ref, l1_ref, l2_ref, o_ref):
    qt = qt_ref[...]
    every = slice(None)
    for base, hf in [(t * PERM_TILE, hf) for t in range(o_ref.shape[0] // PERM_TILE)
                     for hf in range(2)]:
        rows = slice(base + hf * PERM_HALF, base + (hf + 1) * PERM_HALF)
        l0 = l0_ref[rows, :]
        l12 = _dot01(qt, jnp.concatenate([_gather_half(l1_ref, base, hf, every),
                                          _gather_half(l2_ref, base, hf, every)], axis=1))
        l1, l2 = l12[:, :LANES], l12[:, LANES:]
        m = jnp.maximum(jnp.maximum(l0, l1), l2)
        e0, e1, e2 = jnp.exp(l0 - m), jnp.exp(l1 - m), jnp.exp(l2 - m)
        den = e0 + e1 + e2
        w1, w2 = e1 / den, e2 / den
        for hp in range(ATT_HEADS // 2):
            pair = slice(2 * hp * HEAD_DIM, (2 * hp + 2) * HEAD_DIM)
            o1 = jnp.dot(qt, _gather_half(o1_ref, base, hf, pair), preferred_element_type=F32)
            o2 = jnp.dot(qt, _gather_half(o2_ref, base, hf, pair), preferred_element_type=F32)
            for k in range(2):
                h = 2 * hp + k
                sl = slice(h * HEAD_DIM, (h + 1) * HEAD_DIM)
                in_pair = slice(k * HEAD_DIM, (k + 1) * HEAD_DIM)
                o0 = o0_ref[rows, sl].astype(F32)
                o = (o0 + w1[:, h:h + 1] * (o1[:, in_pair] - o0)
                     + w2[:, h:h + 1] * (o2[:, in_pair] - o0))
                o_ref[rows, sl] = o.astype(o_ref.dtype)


def _dsa_merge(outs, lses, perm_t, tm=1024):
    t, hd = outs[0].shape
    assert t % tm == 0 and tm % PERM_TILE == 0
    o_spec = pl.BlockSpec((tm, hd), lambda i: (i, 0))
    l_spec = pl.BlockSpec((tm, LANES), lambda i: (i, 0))
    return pl.pallas_call(
        _merge_kernel,
        grid=(t // tm,),
        in_specs=[pl.BlockSpec((PERM_HALF, PERM_HALF), lambda i: (0, 0)),
                  o_spec, o_spec, o_spec, l_spec, l_spec, l_spec],
        out_specs=o_spec,
        out_shape=jax.ShapeDtypeStruct((t, hd), BF16),
        compiler_params=pltpu.CompilerParams(dimension_semantics=("arbitrary",)),
        name="dsa_merge",
    )(perm_t, *outs, *lses)


def _ffn_up_kernel(x_ref, wu_ref, wg_ref, cw_ref, cb_ref, wd_ref, o_ref, wdb_ref, wub_ref, wgb_ref,
                   us_ref, gs_ref, carry_ref, *, tm, tn, s_len):
    wdb_ref[...] = wd_ref[...].astype(BF16)

    @pl.when(pl.program_id(1) == 0)
    def _():
        wub_ref[...] = wu_ref[...].astype(BF16)
        wgb_ref[...] = wg_ref[...].astype(BF16)

    @pl.when((pl.program_id(1) * tm) % s_len == 0)
    def _():
        carry_ref[...] = jnp.zeros_like(carry_ref)

    n_rb = tm // FFN_ROW_SUB
    subs = [(cb, rb) for cb in range(tn // FFN_COL_SUB) for rb in range(n_rb)]

    def window(idx):
        cb, rb = subs[idx]
        return (slice(rb * FFN_ROW_SUB, (rb + 1) * FFN_ROW_SUB),
                slice(cb * FFN_COL_SUB, (cb + 1) * FFN_COL_SUB))

    def project(idx):
        rows, cols = window(idx)
        slot = idx % FFN_SLOTS
        xr = x_ref[rows, :]
        us_ref[slot] = jnp.dot(xr, wub_ref[:, cols], preferred_element_type=F32)
        gs_ref[slot, 8:] = jnp.dot(xr, wgb_ref[:, cols], preferred_element_type=F32)
        if subs[idx][1] == 0:
            gs_ref[slot, :8] = carry_ref[:, cols]
        else:
            gs_ref[slot, :8] = gs_ref[(idx - 1) % FFN_SLOTS, FFN_ROW_SUB:]
        if subs[idx][1] == n_rb - 1:
            carry_ref[:, cols] = gs_ref[slot, FFN_ROW_SUB:]

    def epilogue(idx):
        rows, cols = window(idx)
        slot = idx % FFN_SLOTS
        g = gs_ref[slot, 8:]
        g1 = gs_ref[slot, 7:7 + FFN_ROW_SUB]
        g2 = gs_ref[slot, 6:6 + FFN_ROW_SUB]
        cw = 0.5 * cw_ref[:, cols]
        half = cw[0:1] * g2 + cw[1:2] * g1 + cw[2:3] * g + 0.5 * cb_ref[:, cols]
        act = half * (1.0 + lax.erf(half * (2.0 ** 0.5)))
        o_ref[rows, cols] = (act * us_ref[slot]).astype(o_ref.dtype)

    ahead = FFN_SLOTS - 1
    for idx in range(ahead):
        project(idx)
    for idx in range(len(subs)):
        if idx + ahead < len(subs):
            project(idx + ahead)
        epilogue(idx)


def _ffn_up(xn, w_up, layer, conv_w, conv_b, w_down, s_len, tm=2048, tn=512):
    m, k = xn.shape
    nn = D_FF // tn
    n_steps = nn * (m // tm)
    slab = D_FF // n_steps
    assert s_len % tm == 0 and tm % FFN_ROW_SUB == 0 and tn % FFN_COL_SUB == 0
    assert slab * n_steps == D_FF and slab % 16 == 0
    d_out = w_down.shape[2]
    step = lambda j, i: j * (m // tm) + i
    return pl.pallas_call(
        functools.partial(_ffn_up_kernel, tm=tm, tn=tn, s_len=s_len),
        grid=(nn, m // tm),
        in_specs=[
            pl.BlockSpec((tm, k), lambda j, i: (i, 0)),
            pl.BlockSpec((None, k, tn), lambda j, i: (layer, 0, j)),
            pl.BlockSpec((None, k, tn), lambda j, i: (layer, 0, nn + j)),
            pl.BlockSpec((3, tn), lambda j, i: (0, j)),
            pl.BlockSpec((1, tn), lambda j, i: (0, j)),
            pl.BlockSpec((None, slab, d_out), lambda j, i: (layer, step(j, i), 0)),
        ],
        out_specs=[pl.BlockSpec((tm, tn), lambda j, i: (i, j)),
                   pl.BlockSpec((None, slab, d_out), lambda j, i: (0, step(j, i), 0))],
        out_shape=[jax.ShapeDtypeStruct((m, D_FF), BF16),
                   jax.ShapeDtypeStruct((1, D_FF, d_out), BF16)],
        scratch_shapes=[pltpu.VMEM((k, tn), BF16), pltpu.VMEM((k, tn), BF16),
                        pltpu.VMEM((FFN_SLOTS, FFN_ROW_SUB, FFN_COL_SUB), F32),
                        pltpu.VMEM((FFN_SLOTS, 8 + FFN_ROW_SUB, FFN_COL_SUB), F32),
                        pltpu.VMEM((8, tn), F32)],
        compiler_params=pltpu.CompilerParams(
            dimension_semantics=("arbitrary", "arbitrary")),
        name="ffn_up",
    )(xn, w_up, w_up, conv_w, conv_b.reshape(1, D_FF), w_down)


def _conv_glu(h, xn, w_up, layer, conv_w, conv_b, w_down, s_len):
    act, w_down_bf16 = _ffn_up(xn, w_up, layer, conv_w, conv_b, w_down, s_len)
    return _matmul(act, w_down_bf16, 0, F32, residual=h, tn=1024)


def kernel(x, attn_norm, gla_w_in, gla_w_a2, gla_b_a2, gla_head_norm, gla_w_out, kv_norm, w_kv,
           dsa_w_q, dsa_w_out, ffn_norm, ffn_w_up, ffn_conv_w, ffn_conv_b, ffn_w_down, final_norm):
    bsz, s_len, d = x.shape
    t = bsz * s_len
    h = x.reshape(t, d)

    n_main = 2 * GLA_KEY_DIM + 2 * GLA_VAL_DIM
    w_in = gla_w_in.astype(BF16)
    w_a1 = jnp.pad(w_in[0, :, n_main:], ((0, 0), (0, LANES - GATE_RANK)))
    w_a2 = jnp.pad(gla_w_a2[0].astype(BF16), ((0, LANES - GATE_RANK), (0, 0)))
    xn, a = _rmsnorm_gate_in(h, attn_norm[0], w_a1)
    proj = _matmul(xn, w_in, 0, BF16, n_cols=n_main, tm=1024, tn=2048)
    o = _gla_recurrence(proj, a, w_a2, gla_b_a2[0], gla_head_norm[0], bsz, s_len)
    h, xn = _matmul_res_norm(o, gla_w_out, 0, h, ffn_norm[0])
    h = _conv_glu(h, xn, ffn_w_up, 0, ffn_conv_w[0], ffn_conv_b[0], ffn_w_down, s_len)

    perm_np = _half_perm_matrix()
    perm = jnp.asarray(perm_np, BF16)
    perm_t = jnp.asarray(perm_np.T, BF16)
    xkv, xq, xq_perm = _rmsnorm2_perm(h, kv_norm, attn_norm[1], perm)
    kv, kv_perm = _matmul_and_permuted(xkv, w_kv[None], 0, perm)

    q_scale = HEAD_DIM ** -0.5 * LOG2_E
    q0 = _matmul(xq, dsa_w_q, 0, BF16, n_cols=ATT_WIDTH, out_scale=q_scale, tm=2048)
    q12 = _matmul(xq_perm, dsa_w_q, 0, BF16, n_cols=2 * ATT_WIDTH, col_start=ATT_WIDTH,
                  out_scale=q_scale, tm=2048)
    o0, lse0 = _dsa_branch(q0, 0, kv, 0, bsz, s_len)
    o1, lse1 = _dsa_branch(q12, 0, kv_perm, 1, bsz, s_len)
    o2, lse2 = _dsa_branch(q12, 1, kv_perm, 2, bsz, s_len)
    o = _dsa_merge([o0, o1, o2], [lse0, lse1, lse2], perm_t)
    h, xn = _matmul_res_norm(o, dsa_w_out, 0, h, ffn_norm[1])
    h = _conv_glu(h, xn, ffn_w_up, 1, ffn_conv_w[1], ffn_conv_b[1], ffn_w_down, s_len)

    return _rmsnorm_streamed(h, final_norm, F32).reshape(bsz, s_len, d)
```
